```python
import jax, jax.numpy as jnp
from jax import lax
import numpy as np

D_MODEL = 1024
BATCH = 1
SEQ = 16384
DEPTH = 2

N_META = 16
CONV_CH = 512
CONV_K = 31
ATT_HEADS = 8
HEAD_DIM = 64
ATT_WIDTH = ATT_HEADS * HEAD_DIM
IDX_HEADS = 8
IDX_DIM = 64
TOPK_MAX = 256
Q_BLOCK = 128
ROPE_THETA = 10000.0
AB_SPLIT_WIDTHS = (CONV_CH, CONV_CH, ATT_WIDTH, ATT_WIDTH, ATT_WIDTH, IDX_HEADS * IDX_DIM, IDX_DIM, IDX_HEADS)
RNN_WIDTH = 1280
RNN_BLOCKS = 10
RNN_BLOCK_W = RNN_WIDTH // RNN_BLOCKS
RNN_CONV_K = 4
RG_C = 8.0
N_GROUPS = 4
EXPERTS_PER_GROUP = 8
N_EXPERTS = N_GROUPS * EXPERTS_PER_GROUP
TOP_K_INNER = 2
D_EXPERT = 512
EXPERT_BLOCK = 128
LN_EPS = 1e-5
DN_ALPHA = (2 * DEPTH) ** 0.25
DN_BETA = (8 * DEPTH) ** -0.25
N_EVEN = (DEPTH + 1) // 2
N_ODD = DEPTH // 2

kernel_name = 'hybrid_conv_dsa_rglru_hmoe'


def layer_norm(x, g, b):
    xf = x.astype(jnp.float32)
    mu = jnp.mean(xf, axis=-1, keepdims=True)
    var = jnp.mean(jnp.square(xf - mu), axis=-1, keepdims=True)
    y = (xf - mu) * lax.rsqrt(var + LN_EPS) * g.astype(jnp.float32) + b.astype(jnp.float32)
    return y.astype(x.dtype)


def causal_depthwise_conv(x, w, b):
    width, ch = w.shape
    y = lax.conv_general_dilated(x, w[:, None, :].astype(x.dtype), window_strides=(1,),
                                 padding=[(width - 1, 0)], dimension_numbers=('NWC', 'WIO', 'NWC'),
                                 feature_group_count=ch)
    return y + b.astype(x.dtype)


def rope(t, pos):
    d = t.shape[-1]
    half = d // 2
    inv_freq = ROPE_THETA ** (-2.0 * jnp.arange(half, dtype=jnp.float32) / d)
    ang = pos.astype(jnp.float32)[:, None] * inv_freq[None, :]
    cos = jnp.cos(ang)[None, :, None, :]
    sin = jnp.sin(ang)[None, :, None, :]
    t1 = t[..., :half].astype(jnp.float32)
    t2 = t[..., half:].astype(jnp.float32)
    return jnp.concatenate([t1 * cos - t2 * sin, t1 * sin + t2 * cos], axis=-1).astype(t.dtype)


def sparse_index_attention(q, k, v, qi, ki, wi, topk):
    bsz, length = q.shape[:2]
    n_blocks = -(-length // Q_BLOCK)
    padded = n_blocks * Q_BLOCK

    def pad(t):
        return jnp.pad(t, [(0, 0), (0, padded - length)] + [(0, 0)] * (t.ndim - 2))

    q, k, v, qi, ki, wi = (pad(t) for t in (q, k, v, qi, ki, wi))
    key_pos = jnp.arange(padded)
    ki32 = ki.astype(jnp.float32)
    take = jax.vmap(lambda t, i: t[i])

    def to_blocks(t):
        return jnp.moveaxis(t.reshape((bsz, n_blocks, Q_BLOCK) + t.shape[2:]), 1, 0)

    def block(args):
        qb, qib, wib, qpos = args
        logits = jnp.einsum('bqhd,bsd->bqhs', qib.astype(jnp.float32), ki32) * (IDX_DIM ** -0.5)
        score = jnp.einsum('bqhs,bqh->bqs', jax.nn.relu(logits), wib.astype(jnp.float32))
        score = jnp.where((key_pos[None, :] <= qpos[:, None])[None], score, -jnp.inf)
        _, idx = lax.top_k(score, topk)
        valid = idx <= qpos[None, :, None]
        k_sel = take(k, idx)
        v_sel = take(v, idx)
        s = jnp.einsum('bqhd,bqkhd->bqhk', qb.astype(jnp.float32), k_sel.astype(jnp.float32)) * (HEAD_DIM ** -0.5)
        p = jax.nn.softmax(jnp.where(valid[:, :, None, :], s, -jnp.inf), axis=-1)
        return jnp.einsum('bqhk,bqkhd->bqhd', p, v_sel.astype(jnp.float32)).astype(qb.dtype)

    out = lax.map(block, (to_blocks(q), to_blocks(qi), to_blocks(wi), key_pos.reshape(n_blocks, Q_BLOCK)))
    return jnp.moveaxis(out, 0, 1).reshape(bsz, padded, ATT_HEADS, HEAD_DIM)[:, :length]


def conv_sparse_attn_mixer(h, pos, n_keys, w_in, conv_w, conv_b, ln_g, ln_b, w_out):
    bsz, length, _ = h.shape
    offsets = np.cumsum(AB_SPLIT_WIDTHS)[:-1].tolist()
    a_val, a_gate, q, k, v, qi, ki, wi = jnp.split(h @ w_in, offsets, axis=-1)
    a = a_val * jax.nn.sigmoid(a_gate)
    a = jax.nn.silu(layer_norm(causal_depthwise_conv(a, conv_w, conv_b), ln_g, ln_b))
    q = rope(q.reshape(bsz, length, ATT_HEADS, HEAD_DIM), pos)
    k = rope(k.reshape(bsz, length, ATT_HEADS, HEAD_DIM), pos)
    v = v.reshape(bsz, length, ATT_HEADS, HEAD_DIM)
    qi = rope(qi.reshape(bsz, length, IDX_HEADS, IDX_DIM), pos)
    ki = rope(ki[:, :, None, :], pos)[:, :, 0]
    wi = wi * (IDX_HEADS ** -0.5)
    topk = min(TOPK_MAX, n_keys // 4)
    o = sparse_index_attention(q, k, v, qi, ki, wi, topk).reshape(bsz, length, ATT_WIDTH)
    return jnp.concatenate([a, o], axis=-1) @ w_out


def rg_lru(xr, w_a, b_a, w_x, b_x, lam):
    bsz, length, width = xr.shape
    xb = xr.reshape(bsz, length, RNN_BLOCKS, RNN_BLOCK_W)
    r = jax.nn.sigmoid(jnp.einsum('blnc,ncd->blnd', xb, w_a).reshape(bsz, length, width) + b_a).astype(jnp.float32)
    i = jax.nn.sigmoid(jnp.einsum('blnc,ncd->blnd', xb, w_x).reshape(bsz, length, width) + b_x).astype(jnp.float32)
    log_a = -RG_C * r * jax.nn.softplus(-lam.astype(jnp.float32))
    a = jnp.exp(log_a)
    u = jnp.sqrt(-jnp.expm1(2.0 * log_a)) * (i * xr.astype(jnp.float32))

    def combine(left, right):
        a1, b1 = left
        a2, b2 = right
        return a1 * a2, a2 * b1 + b2

    _, hs = lax.associative_scan(combine, (a, u), axis=1)
    return hs.astype(xr.dtype)


def rglru_mixer(h, w_in, conv_w, conv_b, w_a, b_a, w_x, b_x, lam, w_out):
    gate, xr = jnp.split(h @ w_in, 2, axis=-1)
    xr = causal_depthwise_conv(xr, conv_w, conv_b)
    y = rg_lru(xr, w_a, b_a, w_x, b_x, lam)
    return (jax.nn.gelu(gate) * y) @ w_out


def grouped_expert_ffn(xf, eid, w_gate, w_up, w_down):
    n_tok, d = xf.shape
    n_pairs = eid.shape[0]
    n_exp = w_gate.shape[0]
    order = jnp.argsort(eid)
    e_sorted = eid[order]
    counts = jnp.bincount(eid, length=n_exp)
    padded = (counts + EXPERT_BLOCK - 1) // EXPERT_BLOCK * EXPERT_BLOCK
    start = jnp.cumsum(counts) - counts
    pend = jnp.cumsum(padded)
    pstart = pend - padded
    dest = (pstart[e_sorted] + jnp.arange(n_pairs) - start[e_sorted]).astype(jnp.int32)
    n_blocks = -(-(n_pairs + n_exp * (EXPERT_BLOCK - 1)) // EXPERT_BLOCK)
    cap = n_blocks * EXPERT_BLOCK
    tok = jnp.full((cap,), n_tok, jnp.int32).at[dest].set((order // TOP_K_INNER).astype(jnp.int32))
    block_expert = jnp.minimum(jnp.searchsorted(pend, jnp.arange(n_blocks) * EXPERT_BLOCK, side='right'), n_exp - 1)
    x_pad = jnp.concatenate([xf, jnp.zeros((1, d), xf.dtype)], axis=0)
    xb = x_pad[tok].reshape(n_blocks, EXPERT_BLOCK, d)

    def run(args):
        xblk, e = args
        return (jax.nn.silu(xblk @ w_gate[e]) * (xblk @ w_up[e])) @ w_down[e]

    yb = lax.map(run, (xb, block_expert)).reshape(cap, d)
    slot_of_pair = jnp.zeros((n_pairs,), jnp.int32).at[order].set(dest)
    return yb[slot_of_pair]


def hierarchical_moe(h, wg, bg, we, be, w_gate, w_up, w_down):
    bsz, length, d = h.shape
    n_tok = bsz * length
    xf = h.reshape(n_tok, d)
    g_prob = jax.nn.softmax((xf @ wg + bg).astype(jnp.float32), axis=-1)
    g_p, g_idx = lax.top_k(g_prob, 1)
    e_logits = (xf @ we + be).astype(jnp.float32).reshape(n_tok, N_GROUPS, EXPERTS_PER_GROUP)
    e_logits = jnp.take_along_axis(e_logits, g_idx[:, :, None], axis=1)[:, 0]
    e_top, e_idx = lax.top_k(e_logits, TOP_K_INNER)
    gate = g_p * jax.nn.softmax(e_top, axis=-1)
    expert = (g_idx * EXPERTS_PER_GROUP + e_idx).reshape(-1)
    y_pairs = grouped_expert_ffn(xf, expert, w_gate, w_up, w_down).reshape(n_tok, TOP_K_INNER, d)
    y = jnp.einsum('nkd,nk->nd', y_pairs, gate.astype(y_pairs.dtype))
    return y.reshape(bsz, length, d)


def setup_inputs(seed: int = 0) -> dict:
    key = jax.random.key(seed)
    it = iter(jax.random.split(key, 32))

    def nrm(shape, scale):
        return jax.random.normal(next(it), shape, jnp.float32) * scale

    in_ab = int(sum(AB_SPLIT_WIDTHS))
    a0 = jax.random.uniform(next(it), (N_ODD, RNN_WIDTH), jnp.float32, minval=0.9, maxval=0.999)
    s = a0 ** (1.0 / RG_C)
    lam = jnp.log(s) - jnp.log1p(-s)
    return {
        'x': nrm((BATCH, SEQ, D_MODEL), 1.0),
        'meta_tokens': nrm((N_META, D_MODEL), 1.0),
        'ab_w_in': nrm((N_EVEN, D_MODEL, in_ab), D_MODEL ** -0.5),
        'ab_conv_w': nrm((N_EVEN, CONV_K, CONV_CH), CONV_K ** -0.5),
        'ab_conv_b': nrm((N_EVEN, CONV_CH), 0.02),
        'ab_ln_g': 1.0 + nrm((N_EVEN, CONV_CH), 0.02),
        'ab_ln_b': nrm((N_EVEN, CONV_CH), 0.02),
        'ab_w_out': nrm((N_EVEN, CONV_CH + ATT_WIDTH, D_MODEL), (CONV_CH + ATT_WIDTH) ** -0.5 * DN_BETA),
        'c_w_in': nrm((N_ODD, D_MODEL, 2 * RNN_WIDTH), D_MODEL ** -0.5),
        'c_conv_w': nrm((N_ODD, RNN_CONV_K, RNN_WIDTH), RNN_CONV_K ** -0.5),
        'c_conv_b': nrm((N_ODD, RNN_WIDTH), 0.02),
        'c_gate_a_w': nrm((N_ODD, RNN_BLOCKS, RNN_BLOCK_W, RNN_BLOCK_W), RNN_BLOCK_W ** -0.5),
        'c_gate_a_b': nrm((N_ODD, RNN_WIDTH), 0.02),
        'c_gate_x_w': nrm((N_ODD, RNN_BLOCKS, RNN_BLOCK_W, RNN_BLOCK_W), RNN_BLOCK_W ** -0.5),
        'c_gate_x_b': nrm((N_ODD, RNN_WIDTH), 0.02),
        'c_lambda': lam,
        'c_w_out': nrm((N_ODD, RNN_WIDTH, D_MODEL), RNN_WIDTH ** -0.5 * DN_BETA),
        'moe_router_group_w': nrm((DEPTH, D_MODEL, N_GROUPS), D_MODEL ** -0.5),
        'moe_router_group_b': nrm((DEPTH, N_GROUPS), 0.01),
        'moe_router_expert_w': nrm((DEPTH, D_MODEL, N_EXPERTS), D_MODEL ** -0.5),
        'moe_router_expert_b': nrm((DEPTH, N_EXPERTS), 0.01),
        'moe_w_gate': nrm((DEPTH, N_EXPERTS, D_MODEL, D_EXPERT), D_MODEL ** -0.5),
        'moe_w_up': nrm((DEPTH, N_EXPERTS, D_MODEL, D_EXPERT), D_MODEL ** -0.5),
        'moe_w_down': nrm((DEPTH, N_EXPERTS, D_EXPERT, D_MODEL), D_EXPERT ** -0.5 * DN_BETA),
        'ln_mix_g': 1.0 + nrm((DEPTH, D_MODEL), 0.02),
        'ln_mix_b': nrm((DEPTH, D_MODEL), 0.02),
        'ln_ffn_g': 1.0 + nrm((DEPTH, D_MODEL), 0.02),
        'ln_ffn_b': nrm((DEPTH, D_MODEL), 0.02),
    }


def reference(x, meta_tokens, ab_w_in, ab_conv_w, ab_conv_b, ab_ln_g, ab_ln_b, ab_w_out,
              c_w_in, c_conv_w, c_conv_b, c_gate_a_w, c_gate_a_b, c_gate_x_w, c_gate_x_b, c_lambda, c_w_out,
              moe_router_group_w, moe_router_group_b, moe_router_expert_w, moe_router_expert_b,
              moe_w_gate, moe_w_up, moe_w_down, ln_mix_g, ln_mix_b, ln_ffn_g, ln_ffn_b):
    bsz, seq, d = x.shape
    meta = jnp.broadcast_to(meta_tokens[None].astype(x.dtype), (bsz, N_META, d))
    h = jnp.concatenate([meta, x], axis=1)
    pos = jnp.arange(N_META + seq)
    for layer in range(DEPTH):
        j = layer // 2
        if layer % 2 == 0:
            m = conv_sparse_attn_mixer(h, pos, seq, ab_w_in[j], ab_conv_w[j], ab_conv_b[j],
                                       ab_ln_g[j], ab_ln_b[j], ab_w_out[j])
        else:
            m = rglru_mixer(h, c_w_in[j], c_conv_w[j], c_conv_b[j], c_gate_a_w[j], c_gate_a_b[j],
                            c_gate_x_w[j], c_gate_x_b[j], c_lambda[j], c_w_out[j])
        h = layer_norm(DN_ALPHA * h + m, ln_mix_g[layer], ln_mix_b[layer])
        f = hierarchical_moe(h, moe_router_group_w[layer], moe_router_group_b[layer],
                             moe_router_expert_w[layer], moe_router_expert_b[layer],
                             moe_w_gate[layer], moe_w_up[layer], moe_w_down[layer])
        h = layer_norm(DN_ALPHA * h + f, ln_ffn_g[layer], ln_ffn_b[layer])
    return h[:, N_META:]
```

```python
import functools

import jax
import jax.numpy as jnp
import numpy as np
from jax import lax
from jax.experimental import pallas as pl
from jax.experimental.pallas import tpu as pltpu

F32 = jnp.float32
BF16 = jnp.bfloat16
I32 = jnp.int32

N_META = 16
CONV_CH = 512
CONV_K = 31
ATT_HEADS = 8
HEAD_DIM = 64
ATT_WIDTH = ATT_HEADS * HEAD_DIM
IDX_HEADS = 8
IDX_DIM = 64
TOPK_MAX = 256
ROPE_THETA = 10000.0
RNN_WIDTH = 1280
RNN_BLOCKS = 10
RNN_BLOCK_W = RNN_WIDTH // RNN_BLOCKS
RNN_CONV_K = 4
RG_C = 8.0
N_GROUPS = 4
EXPERTS_PER_GROUP = 8
N_EXPERTS = N_GROUPS * EXPERTS_PER_GROUP
D_EXPERT = 512
LN_EPS = 1e-5
DEPTH = 2
DN_ALPHA = (2 * DEPTH) ** 0.25

LANE = 128
VMEM_BYTES = 64 << 20

Q_BLOCK = 128
KEY_CHUNK = 512
CONV_HALO = 32
RNN_HALO = 8
EXPERT_ROWS = 256
INT_MIN = -(2 ** 31)
NEG_MASK = -2e30
M_INIT = -1e30

_NT = (((1,), (1,)), ((), ()))


def _tile_rows(n):
    for t in (512, 384, 256, 128):
        if n % t == 0:
            return t
    raise ValueError(n)


def _mm(a, b):
    return jnp.dot(a, b, preferred_element_type=F32)


def _layer_norm(x, g, b):
    mu = jnp.mean(x, axis=-1, keepdims=True)
    xc = x - mu
    var = jnp.mean(xc * xc, axis=-1, keepdims=True)
    return xc * lax.rsqrt(var + LN_EPS) * g + b


def _rope_group(t, cos, sin_signed, first_half):
    partner = jnp.where(first_half, pltpu.roll(t, LANE - 32, 1), pltpu.roll(t, 32, 1))
    return t * cos + partner * sin_signed


def _params(sem, vmem_mb):
    return pltpu.CompilerParams(dimension_semantics=sem, vmem_limit_bytes=vmem_mb << 20)


def _l0_in_kernel(h_ref, wglu_ref, wqkv_ref, wsm_ref, cos_ref, sin_ref, cw_ref, cb_ref, lg_ref, lb_ref,
                  a2_ref, q_ref, k_ref, v_ref, qi_ref, ki_ref, wi_ref, abuf):
    i = pl.program_id(0)
    tm = h_ref.shape[0]
    hb = h_ref[...].astype(BF16)

    glu = _mm(hb, wglu_ref[...])
    a = glu[:, :CONV_CH] * jax.nn.sigmoid(glu[:, CONV_CH:])

    @pl.when(i == 0)
    def _():
        abuf[0:CONV_HALO, :] = jnp.zeros((CONV_HALO, CONV_CH), F32)

    abuf[pl.ds(CONV_HALO, tm), :] = a
    acc = jnp.broadcast_to(cb_ref[...], (tm, CONV_CH))
    for j in range(CONV_K):
        acc = acc + cw_ref[j:j + 1, :] * abuf[pl.ds(CONV_HALO - (CONV_K - 1) + j, tm), :]
    abuf[0:CONV_HALO, :] = abuf[pl.ds(tm, CONV_HALO), :]
    y = _layer_norm(acc, lg_ref[...], lb_ref[...])
    a2_ref[...] = (y * jax.nn.sigmoid(y)).astype(BF16)

    cos = cos_ref[...]
    sin = sin_ref[...]
    lane = lax.broadcasted_iota(I32, (tm, LANE), 1)
    first_half = (lane % HEAD_DIM) < (HEAD_DIM // 2)
    qkv = _mm(hb, wqkv_ref[...])
    for g in range(ATT_WIDTH // LANE):
        sl = slice(g * LANE, (g + 1) * LANE)
        qg = qkv[:, g * LANE:(g + 1) * LANE]
        kg = qkv[:, ATT_WIDTH + g * LANE:ATT_WIDTH + (g + 1) * LANE]
        ig = qkv[:, 3 * ATT_WIDTH + g * LANE:3 * ATT_WIDTH + (g + 1) * LANE]
        q_ref[:, sl] = (_rope_group(qg, cos, sin, first_half) * (HEAD_DIM ** -0.5)).astype(BF16)
        k_ref[:, sl] = _rope_group(kg, cos, sin, first_half).astype(BF16)
        qi_ref[:, sl] = (_rope_group(ig, cos, sin, first_half) * (IDX_DIM ** -0.5)).astype(BF16)
    v_ref[...] = qkv[:, 2 * ATT_WIDTH:3 * ATT_WIDTH].astype(BF16)

    sm = _mm(hb, wsm_ref[...])
    ki = _rope_group(sm, cos, sin, first_half)
    ki_ref[...] = jnp.where(lane < IDX_DIM, ki, 0.0).astype(BF16)
    wi_ref[...] = sm * (IDX_HEADS ** -0.5)


def _l0_in(h, wglu, wqkv, wsm, cos, sin, cw, cb, lg, lb):
    lp, d = h.shape
    tm = _tile_rows(lp)
    row = lambda w: pl.BlockSpec((tm, w), lambda i: (i, 0))
    full = lambda a: pl.BlockSpec(a.shape, lambda i: (0,) * a.ndim)
    outs = [jax.ShapeDtypeStruct((lp, CONV_CH), BF16)] + [jax.ShapeDtypeStruct((lp, ATT_WIDTH), BF16)] * 4 + [
        jax.ShapeDtypeStruct((lp, LANE), BF16), jax.ShapeDtypeStruct((lp, LANE), F32)]
    return pl.pallas_call(
        _l0_in_kernel,
        grid=(lp // tm,),
        in_specs=[row(d), full(wglu), full(wqkv), full(wsm), row(LANE), row(LANE), full(cw), full(cb), full(lg),
                  full(lb)],
        out_specs=[row(CONV_CH)] + [row(ATT_WIDTH)] * 4 + [row(LANE), row(LANE)],
        out_shape=outs,
        scratch_shapes=[pltpu.VMEM((CONV_HALO + tm, CONV_CH), F32)],
        compiler_params=_params(("arbitrary",), 48),
        name="l0_in",
    )(h, wglu, wqkv, wsm, cos, sin, cw, cb, lg, lb)


def _dsa_kernel(q_ref, qi_ref, wi_ref, k_ref, v_ref, ki_ref, u_ref, o_ref,
                s_ref, qi8_ref, qh_ref, wb_ref, m_ref, l_ref, acc_ref, *, topk):
    i = pl.program_id(0)
    bq = q_ref.shape[0]
    ck = KEY_CHUNK
    nch = ((i + 1) * bq + ck - 1) // ck
    lane = lax.broadcasted_iota(I32, (bq, LANE), 1)
    low = lane < HEAD_DIM

    for g in range(ATT_WIDTH // LANE):
        pair = qi_ref[:, g * LANE:(g + 1) * LANE].astype(F32)
        qi8_ref[pl.ds((2 * g) * bq, bq), :] = jnp.where(low, pair, 0.0).astype(BF16)
        qi8_ref[pl.ds((2 * g + 1) * bq, bq), :] = jnp.where(low, pltpu.roll(pair, HEAD_DIM, 1), 0.0).astype(BF16)
        qp = q_ref[:, g * LANE:(g + 1) * LANE].astype(F32)
        qh_ref[2 * g] = jnp.where(low, qp, 0.0).astype(BF16)
        qh_ref[2 * g + 1] = jnp.where(low, 0.0, qp).astype(BF16)
    wi = wi_ref[...]
    for h in range(IDX_HEADS):
        wb_ref[h] = jnp.broadcast_to(wi[:, IDX_DIM + h:IDX_DIM + h + 1], (bq, ck))

    def index_scores(c, masked):
        kic = ki_ref[pl.ds(pl.multiple_of(c * ck, ck), ck), :]
        lg = lax.dot_general(qi8_ref[...], kic, _NT, preferred_element_type=F32)
        sc = jnp.zeros((bq, ck), F32)
        for h in range(IDX_HEADS):
            sc = sc + jnp.maximum(lg[h * bq:(h + 1) * bq, :], 0.0) * wb_ref[h]
        bits = lax.bitcast_convert_type(sc, I32)
        key = bits ^ ((bits >> 31) & jnp.int32(0x7FFFFFFF))
        if masked:
            kpos = c * ck + lax.broadcasted_iota(I32, (bq, ck), 1)
            qpos = i * bq + lax.broadcasted_iota(I32, (bq, ck), 0)
            key = jnp.where(kpos <= qpos, key, jnp.int32(INT_MIN))
        s_ref[c] = key

    def full_chunk(c, carry):
        index_scores(c, False)
        return carry

    lax.fori_loop(0, nch - 1, full_chunk, 0)
    index_scores(nch - 1, True)

    def count_ge(cand):
        def body(c, cnt):
            blk = s_ref[c]
            for g in range(ck // LANE):
                cnt = cnt + jnp.where(blk[:, g * LANE:(g + 1) * LANE] >= cand, 1.0, 0.0)
            return cnt
        cnt = lax.fori_loop(0, nch, body, jnp.zeros((bq, LANE), F32))
        return jnp.sum(cnt, axis=1, keepdims=True)

    kf = float(topk)

    def bisect(it, lo):
        cand = lo + jnp.left_shift(jnp.int32(1), 31 - it)
        return jnp.where(count_ge(cand) >= kf, cand, lo)

    thr = lax.fori_loop(0, 32, bisect, jnp.full((bq, LANE), INT_MIN, I32))
    n_above = count_ge(thr + 1)
    thr1 = thr[:, 0:1]
    need = jnp.where(thr1 == INT_MIN, 0.0, kf - n_above)
    thr_b = jnp.broadcast_to(thr1, (bq, ck))
    need_b = jnp.broadcast_to(need, (bq, ck))

    m_ref[...] = jnp.full(m_ref.shape, M_INIT, F32)
    l_ref[...] = jnp.zeros(l_ref.shape, F32)
    acc_ref[...] = jnp.zeros(acc_ref.shape, F32)

    def attend(c, ties_seen):
        keys = s_ref[c]
        eq = keys == thr_b
        prefix = _mm(jnp.where(eq, 1.0, 0.0).astype(BF16), u_ref[...]) + ties_seen
        bias = jnp.where(keys > thr_b, 0.0, jnp.where(eq, jnp.where(prefix <= need_b, 0.0, NEG_MASK), NEG_MASK))
        rows = pl.ds(pl.multiple_of(c * ck, ck), ck)
        for h in range(ATT_HEADS):
            cols = slice((h // 2) * LANE, (h // 2 + 1) * LANE)
            s = lax.dot_general(qh_ref[h], k_ref[rows, cols], _NT, preferred_element_type=F32) + bias
            m_prev = m_ref[h]
            m_new = jnp.maximum(m_prev, jnp.max(s, axis=1, keepdims=True))
            alpha = jnp.exp(m_prev - m_new)
            p = jnp.exp(s - m_new)
            l_ref[h] = alpha * l_ref[h] + jnp.sum(p, axis=1, keepdims=True)
            acc_ref[h] = alpha * acc_ref[h] + _mm(p.astype(BF16), v_ref[rows, cols])
            m_ref[h] = m_new
        return prefix[:, ck - 1:ck]

    lax.fori_loop(0, nch, attend, jnp.zeros((bq, 1), F32))

    for g in range(ATT_WIDTH // LANE):
        even = acc_ref[2 * g] * (1.0 / l_ref[2 * g])
        odd = acc_ref[2 * g + 1] * (1.0 / l_ref[2 * g + 1])
        o_ref[:, g * LANE:(g + 1) * LANE] = jnp.where(low, even, odd).astype(BF16)


def _dsa(q, qi, wi, k, v, ki, topk):
    lp = q.shape[0]
    lk = k.shape[0]
    bq, ck = Q_BLOCK, KEY_CHUNK
    assert topk <= ck and lk % ck == 0 and lk >= lp
    u = (np.arange(ck)[:, None] <= np.arange(ck)[None, :]).astype(np.float32)
    u = jnp.asarray(u, BF16)
    row = lambda w: pl.BlockSpec((bq, w), lambda i: (i, 0))
    res = lambda a: pl.BlockSpec(a.shape, lambda i: (0, 0), pipeline_mode=pl.Buffered(1))
    return pl.pallas_call(
        functools.partial(_dsa_kernel, topk=topk),
        grid=(lp // bq,),
        in_specs=[row(ATT_WIDTH), row(ATT_WIDTH), row(LANE), res(k), res(v), res(ki), res(u)],
        out_specs=row(ATT_WIDTH),
        out_shape=jax.ShapeDtypeStruct((lp, ATT_WIDTH), BF16),
        scratch_shapes=[
            pltpu.VMEM((lk // ck, bq, ck), I32),
            pltpu.VMEM((IDX_HEADS * bq, LANE), BF16),
            pltpu.VMEM((ATT_HEADS, bq, LANE), BF16),
            pltpu.VMEM((IDX_HEADS, bq, ck), F32),
            pltpu.VMEM((ATT_HEADS, bq, 1), F32),
            pltpu.VMEM((ATT_HEADS, bq, 1), F32),
            pltpu.VMEM((ATT_HEADS, bq, LANE), F32),
        ],
        compiler_params=_params(("arbitrary",), 60),
        name="dsa",
    )(q, qi, wi, k, v, ki, u)


def _l0_out_kernel(a2_ref, o_ref, wa_ref, wo_ref, h_ref, g_ref, b_ref, out_ref):
    m = _mm(a2_ref[...], wa_ref[...]) + _mm(o_ref[...], wo_ref[...])
    out_ref[...] = _layer_norm(DN_ALPHA * h_ref[...] + m, g_ref[...], b_ref[...])


def _l0_out(a2, o, wa, wo, h, g, b):
    lp, d = h.shape
    tm = _tile_rows(lp)
    row = lambda w: pl.BlockSpec((tm, w), lambda i: (i, 0))
    full = lambda a: pl.BlockSpec(a.shape, lambda i: (0,) * a.ndim)
    return pl.pallas_call(
        _l0_out_kernel,
        grid=(lp // tm,),
        in_specs=[row(CONV_CH), row(ATT_WIDTH), full(wa), full(wo), row(d), full(g), full(b)],
        out_specs=row(d),
        out_shape=jax.ShapeDtypeStruct((lp, d), F32),
        compiler_params=_params(("parallel",), 32),
        name="l0_out",
    )(a2, o, wa, wo, h, g, b)


def _l1_kernel(h_ref, win_ref, cw_ref, cb_ref, wa_ref, ba_ref, wx_ref, bx_ref, lam_ref, wout_ref, g_ref, b_ref,
               out_ref, xbuf, hstate):
    i = pl.program_id(0)
    tm = h_ref.shape[0]
    h = h_ref[...]
    z = _mm(h.astype(BF16), win_ref[...])
    gate = z[:, :RNN_WIDTH]

    @pl.when(i == 0)
    def _():
        xbuf[0:RNN_HALO, :] = jnp.zeros((RNN_HALO, RNN_WIDTH), F32)
        hstate[...] = jnp.zeros(hstate.shape, F32)

    xbuf[pl.ds(RNN_HALO, tm), :] = z[:, RNN_WIDTH:]
    xc = jnp.broadcast_to(cb_ref[...], (tm, RNN_WIDTH))
    for j in range(RNN_CONV_K):
        xc = xc + cw_ref[j:j + 1, :] * xbuf[pl.ds(RNN_HALO - (RNN_CONV_K - 1) + j, tm), :]
    xbuf[0:RNN_HALO, :] = xbuf[pl.ds(tm, RNN_HALO), :]

    xcb = xc.astype(BF16)
    ra, ri = [], []
    for n in range(RNN_BLOCKS):
        blk = xcb[:, n * RNN_BLOCK_W:(n + 1) * RNN_BLOCK_W]
        ra.append(_mm(blk, wa_ref[n]))
        ri.append(_mm(blk, wx_ref[n]))
    r = jax.nn.sigmoid(jnp.concatenate(ra, axis=1) + ba_ref[...])
    ig = jax.nn.sigmoid(jnp.concatenate(ri, axis=1) + bx_ref[...])
    nl = -lam_ref[...]
    softplus = jnp.maximum(nl, 0.0) + jnp.log(1.0 + jnp.exp(-jnp.abs(nl)))
    log_a = -RG_C * r * softplus
    a = jnp.exp(log_a)
    u = jnp.sqrt(1.0 - jnp.exp(2.0 * log_a)) * (ig * xc)

    rows = lax.broadcasted_iota(I32, (tm, RNN_WIDTH), 0)
    d = 1
    while d < tm:
        keep = rows >= d
        a_sh = jnp.where(keep, pltpu.roll(a, d, 0), 1.0)
        u_sh = jnp.where(keep, pltpu.roll(u, d, 0), 0.0)
        u = a * u_sh + u
        a = a * a_sh
        d *= 2
    hs = u + a * hstate[0:1, :]
    hstate[0:1, :] = hs[tm - 1:tm, :]

    y = (jax.nn.gelu(gate) * hs).astype(BF16)
    m = _mm(y, wout_ref[...])
    out_ref[...] = _layer_norm(DN_ALPHA * h + m, g_ref[...], b_ref[...])


def _l1_mixer(h, win, cw, cb, wa, ba, wx, bx, lam, wout, g, b):
    lp, d = h.shape
    tm = _tile_rows(lp)
    row = lambda w: pl.BlockSpec((tm, w), lambda i: (i, 0))
    full = lambda a: pl.BlockSpec(a.shape, lambda i: (0,) * a.ndim)
    args = (h, win, cw, cb, wa, ba, wx, bx, lam, wout, g, b)
    return pl.pallas_call(
        _l1_kernel,
        grid=(lp // tm,),
        in_specs=[row(d)] + [full(a) for a in args[1:]],
        out_specs=row(d),
        out_shape=jax.ShapeDtypeStruct((lp, d), F32),
        scratch_shapes=[pltpu.VMEM((RNN_HALO + tm, RNN_WIDTH), F32), pltpu.VMEM((8, RNN_WIDTH), F32)],
        compiler_params=_params(("arbitrary",), 56),
        name="l1_mixer",
    )(*args)


def _router_kernel(h_ref, wr_ref, br_ref, ltri_ref, eid_ref, gate_ref, rank_ref, cnt_ref, carry_ref, *, n_real):
    i = pl.program_id(0)
    tm = h_ref.shape[0]

    @pl.when(i == 0)
    def _():
        carry_ref[...] = jnp.zeros(carry_ref.shape, F32)

    logits = jnp.dot(h_ref[...], wr_ref[...], precision=lax.Precision.HIGHEST,
                     preferred_element_type=F32) + br_ref[...]
    lane = lax.broadcasted_iota(I32, (tm, LANE), 1).astype(F32)
    ninf = -jnp.inf
    big = float(LANE)

    gl = jnp.where(lane < N_GROUPS, logits[:, :LANE], ninf)
    gmax = jnp.max(gl, axis=1, keepdims=True)
    g_p = 1.0 / jnp.sum(jnp.exp(gl - gmax), axis=1, keepdims=True)
    g_idx = jnp.min(jnp.where(gl == gmax, lane, big), axis=1, keepdims=True)

    first = g_idx * EXPERTS_PER_GROUP
    el = logits[:, LANE:]
    m1 = jnp.where(lane >= first, jnp.where(lane < first + EXPERTS_PER_GROUP, el, ninf), ninf)
    t1 = jnp.max(m1, axis=1, keepdims=True)
    i1 = jnp.min(jnp.where(m1 == t1, lane, big), axis=1, keepdims=True)
    m2 = jnp.where(lane == i1, ninf, m1)
    t2 = jnp.max(m2, axis=1, keepdims=True)
    i2 = jnp.min(jnp.where(m2 == t2, lane, big), axis=1, keepdims=True)
    e2 = jnp.exp(t2 - t1)
    den = 1.0 / (1.0 + e2)

    tok = i * tm + lax.broadcasted_iota(I32, (tm, LANE), 0)
    valid = tok < n_real
    oh0 = jnp.where(valid, jnp.where(lane == i1, 1.0, 0.0), 0.0)
    oh1 = jnp.where(valid, jnp.where(lane == i2, 1.0, 0.0), 0.0)
    ohs = oh0 + oh1
    before = _mm(ltri_ref[...], ohs.astype(BF16)) + carry_ref[...]
    carry_ref[...] = carry_ref[...] + jnp.sum(ohs, axis=0, keepdims=True)
    cnt_ref[...] = carry_ref[...]

    eid_ref[:, 0:1] = i1.astype(I32)
    eid_ref[:, 1:2] = i2.astype(I32)
    gate_ref[:, 0:1] = g_p * den
    gate_ref[:, 1:2] = g_p * e2 * den
    rank_ref[:, 0:1] = jnp.sum(oh0 * before, axis=1, keepdims=True).astype(I32)
    rank_ref[:, 1:2] = jnp.sum(oh1 * before, axis=1, keepdims=True).astype(I32)


def _router(h, wr, br, n_real):
    lp, d = h.shape
    tm = _tile_rows(lp)
    ltri = jnp.asarray((np.arange(tm)[:, None] > np.arange(tm)[None, :]).astype(np.float32), BF16)
    row = lambda w: pl.BlockSpec((tm, w), lambda i: (i, 0))
    full = lambda a: pl.BlockSpec(a.shape, lambda i: (0,) * a.ndim)
    return pl.pallas_call(
        functools.partial(_router_kernel, n_real=n_real),
        grid=(lp // tm,),
        in_specs=[row(d), full(wr), full(br), full(ltri)],
        out_specs=[row(2), row(2), row(2), pl.BlockSpec((1, LANE), lambda i: (0, 0))],
        out_shape=[jax.ShapeDtypeStruct((lp, 2), I32), jax.ShapeDtypeStruct((lp, 2), F32),
                   jax.ShapeDtypeStruct((lp, 2), I32), jax.ShapeDtypeStruct((1, LANE), F32)],
        scratch_shapes=[pltpu.VMEM((1, LANE), F32)],
        compiler_params=_params(("arbitrary",), 32),
        name="moe_router",
    )(h, wr, br, ltri)


def _row_copy(src_ref, src_row, dst_ref, dst_row, sem):
    return pltpu.make_async_copy(src_ref.at[pl.ds(src_row, 1), :], dst_ref.at[pl.ds(dst_row, 1), :], sem)


def _dispatch_kernel(dest_ref, h_ref, xb_in_ref, xb_ref, sem, *, n_real):
    del xb_in_ref
    i = pl.program_id(0)
    tm = h_ref.shape[0]
    n_rows = jnp.clip(n_real - i * tm, 0, tm)

    def issue(r, carry):
        for s in range(2):
            _row_copy(h_ref, r, xb_ref, dest_ref[2 * (i * tm + r) + s], sem).start()
        return carry

    def drain(r, carry):
        for s in range(2):
            _row_copy(h_ref, 0, xb_ref, 0, sem).wait()
        return carry

    lax.fori_loop(0, n_rows, issue, 0)
    lax.fori_loop(0, n_rows, drain, 0)


def _dispatch(dest_flat, h, cap, n_real):
    lp, d = h.shape
    tm = _tile_rows(lp)
    xb0 = jnp.zeros((cap, d), F32)
    grid_spec = pltpu.PrefetchScalarGridSpec(
        num_scalar_prefetch=1,
        grid=(lp // tm,),
        in_specs=[pl.BlockSpec((tm, d), lambda i, dest: (i, 0)), pl.BlockSpec(memory_space=pl.ANY)],
        out_specs=pl.BlockSpec(memory_space=pl.ANY),
        scratch_shapes=[pltpu.SemaphoreType.DMA(())],
    )
    return pl.pallas_call(
        functools.partial(_dispatch_kernel, n_real=n_real),
        grid_spec=grid_spec,
        out_shape=jax.ShapeDtypeStruct((cap, d), F32),
        input_output_aliases={2: 0},
        compiler_params=pltpu.CompilerParams(dimension_semantics=("arbitrary",), has_side_effects=True),
        name="moe_dispatch",
    )(dest_flat, h, xb0)


def _experts_kernel(bexp_ref, nused_ref, xb_ref, wg_ref, wu_ref, wd_ref, yb_ref, wg_s, wu_s, wd_s):
    b = pl.program_id(0)
    prev = bexp_ref[jnp.maximum(b - 1, 0)]

    @pl.when((b == 0) | (bexp_ref[b] != prev))
    def _():
        wg_s[...] = wg_ref[...].astype(BF16)
        wu_s[...] = wu_ref[...].astype(BF16)
        wd_s[...] = wd_ref[...].astype(BF16)

    @pl.when(b < nused_ref[0])
    def _():
        x = xb_ref[...].astype(BF16)
        gt = _mm(x, wg_s[...])
        up = _mm(x, wu_s[...])
        mid = (gt * jax.nn.sigmoid(gt) * up).astype(BF16)
        yb_ref[...] = _mm(mid, wd_s[...])

    @pl.when(b >= nused_ref[0])
    def _():
        yb_ref[...] = jnp.zeros(yb_ref.shape, F32)


def _experts(bexp, nused, xb, wg, wu, wd):
    cap, d = xb.shape
    bm = EXPERT_ROWS
    wspec = lambda a: pl.BlockSpec((None,) + a.shape[1:], lambda b, bexp, nused: (bexp[b], 0, 0))
    grid_spec = pltpu.PrefetchScalarGridSpec(
        num_scalar_prefetch=2,
        grid=(cap // bm,),
        in_specs=[pl.BlockSpec((bm, d), lambda b, bexp, nused: (b, 0)), wspec(wg), wspec(wu), wspec(wd)],
        out_specs=pl.BlockSpec((bm, d), lambda b, bexp, nused: (b, 0)),
        scratch_shapes=[pltpu.VMEM(wg.shape[1:], BF16), pltpu.VMEM(wu.shape[1:], BF16),
                        pltpu.VMEM(wd.shape[1:], BF16)],
    )
    return pl.pallas_call(
        _experts_kernel,
        grid_spec=grid_spec,
        out_shape=jax.ShapeDtypeStruct((cap, d), F32),
        compiler_params=_params(("arbitrary",), 48),
        name="moe_experts",
    )(bexp, nused, xb, wg, wu, wd)


def _combine_kernel(dest_ref, yb_ref, gate_ref, h_ref, g_ref, b_ref, out_ref, ybuf, sem):
    i = pl.program_id(0)
    tm = h_ref.shape[0]

    def issue(r, carry):
        for s in range(2):
            _row_copy(yb_ref, dest_ref[2 * (i * tm + r) + s], ybuf.at[s], r, sem).start()
        return carry

    def drain(r, carry):
        for s in range(2):
            _row_copy(yb_ref, 0, ybuf.at[s], 0, sem).wait()
        return carry

    lax.fori_loop(0, tm, issue, 0)
    lax.fori_loop(0, tm, drain, 0)
    gate = gate_ref[...]
    y = gate[:, 0:1] * ybuf[0] + gate[:, 1:2] * ybuf[1]
    out_ref[...] = _layer_norm(DN_ALPHA * h_ref[...] + y, g_ref[...], b_ref[...])


def _combine(dest_flat, yb, gate, h, g, b):
    lp, d = h.shape
    tm = _tile_rows(lp)
    full = lambda a: pl.BlockSpec(a.shape, lambda i, dest: (0,) * a.ndim)
    grid_spec = pltpu.PrefetchScalarGridSpec(
        num_scalar_prefetch=1,
        grid=(lp // tm,),
        in_specs=[pl.BlockSpec(memory_space=pl.ANY), pl.BlockSpec((tm, 2), lambda i, dest: (i, 0)),
                  pl.BlockSpec((tm, d), lambda i, dest: (i, 0)), full(g), full(b)],
        out_specs=pl.BlockSpec((tm, d), lambda i, dest: (i, 0)),
        scratch_shapes=[pltpu.VMEM((2, tm, d), F32), pltpu.SemaphoreType.DMA(())],
    )
    return pl.pallas_call(
        _combine_kernel,
        grid_spec=grid_spec,
        out_shape=jax.ShapeDtypeStruct((lp, d), F32),
        compiler_params=_params(("arbitrary",), 32),
        name="moe_combine",
    )(dest_flat, yb, gate, h, g, b)


def _moe(h, n_real, wg, bg, we, be, w_gate, w_up, w_down, ln_g, ln_b):
    lp, d = h.shape
    bm = EXPERT_ROWS
    wr = jnp.zeros((d, 2 * LANE), F32).at[:, :N_GROUPS].set(wg).at[:, LANE:LANE + N_EXPERTS].set(we)
    br = jnp.zeros((1, 2 * LANE), F32).at[0, :N_GROUPS].set(bg).at[0, LANE:LANE + N_EXPERTS].set(be)
    eid, gate, rank, cnt = _router(h, wr, br, n_real)

    counts = cnt[0, :N_EXPERTS].astype(I32)
    padded = (counts + bm - 1) // bm * bm
    pend = jnp.cumsum(padded)
    pstart = pend - padded
    valid = (jnp.arange(lp) < n_real)[:, None]
    dest = jnp.where(valid, pstart[eid] + rank, 0).astype(I32)
    dest_flat = dest.reshape(-1)
    n_blocks = -(-(2 * n_real + N_EXPERTS * (bm - 1)) // bm)
    bexp = jnp.minimum(jnp.searchsorted(pend, jnp.arange(n_blocks, dtype=I32) * bm, side='right'),
                       N_EXPERTS - 1).astype(I32)
    nused = (pend[-1:] // bm).astype(I32)

    xb = _dispatch(dest_flat, h, n_blocks * bm, n_real)
    yb = _experts(bexp, nused, xb, w_gate, w_up, w_down)
    return _combine(dest_flat, yb, gate, h, ln_g, ln_b)


def kernel(x, meta_tokens, ab_w_in, ab_conv_w, ab_conv_b, ab_ln_g, ab_ln_b, ab_w_out, c_w_in, c_conv_w, c_conv_b, c_gate_a_w, c_gate_a_b, c_gate_x_w, c_gate_x_b, c_lambda, c_w_out, moe_router_group_w, moe_router_group_b, moe_router_expert_w, moe_router_expert_b, moe_w_gate, moe_w_up, moe_w_down, ln_mix_g, ln_mix_b, ln_ffn_g, ln_ffn_b):
    bsz, seq, d = x.shape
    assert bsz == 1, "kernel is written for batch 1"
    n_real = N_META + seq
    lp = -(-n_real // Q_BLOCK) * Q_BLOCK
    lk = -(-lp // KEY_CHUNK) * KEY_CHUNK
    topk = min(TOPK_MAX, seq // 4)
    row2 = lambda a: a.reshape(1, -1)

    h = jnp.concatenate([meta_tokens.astype(x.dtype), x[0], jnp.zeros((lp - n_real, d), x.dtype)], axis=0)

    half = HEAD_DIM // 2
    inv_freq = ROPE_THETA ** (-2.0 * jnp.arange(half, dtype=F32) / HEAD_DIM)
    ang = jnp.arange(lp, dtype=F32)[:, None] * inv_freq[None, :]
    cos = jnp.tile(jnp.cos(ang), (1, 4))
    sin = jnp.tile(jnp.concatenate([-jnp.sin(ang), jnp.sin(ang)], axis=1), (1, 2))

    for layer in range(DEPTH):
        j = layer // 2
        if layer % 2 == 0:
            w_in = ab_w_in[j]
            wglu = w_in[:, :2 * CONV_CH].astype(BF16)
            wqkv = w_in[:, 2 * CONV_CH:2 * CONV_CH + 4 * ATT_WIDTH].astype(BF16)
            wsm = jnp.zeros((d, LANE), F32).at[:, :IDX_DIM + IDX_HEADS].set(
                w_in[:, 2 * CONV_CH + 4 * ATT_WIDTH:]).astype(BF16)
            a2, q, k, v, qi, ki, wi = _l0_in(h, wglu, wqkv, wsm, cos, sin, ab_conv_w[j], row2(ab_conv_b[j]),
                                             row2(ab_ln_g[j]), row2(ab_ln_b[j]))
            pad = lambda t: jnp.pad(t, ((0, lk - lp), (0, 0)))
            o = _dsa(q, qi, wi, pad(k), pad(v), pad(ki), topk)
            w_out = ab_w_out[j].astype(BF16)
            h = _l0_out(a2, o, w_out[:CONV_CH], w_out[CONV_CH:], h, row2(ln_mix_g[layer]), row2(ln_mix_b[layer]))
        else:
            h = _l1_mixer(h, c_w_in[j].astype(BF16), c_conv_w[j], row2(c_conv_b[j]),
                          c_gate_a_w[j].astype(BF16), row2(c_gate_a_b[j]),
                          c_gate_x_w[j].astype(BF16), row2(c_gate_x_b[j]), row2(c_lambda[j]),
                          c_w_out[j].astype(BF16), row2(ln_mix_g[layer]), row2(ln_mix_b[layer]))
        h = _moe(h, n_real, moe_router_group_w[layer], moe_router_group_b[layer], moe_router_expert_w[layer],
                 moe_router_expert_b[layer], moe_w_gate[layer], moe_w_up[layer], moe_w_down[layer],
                 row2(ln_ffn_g[layer]), row2(ln_ffn_b[layer]))
    return h[N_META:n_real][None]
```

```python
import functools

import jax
import jax.numpy as jnp
import numpy as np
from jax import lax
from jax.experimental import pallas as pl
from jax.experimental.pallas import tpu as pltpu

F32 = jnp.float32
BF16 = jnp.bfloat16
I32 = jnp.int32

N_META = 16
CONV_CH = 512
CONV_K = 31
ATT_HEADS = 8
HEAD_DIM = 64
ATT_WIDTH = ATT_HEADS * HEAD_DIM
IDX_HEADS = 8
IDX_DIM = 64
TOPK_MAX = 256
ROPE_THETA = 10000.0
RNN_WIDTH = 1280
RNN_BLOCKS = 10
RNN_BLOCK_W = RNN_WIDTH // RNN_BLOCKS
RNN_CONV_K = 4
RG_C = 8.0
N_GROUPS = 4
EXPERTS_PER_GROUP = 8
N_EXPERTS = N_GROUPS * EXPERTS_PER_GROUP
D_EXPERT = 512
LN_EPS = 1e-5
DEPTH = 2
DN_ALPHA = (2 * DEPTH) ** 0.25

LANE = 128
VMEM_BYTES = 64 << 20

Q_BLOCK = 128
KEY_CHUNK = 512
CONV_HALO = 32
RNN_HALO = 8
EXPERT_ROWS = 256
INT_MIN = -(2 ** 31)
NEG_MASK = -2e30
M_INIT = -1e30
LOG2_E = 1.4426950408889634

_NT = (((1,), (1,)), ((), ()))


def _tile_rows(n):
    for t in (512, 384, 256, 128):
        if n % t == 0:
            return t
    raise ValueError(n)


def _mm(a, b):
    return jnp.dot(a, b, preferred_element_type=F32)


def _layer_norm(x, g, b):
    mu = jnp.mean(x, axis=-1, keepdims=True)
    xc = x - mu
    var = jnp.mean(xc * xc, axis=-1, keepdims=True)
    return xc * lax.rsqrt(var + LN_EPS) * g + b


def _rope_group(t, cos, sin_signed, first_half):
    partner = jnp.where(first_half, pltpu.roll(t, LANE - 32, 1), pltpu.roll(t, 32, 1))
    return t * cos + partner * sin_signed


def _params(sem, vmem_mb):
    return pltpu.CompilerParams(dimension_semantics=sem, vmem_limit_bytes=vmem_mb << 20)


def _l0_in_kernel(h_ref, wglu_ref, wqkv_ref, wsm_ref, cos_ref, sin_ref, cw_ref, cb_ref, lg_ref, lb_ref,
                  a2_ref, q_ref, k_ref, v_ref, qi_ref, ki_ref, wi_ref, abuf):
    i = pl.program_id(0)
    tm = h_ref.shape[0]
    hb = h_ref[...].astype(BF16)

    glu = _mm(hb, wglu_ref[...])
    a = glu[:, :CONV_CH] * jax.nn.sigmoid(glu[:, CONV_CH:])

    @pl.when(i == 0)
    def _():
        abuf[0:CONV_HALO, :] = jnp.zeros((CONV_HALO, CONV_CH), F32)

    abuf[pl.ds(CONV_HALO, tm), :] = a
    acc = jnp.broadcast_to(cb_ref[...], (tm, CONV_CH))
    for j in range(CONV_K):
        acc = acc + cw_ref[j:j + 1, :] * abuf[pl.ds(CONV_HALO - (CONV_K - 1) + j, tm), :]
    abuf[0:CONV_HALO, :] = abuf[pl.ds(tm, CONV_HALO), :]
    y = _layer_norm(acc, lg_ref[...], lb_ref[...])
    a2_ref[...] = (y * jax.nn.sigmoid(y)).astype(BF16)

    cos = cos_ref[...]
    sin = sin_ref[...]
    lane = lax.broadcasted_iota(I32, (tm, LANE), 1)
    first_half = (lane % HEAD_DIM) < (HEAD_DIM // 2)
    qkv = _mm(hb, wqkv_ref[...])
    for g in range(ATT_WIDTH // LANE):
        sl = slice(g * LANE, (g + 1) * LANE)
        qg = qkv[:, g * LANE:(g + 1) * LANE]
        kg = qkv[:, ATT_WIDTH + g * LANE:ATT_WIDTH + (g + 1) * LANE]
        ig = qkv[:, 3 * ATT_WIDTH + g * LANE:3 * ATT_WIDTH + (g + 1) * LANE]
        q_ref[:, sl] = (_rope_group(qg, cos, sin, first_half) * (LOG2_E * HEAD_DIM ** -0.5)).astype(BF16)
        k_ref[:, sl] = _rope_group(kg, cos, sin, first_half).astype(BF16)
        qi_ref[:, sl] = (_rope_group(ig, cos, sin, first_half) * (IDX_DIM ** -0.5)).astype(BF16)
    v_ref[...] = qkv[:, 2 * ATT_WIDTH:3 * ATT_WIDTH].astype(BF16)

    sm = _mm(hb, wsm_ref[...])
    ki = _rope_group(sm, cos, sin, first_half)
    ki_ref[...] = jnp.where(lane < IDX_DIM, ki, 0.0).astype(BF16)
    wi_ref[...] = sm * (IDX_HEADS ** -0.5)


def _l0_in(h, wglu, wqkv, wsm, cos, sin, cw, cb, lg, lb):
    lp, d = h.shape
    tm = _tile_rows(lp)
    row = lambda w: pl.BlockSpec((tm, w), lambda i: (i, 0))
    full = lambda a: pl.BlockSpec(a.shape, lambda i: (0,) * a.ndim)
    outs = [jax.ShapeDtypeStruct((lp, CONV_CH), BF16)] + [jax.ShapeDtypeStruct((lp, ATT_WIDTH), BF16)] * 4 + [
        jax.ShapeDtypeStruct((lp, LANE), BF16), jax.ShapeDtypeStruct((lp, LANE), F32)]
    return pl.pallas_call(
        _l0_in_kernel,
        grid=(lp // tm,),
        in_specs=[row(d), full(wglu), full(wqkv), full(wsm), row(LANE), row(LANE), full(cw), full(cb), full(lg),
                  full(lb)],
        out_specs=[row(CONV_CH)] + [row(ATT_WIDTH)] * 4 + [row(LANE), row(LANE)],
        out_shape=outs,
        scratch_shapes=[pltpu.VMEM((CONV_HALO + tm, CONV_CH), F32)],
        compiler_params=_params(("arbitrary",), 48),
        name="l0_in",
    )(h, wglu, wqkv, wsm, cos, sin, cw, cb, lg, lb)


def _dsa_kernel(q_ref, qi_ref, wi_ref, k_ref, v_ref, ki_ref, u_ref, o_ref,
                s_ref, qi8_ref, qh_ref, wb_ref, m_ref, acc_ref, *, topk):
    i = pl.program_id(0)
    bq = q_ref.shape[0]
    ck = KEY_CHUNK
    nch = ((i + 1) * bq + ck - 1) // ck
    lane = lax.broadcasted_iota(I32, (bq, LANE), 1)
    low = lane < HEAD_DIM

    for g in range(ATT_WIDTH // LANE):
        pair = qi_ref[:, g * LANE:(g + 1) * LANE].astype(F32)
        qi8_ref[pl.ds((2 * g) * bq, bq), :] = jnp.where(low, pair, 0.0).astype(BF16)
        qi8_ref[pl.ds((2 * g + 1) * bq, bq), :] = jnp.where(low, pltpu.roll(pair, HEAD_DIM, 1), 0.0).astype(BF16)
        qp = q_ref[:, g * LANE:(g + 1) * LANE].astype(F32)
        qh_ref[g, 0:bq, :] = jnp.where(low, qp, 0.0).astype(BF16)
        qh_ref[g, bq:2 * bq, :] = jnp.where(low, 0.0, qp).astype(BF16)
    wi = wi_ref[...]
    for h in range(IDX_HEADS):
        wb_ref[h] = jnp.broadcast_to(wi[:, IDX_DIM + h:IDX_DIM + h + 1], (bq, ck))

    ngrp = ck // LANE

    def index_scores(c, masked, lohi):
        smin, smax = lohi
        kic = ki_ref[pl.ds(pl.multiple_of(c * ck, ck), ck), :]
        lg = lax.dot_general(qi8_ref[...], kic, _NT, preferred_element_type=F32)
        sc = jnp.zeros((bq, ck), F32)
        for h in range(IDX_HEADS):
            sc = sc + jnp.maximum(lg[h * bq:(h + 1) * bq, :], 0.0) * wb_ref[h]
        bits = lax.bitcast_convert_type(sc, I32)
        key = bits ^ ((bits >> 31) & jnp.int32(0x7FFFFFFF))
        sc_lo = sc_hi = sc
        if masked:
            kpos = c * ck + lax.broadcasted_iota(I32, (bq, ck), 1)
            qpos = i * bq + lax.broadcasted_iota(I32, (bq, ck), 0)
            valid = kpos <= qpos
            key = jnp.where(valid, key, jnp.int32(INT_MIN))
            sc_lo = jnp.where(valid, sc, jnp.inf)
            sc_hi = jnp.where(valid, sc, -jnp.inf)
        s_ref[c] = key
        for j in range(ngrp):
            smin = jnp.minimum(smin, sc_lo[:, j * LANE:(j + 1) * LANE])
            smax = jnp.maximum(smax, sc_hi[:, j * LANE:(j + 1) * LANE])
        return smin, smax

    lohi = lax.fori_loop(0, nch - 1, lambda c, lohi: index_scores(c, False, lohi),
                         (jnp.full((bq, LANE), jnp.inf, F32), jnp.full((bq, LANE), -jnp.inf, F32)))
    smin, smax = index_scores(nch - 1, True, lohi)

    def to_key(x):
        bits = lax.bitcast_convert_type(x, I32)
        return bits ^ ((bits >> 31) & jnp.int32(0x7FFFFFFF))

    def count_ge(cand):
        def body(c, cnt):
            blk = s_ref[c]
            for g in range(ngrp):
                cnt = cnt + jnp.where(blk[:, g * LANE:(g + 1) * LANE] >= cand, 1.0, 0.0)
            return cnt
        cnt = lax.fori_loop(0, nch, body, jnp.zeros((bq, LANE), F32))
        return jnp.sum(cnt, axis=1, keepdims=True)

    kf = float(topk)
    zeros = jnp.zeros((bq, LANE), F32)
    n_valid = (i * bq + lax.broadcasted_iota(I32, (bq, LANE), 0) + 1).astype(F32)
    lo0 = to_key(jnp.min(smin, axis=1, keepdims=True) + zeros)
    hi0 = to_key(jnp.max(smax, axis=1, keepdims=True) + zeros) + 1

    def open_rows(lo, hi, n_lo):
        return jnp.where(n_lo > kf, jnp.where(hi > lo + 1, 1.0, 0.0), 0.0)

    def bisect_cond(st):
        it, _, _, _, any_open = st
        return jnp.logical_and(it < 34, any_open)

    def bisect_body(st):
        it, lo, hi, n_lo, _ = st
        is_open = open_rows(lo, hi, n_lo) > 0.0
        mid = (lo >> 1) + (hi >> 1) + (lo & hi & 1)
        cand = jnp.where(is_open, mid, lo)
        n = count_ge(cand) + zeros
        ge = n >= kf
        lo2 = jnp.where(is_open, jnp.where(ge, cand, lo), lo)
        hi2 = jnp.where(is_open, jnp.where(ge, hi, cand), hi)
        n2 = jnp.where(is_open, jnp.where(ge, n, n_lo), n_lo)
        return it + 1, lo2, hi2, n2, jnp.max(open_rows(lo2, hi2, n2)) > 0.0

    _, lo_f, _, n_lo, _ = lax.while_loop(
        bisect_cond, bisect_body, (jnp.int32(0), lo0, hi0, n_valid, jnp.max(open_rows(lo0, hi0, n_valid)) > 0.0))
    thr = jnp.where(n_valid <= kf, jnp.int32(INT_MIN), lo_f)
    thr1 = thr[:, 0:1]
    thr_b = jnp.broadcast_to(thr1, (bq, ck))

    m_ref[...] = jnp.full(m_ref.shape, M_INIT, F32)
    acc_ref[...] = jnp.zeros(acc_ref.shape, F32)
    ones = jnp.ones((ck, LANE), BF16)

    def attend_pairs(c, bias):
        rows = pl.ds(pl.multiple_of(c * ck, ck), ck)
        bias2 = jnp.concatenate([bias, bias], axis=0)
        for g in range(ATT_WIDTH // LANE):
            cols = slice(g * LANE, (g + 1) * LANE)
            s2 = lax.dot_general(qh_ref[g], k_ref[rows, cols], _NT, preferred_element_type=F32) + bias2
            sj = [s2[:, j * LANE:(j + 1) * LANE] for j in range(ngrp)]
            mx = sj[0]
            for j in range(1, ngrp):
                mx = jnp.maximum(mx, sj[j])
            m_prev = m_ref[g]
            m_new = jnp.maximum(m_prev, jnp.max(mx, axis=1, keepdims=True))
            alpha = jnp.exp2(m_prev - m_new)
            p2 = jnp.concatenate([jnp.exp2(s - m_new) for s in sj], axis=1).astype(BF16)
            v_aug = jnp.concatenate([v_ref[rows, cols], ones], axis=1)
            acc_ref[g] = jnp.concatenate([alpha, alpha], axis=1) * acc_ref[g] + _mm(p2, v_aug)
            m_ref[g] = m_new

    has_ties = jnp.max(n_lo) > kf

    @pl.when(jnp.logical_not(has_ties))
    def _():
        floor_b = jnp.maximum(thr_b, jnp.int32(INT_MIN + 1))

        def attend(c, carry):
            attend_pairs(c, jnp.where(s_ref[c] >= floor_b, 0.0, NEG_MASK))
            return carry

        lax.fori_loop(0, nch, attend, 0)

    @pl.when(has_ties)
    def _():
        n_above = count_ge(thr + 1)
        need = jnp.where(thr1 == INT_MIN, 0.0, kf - n_above)
        need_b = jnp.broadcast_to(need, (bq, ck))

        def attend(c, ties_seen):
            keys = s_ref[c]
            eq = keys == thr_b
            prefix = _mm(jnp.where(eq, 1.0, 0.0).astype(BF16), u_ref[...]) + ties_seen
            take_tie = jnp.where(eq, jnp.where(prefix <= need_b, 0.0, NEG_MASK), NEG_MASK)
            attend_pairs(c, jnp.where(keys > thr_b, 0.0, take_tie))
            return prefix[:, ck - 1:ck]

        lax.fori_loop(0, nch, attend, jnp.zeros((bq, 1), F32))

    for g in range(ATT_WIDTH // LANE):
        acc = acc_ref[g]
        o2 = acc[:, :LANE] * (1.0 / acc[:, LANE:])
        o_ref[:, g * LANE:(g + 1) * LANE] = jnp.where(low, o2[:bq], o2[bq:]).astype(BF16)


def _dsa(q, qi, wi, k, v, ki, topk):
    lp = q.shape[0]
    lk = k.shape[0]
    bq, ck = Q_BLOCK, KEY_CHUNK
    assert topk <= ck and lk % ck == 0 and lk >= lp
    u = (np.arange(ck)[:, None] <= np.arange(ck)[None, :]).astype(np.float32)
    u = jnp.asarray(u, BF16)
    row = lambda w: pl.BlockSpec((bq, w), lambda i: (i, 0))
    res = lambda a: pl.BlockSpec(a.shape, lambda i: (0, 0), pipeline_mode=pl.Buffered(1))
    return pl.pallas_call(
        functools.partial(_dsa_kernel, topk=topk),
        grid=(lp // bq,),
        in_specs=[row(ATT_WIDTH), row(ATT_WIDTH), row(LANE), res(k), res(v), res(ki), res(u)],
        out_specs=row(ATT_WIDTH),
        out_shape=jax.ShapeDtypeStruct((lp, ATT_WIDTH), BF16),
        scratch_shapes=[
            pltpu.VMEM((lk // ck, bq, ck), I32),
            pltpu.VMEM((IDX_HEADS * bq, LANE), BF16),
            pltpu.VMEM((ATT_HEADS // 2, 2 * bq, LANE), BF16),
            pltpu.VMEM((IDX_HEADS, bq, ck), F32),
            pltpu.VMEM((ATT_HEADS // 2, 2 * bq, LANE), F32),
            pltpu.VMEM((ATT_HEADS // 2, 2 * bq, 2 * LANE), F32),
        ],
        compiler_params=_params(("arbitrary",), 60),
        name="dsa",
    )(q, qi, wi, k, v, ki, u)


def _l0_out_kernel(a2_ref, o_ref, wa_ref, wo_ref, h_ref, g_ref, b_ref, out_ref):
    m = _mm(a2_ref[...], wa_ref[...]) + _mm(o_ref[...], wo_ref[...])
    out_ref[...] = _layer_norm(DN_ALPHA * h_ref[...] + m, g_ref[...], b_ref[...])


def _l0_out(a2, o, wa, wo, h, g, b):
    lp, d = h.shape
    tm = _tile_rows(lp)
    row = lambda w: pl.BlockSpec((tm, w), lambda i: (i, 0))
    full = lambda a: pl.BlockSpec(a.shape, lambda i: (0,) * a.ndim)
    return pl.pallas_call(
        _l0_out_kernel,
        grid=(lp // tm,),
        in_specs=[row(CONV_CH), row(ATT_WIDTH), full(wa), full(wo), row(d), full(g), full(b)],
        out_specs=row(d),
        out_shape=jax.ShapeDtypeStruct((lp, d), F32),
        compiler_params=_params(("parallel",), 32),
        name="l0_out",
    )(a2, o, wa, wo, h, g, b)


def _l1_kernel(h_ref, win_ref, cw_ref, cb_ref, wa_ref, ba_ref, wx_ref, bx_ref, lam_ref, wout_ref, g_ref, b_ref,
               out_ref, xbuf, hstate):
    i = pl.program_id(0)
    tm = h_ref.shape[0]
    h = h_ref[...]
    z = _mm(h.astype(BF16), win_ref[...])
    gate = z[:, :RNN_WIDTH]

    @pl.when(i == 0)
    def _():
        xbuf[0:RNN_HALO, :] = jnp.zeros((RNN_HALO, RNN_WIDTH), F32)
        hstate[...] = jnp.zeros(hstate.shape, F32)

    xbuf[pl.ds(RNN_HALO, tm), :] = z[:, RNN_WIDTH:]
    xc = jnp.broadcast_to(cb_ref[...], (tm, RNN_WIDTH))
    for j in range(RNN_CONV_K):
        xc = xc + cw_ref[j:j + 1, :] * xbuf[pl.ds(RNN_HALO - (RNN_CONV_K - 1) + j, tm), :]
    xbuf[0:RNN_HALO, :] = xbuf[pl.ds(tm, RNN_HALO), :]

    xcb = xc.astype(BF16)
    ra, ri = [], []
    for n in range(RNN_BLOCKS):
        blk = xcb[:, n * RNN_BLOCK_W:(n + 1) * RNN_BLOCK_W]
        ra.append(_mm(blk, wa_ref[n]))
        ri.append(_mm(blk, wx_ref[n]))
    r = jax.nn.sigmoid(jnp.concatenate(ra, axis=1) + ba_ref[...])
    ig = jax.nn.sigmoid(jnp.concatenate(ri, axis=1) + bx_ref[...])
    nl = -lam_ref[...]
    softplus = jnp.maximum(nl, 0.0) + jnp.log(1.0 + jnp.exp(-jnp.abs(nl)))
    log_a = -RG_C * r * softplus
    a = jnp.exp(log_a)
    u = jnp.sqrt(1.0 - jnp.exp(2.0 * log_a)) * (ig * xc)

    rows = lax.broadcasted_iota(I32, (tm, RNN_WIDTH), 0)
    d = 1
    while d < tm:
        keep = rows >= d
        a_sh = jnp.where(keep, pltpu.roll(a, d, 0), 1.0)
        u_sh = jnp.where(keep, pltpu.roll(u, d, 0), 0.0)
        u = a * u_sh + u
        a = a * a_sh
        d *= 2
    hs = u + a * hstate[0:1, :]
    hstate[0:1, :] = hs[tm - 1:tm, :]

    y = (jax.nn.gelu(gate) * hs).astype(BF16)
    m = _mm(y, wout_ref[...])
    out_ref[...] = _layer_norm(DN_ALPHA * h + m, g_ref[...], b_ref[...])


def _l1_mixer(h, win, cw, cb, wa, ba, wx, bx, lam, wout, g, b):
    lp, d = h.shape
    tm = _tile_rows(lp)
    row = lambda w: pl.BlockSpec((tm, w), lambda i: (i, 0))
    full = lambda a: pl.BlockSpec(a.shape, lambda i: (0,) * a.ndim)
    args = (h, win, cw, cb, wa, ba, wx, bx, lam, wout, g, b)
    return pl.pallas_call(
        _l1_kernel,
        grid=(lp // tm,),
        in_specs=[row(d)] + [full(a) for a in args[1:]],
        out_specs=row(d),
        out_shape=jax.ShapeDtypeStruct((lp, d), F32),
        scratch_shapes=[pltpu.VMEM((RNN_HALO + tm, RNN_WIDTH), F32), pltpu.VMEM((8, RNN_WIDTH), F32)],
        compiler_params=_params(("arbitrary",), 56),
        name="l1_mixer",
    )(*args)


def _router_kernel(h_ref, wr_ref, br_ref, ltri_ref, eid_ref, gate_ref, rank_ref, cnt_ref, carry_ref, *, n_real):
    i = pl.program_id(0)
    tm = h_ref.shape[0]

    @pl.when(i == 0)
    def _():
        carry_ref[...] = jnp.zeros(carry_ref.shape, F32)

    logits = jnp.dot(h_ref[...], wr_ref[...], precision=lax.Precision.HIGHEST,
                     preferred_element_type=F32) + br_ref[...]
    lane = lax.broadcasted_iota(I32, (tm, LANE), 1).astype(F32)
    ninf = -jnp.inf
    big = float(LANE)

    gl = jnp.where(lane < N_GROUPS, logits[:, :LANE], ninf)
    gmax = jnp.max(gl, axis=1, keepdims=True)
    g_p = 1.0 / jnp.sum(jnp.exp(gl - gmax), axis=1, keepdims=True)
    g_idx = jnp.min(jnp.where(gl == gmax, lane, big), axis=1, keepdims=True)

    first = g_idx * EXPERTS_PER_GROUP
    el = logits[:, LANE:]
    m1 = jnp.where(lane >= first, jnp.where(lane < first + EXPERTS_PER_GROUP, el, ninf), ninf)
    t1 = jnp.max(m1, axis=1, keepdims=True)
    i1 = jnp.min(jnp.where(m1 == t1, lane, big), axis=1, keepdims=True)
    m2 = jnp.where(lane == i1, ninf, m1)
    t2 = jnp.max(m2, axis=1, keepdims=True)
    i2 = jnp.min(jnp.where(m2 == t2, lane, big), axis=1, keepdims=True)
    e2 = jnp.exp(t2 - t1)
    den = 1.0 / (1.0 + e2)

    tok = i * tm + lax.broadcasted_iota(I32, (tm, LANE), 0)
    valid = tok < n_real
    oh0 = jnp.where(valid, jnp.where(lane == i1, 1.0, 0.0), 0.0)
    oh1 = jnp.where(valid, jnp.where(lane == i2, 1.0, 0.0), 0.0)
    ohs = oh0 + oh1
    before = _mm(ltri_ref[...], ohs.astype(BF16)) + carry_ref[...]
    carry_ref[...] = carry_ref[...] + jnp.sum(ohs, axis=0, keepdims=True)
    cnt_ref[...] = carry_ref[...]

    eid_ref[:, 0:1] = i1.astype(I32)
    eid_ref[:, 1:2] = i2.astype(I32)
    gate_ref[:, 0:1] = g_p * den
    gate_ref[:, 1:2] = g_p * e2 * den
    rank_ref[:, 0:1] = jnp.sum(oh0 * before, axis=1, keepdims=True).astype(I32)
    rank_ref[:, 1:2] = jnp.sum(oh1 * before, axis=1, keepdims=True).astype(I32)


def _router(h, wr, br, n_real):
    lp, d = h.shape
    tm = _tile_rows(lp)
    ltri = jnp.asarray((np.arange(tm)[:, None] > np.arange(tm)[None, :]).astype(np.float32), BF16)
    row = lambda w: pl.BlockSpec((tm, w), lambda i: (i, 0))
    full = lambda a: pl.BlockSpec(a.shape, lambda i: (0,) * a.ndim)
    return pl.pallas_call(
        functools.partial(_router_kernel, n_real=n_real),
        grid=(lp // tm,),
        in_specs=[row(d), full(wr), full(br), full(ltri)],
        out_specs=[row(2), row(2), row(2), pl.BlockSpec((1, LANE), lambda i: (0, 0))],
        out_shape=[jax.ShapeDtypeStruct((lp, 2), I32), jax.ShapeDtypeStruct((lp, 2), F32),
                   jax.ShapeDtypeStruct((lp, 2), I32), jax.ShapeDtypeStruct((1, LANE), F32)],
        scratch_shapes=[pltpu.VMEM((1, LANE), F32)],
        compiler_params=_params(("arbitrary",), 32),
        name="moe_router",
    )(h, wr, br, ltri)


def _row_copy(src_ref, src_row, dst_ref, dst_row, sem):
    return pltpu.make_async_copy(src_ref.at[pl.ds(src_row, 1), :], dst_ref.at[pl.ds(dst_row, 1), :], sem)


def _dispatch_kernel(dest_ref, h_ref, xb_in_ref, xb_ref, sem, *, n_real):
    del xb_in_ref
    i = pl.program_id(0)
    tm = h_ref.shape[0]
    n_rows = jnp.clip(n_real - i * tm, 0, tm)

    def issue(r, carry):
        for s in range(2):
            _row_copy(h_ref, r, xb_ref, dest_ref[2 * (i * tm + r) + s], sem).start()
        return carry

    def drain(r, carry):
        for s in range(2):
            _row_copy(h_ref, 0, xb_ref, 0, sem).wait()
        return carry

    lax.fori_loop(0, n_rows, issue, 0)
    lax.fori_loop(0, n_rows, drain, 0)


def _dispatch(dest_flat, h, cap, n_real):
    lp, d = h.shape
    tm = _tile_rows(lp)
    xb0 = jnp.zeros((cap, d), F32)
    grid_spec = pltpu.PrefetchScalarGridSpec(
        num_scalar_prefetch=1,
        grid=(lp // tm,),
        in_specs=[pl.BlockSpec((tm, d), lambda i, dest: (i, 0)), pl.BlockSpec(memory_space=pl.ANY)],
        out_specs=pl.BlockSpec(memory_space=pl.ANY),
        scratch_shapes=[pltpu.SemaphoreType.DMA(())],
    )
    return pl.pallas_call(
        functools.partial(_dispatch_kernel, n_real=n_real),
        grid_spec=grid_spec,
        out_shape=jax.ShapeDtypeStruct((cap, d), F32),
        input_output_aliases={2: 0},
        compiler_params=pltpu.CompilerParams(dimension_semantics=("arbitrary",), has_side_effects=True),
        name="moe_dispatch",
    )(dest_flat, h, xb0)


def _experts_kernel(bexp_ref, nused_ref, xb_ref, wg_ref, wu_ref, wd_ref, yb_ref, wg_s, wu_s, wd_s):
    b = pl.program_id(0)
    prev = bexp_ref[jnp.maximum(b - 1, 0)]

    @pl.when((b == 0) | (bexp_ref[b] != prev))
    def _():
        wg_s[...] = wg_ref[...].astype(BF16)
        wu_s[...] = wu_ref[...].astype(BF16)
        wd_s[...] = wd_ref[...].astype(BF16)

    @pl.when(b < nused_ref[0])
    def _():
        x = xb_ref[...].astype(BF16)
        gt = _mm(x, wg_s[...])
        up = _mm(x, wu_s[...])
        mid = (gt * jax.nn.sigmoid(gt) * up).astype(BF16)
        yb_ref[...] = _mm(mid, wd_s[...])

    @pl.when(b >= nused_ref[0])
    def _():
        yb_ref[...] = jnp.zeros(yb_ref.shape, F32)


def _experts(bexp, nused, xb, wg, wu, wd):
    cap, d = xb.shape
    bm = EXPERT_ROWS
    wspec = lambda a: pl.BlockSpec((None,) + a.shape[1:], lambda b, bexp, nused: (bexp[b], 0, 0))
    grid_spec = pltpu.PrefetchScalarGridSpec(
        num_scalar_prefetch=2,
        grid=(cap // bm,),
        in_specs=[pl.BlockSpec((bm, d), lambda b, bexp, nused: (b, 0)), wspec(wg), wspec(wu), wspec(wd)],
        out_specs=pl.BlockSpec((bm, d), lambda b, bexp, nused: (b, 0)),
        scratch_shapes=[pltpu.VMEM(wg.shape[1:], BF16), pltpu.VMEM(wu.shape[1:], BF16),
                        pltpu.VMEM(wd.shape[1:], BF16)],
    )
    return pl.pallas_call(
        _experts_kernel,
        grid_spec=grid_spec,
        out_shape=jax.ShapeDtypeStruct((cap, d), F32),
        compiler_params=_params(("arbitrary",), 48),
        name="moe_experts",
    )(bexp, nused, xb, wg, wu, wd)


def _combine_kernel(dest_ref, yb_ref, gate_ref, h_ref, g_ref, b_ref, out_ref, ybuf, sem):
    i = pl.program_id(0)
    tm = h_ref.shape[0]

    def issue(r, carry):
        for s in range(2):
            _row_copy(yb_ref, dest_ref[2 * (i * tm + r) + s], ybuf.at[s], r, sem).start()
        return carry

    def drain(r, carry):
        for s in range(2):
            _row_copy(yb_ref, 0, ybuf.at[s], 0, sem).wait()
        return carry

    lax.fori_loop(0, tm, issue, 0)
    lax.fori_loop(0, tm, drain, 0)
    gate = gate_ref[...]
    y = gate[:, 0:1] * ybuf[0] + gate[:, 1:2] * ybuf[1]
    out_ref[...] = _layer_norm(DN_ALPHA * h_ref[...] + y, g_ref[...], b_ref[...])


def _combine(dest_flat, yb, gate, h, g, b):
    lp, d = h.shape
    tm = _tile_rows(lp)
    full = lambda a: pl.BlockSpec(a.shape, lambda i, dest: (0,) * a.ndim)
    grid_spec = pltpu.PrefetchScalarGridSpec(
        num_scalar_prefetch=1,
        grid=(lp // tm,),
        in_specs=[pl.BlockSpec(memory_space=pl.ANY), pl.BlockSpec((tm, 2), lambda i, dest: (i, 0)),
                  pl.BlockSpec((tm, d), lambda i, dest: (i, 0)), full(g), full(b)],
        out_specs=pl.BlockSpec((tm, d), lambda i, dest: (i, 0)),
        scratch_shapes=[pltpu.VMEM((2, tm, d), F32), pltpu.SemaphoreType.DMA(())],
    )
    return pl.pallas_call(
        _combine_kernel,
        grid_spec=grid_spec,
        out_shape=jax.ShapeDtypeStruct((lp, d), F32),
        compiler_params=_params(("arbitrary",), 32),
        name="moe_combine",
    )(dest_flat, yb, gate, h, g, b)


def _moe(h, n_real, wg, bg, we, be, w_gate, w_up, w_down, ln_g, ln_b):
    lp, d = h.shape
    bm = EXPERT_ROWS
    wr = jnp.zeros((d, 2 * LANE), F32).at[:, :N_GROUPS].set(wg).at[:, LANE:LANE + N_EXPERTS].set(we)
    br = jnp.zeros((1, 2 * LANE), F32).at[0, :N_GROUPS].set(bg).at[0, LANE:LANE + N_EXPERTS].set(be)
    eid, gate, rank, cnt = _router(h, wr, br, n_real)

    counts = cnt[0, :N_EXPERTS].astype(I32)
    padded = (counts + bm - 1) // bm * bm
    pend = jnp.cumsum(padded)
    pstart = pend - padded
    valid = (jnp.arange(lp) < n_real)[:, None]
    dest = jnp.where(valid, pstart[eid] + rank, 0).astype(I32)
    dest_flat = dest.reshape(-1)
    n_blocks = -(-(2 * n_real + N_EXPERTS * (bm - 1)) // bm)
    bexp = jnp.minimum(jnp.searchsorted(pend, jnp.arange(n_blocks, dtype=I32) * bm, side='right'),
                       N_EXPERTS - 1).astype(I32)
    nused = (pend[-1:] // bm).astype(I32)

    xb = _dispatch(dest_flat, h, n_blocks * bm, n_real)
    yb = _experts(bexp, nused, xb, w_gate, w_up, w_down)
    return _combine(dest_flat, yb, gate, h, ln_g, ln_b)


def kernel(x, meta_tokens, ab_w_in, ab_conv_w, ab_conv_b, ab_ln_g, ab_ln_b, ab_w_out, c_w_in, c_conv_w, c_conv_b, c_gate_a_w, c_gate_a_b, c_gate_x_w, c_gate_x_b, c_lambda, c_w_out, moe_router_group_w, moe_router_group_b, moe_router_expert_w, moe_router_expert_b, moe_w_gate, moe_w_up, moe_w_down, ln_mix_g, ln_mix_b, ln_ffn_g, ln_ffn_b):
    bsz, seq, d = x.shape
    assert bsz == 1, "kernel is written for batch 1"
    n_real = N_META + seq
    lp = -(-n_real // Q_BLOCK) * Q_BLOCK
    lk = -(-lp // KEY_CHUNK) * KEY_CHUNK
    topk = min(TOPK_MAX, seq // 4)
    row2 = lambda a: a.reshape(1, -1)

    h = jnp.concatenate([meta_tokens.astype(x.dtype), x[0], jnp.zeros((lp - n_real, d), x.dtype)], axis=0)

    half = HEAD_DIM // 2
    inv_freq = ROPE_THETA ** (-2.0 * jnp.arange(half, dtype=F32) / HEAD_DIM)
    ang = jnp.arange(lp, dtype=F32)[:, None] * inv_freq[None, :]
    cos = jnp.tile(jnp.cos(ang), (1, 4))
    sin = jnp.tile(jnp.concatenate([-jnp.sin(ang), jnp.sin(ang)], axis=1), (1, 2))

    for layer in range(DEPTH):
        j = layer // 2
        if layer % 2 == 0:
            w_in = ab_w_in[j]
            wglu = w_in[:, :2 * CONV_CH].astype(BF16)
            wqkv = w_in[:, 2 * CONV_CH:2 * CONV_CH + 4 * ATT_WIDTH].astype(BF16)
            wsm = jnp.zeros((d, LANE), F32).at[:, :IDX_DIM + IDX_HEADS].set(
                w_in[:, 2 * CONV_CH + 4 * ATT_WIDTH:]).astype(BF16)
            a2, q, k, v, qi, ki, wi = _l0_in(h, wglu, wqkv, wsm, cos, sin, ab_conv_w[j], row2(ab_conv_b[j]),
                                             row2(ab_ln_g[j]), row2(ab_ln_b[j]))
            pad = lambda t: jnp.pad(t, ((0, lk - lp), (0, 0)))
            o = _dsa(q, qi, wi, pad(k), pad(v), pad(ki), topk)
            w_out = ab_w_out[j].astype(BF16)
            h = _l0_out(a2, o, w_out[:CONV_CH], w_out[CONV_CH:], h, row2(ln_mix_g[layer]), row2(ln_mix_b[layer]))
        else:
            h = _l1_mixer(h, c_w_in[j].astype(BF16), c_conv_w[j], row2(c_conv_b[j]),
                          c_gate_a_w[j].astype(BF16), row2(c_gate_a_b[j]),
                          c_gate_x_w[j].astype(BF16), row2(c_gate_x_b[j]), row2(c_lambda[j]),
                          c_w_out[j].astype(BF16), row2(ln_mix_g[layer]), row2(ln_mix_b[layer]))
        h = _moe(h, n_real, moe_router_group_w[layer], moe_router_group_b[layer], moe_router_expert_w[layer],
                 moe_router_expert_b[layer], moe_w_gate[layer], moe_w_up[layer], moe_w_down[layer],
                 row2(ln_ffn_g[layer]), row2(ln_ffn_b[layer]))
    return h[N_META:n_real][None]
```

```python
import functools

import jax
import jax.numpy as jnp
import numpy as np
from jax import lax
from jax.experimental import pallas as pl
from jax.experimental.pallas import tpu as pltpu

F32 = jnp.float32
BF16 = jnp.bfloat16
I32 = jnp.int32

N_META = 16
CONV_CH = 512
CONV_K = 31
ATT_HEADS = 8
HEAD_DIM = 64
ATT_WIDTH = ATT_HEADS * HEAD_DIM
IDX_HEADS = 8
IDX_DIM = 64
TOPK_MAX = 256
ROPE_THETA = 10000.0
RNN_WIDTH = 1280
RNN_BLOCKS = 10
RNN_BLOCK_W = RNN_WIDTH // RNN_BLOCKS
RNN_CONV_K = 4
RG_C = 8.0
N_GROUPS = 4
EXPERTS_PER_GROUP = 8
N_EXPERTS = N_GROUPS * EXPERTS_PER_GROUP
D_EXPERT = 512
LN_EPS = 1e-5
DEPTH = 2
DN_ALPHA = (2 * DEPTH) ** 0.25

LANE = 128
VMEM_BYTES = 64 << 20

Q_BLOCK = 128
KEY_CHUNK = 512
CONV_HALO = 32
RNN_HALO = 8
EXPERT_ROWS = 256
ROW_DMA_UNROLL = 8
INT_MIN = -(2 ** 31)
NEG_MASK = -2e30
M_INIT = -1e30
LOG2_E = 1.4426950408889634

_NT = (((1,), (1,)), ((), ()))


def _tile_rows(n):
    for t in (512, 384, 256, 128):
        if n % t == 0:
            return t
    raise ValueError(n)


def _mm(a, b):
    return jnp.dot(a, b, preferred_element_type=F32)


def _layer_norm(x, g, b):
    mu = jnp.mean(x, axis=-1, keepdims=True)
    xc = x - mu
    var = jnp.mean(xc * xc, axis=-1, keepdims=True)
    return xc * lax.rsqrt(var + LN_EPS) * g + b


def _rope_group(t, cos, sin_signed, first_half):
    partner = jnp.where(first_half, pltpu.roll(t, LANE - 32, 1), pltpu.roll(t, 32, 1))
    return t * cos + partner * sin_signed


def _params(sem, vmem_mb):
    return pltpu.CompilerParams(dimension_semantics=sem, vmem_limit_bytes=vmem_mb << 20)


def _l0_in_kernel(h_ref, wglu_ref, wqkv_ref, wsm_ref, cos_ref, sin_ref, cw_ref, cb_ref, lg_ref, lb_ref,
                  a2_ref, q_ref, k_ref, v_ref, qi_ref, ki_ref, wi_ref, abuf):
    i = pl.program_id(0)
    tm = h_ref.shape[0]
    hb = h_ref[...].astype(BF16)

    glu = _mm(hb, wglu_ref[...])
    a = glu[:, :CONV_CH] * jax.nn.sigmoid(glu[:, CONV_CH:])

    @pl.when(i == 0)
    def _():
        abuf[0:CONV_HALO, :] = jnp.zeros((CONV_HALO, CONV_CH), F32)

    abuf[pl.ds(CONV_HALO, tm), :] = a
    acc = jnp.broadcast_to(cb_ref[...], (tm, CONV_CH))
    for j in range(CONV_K):
        acc = acc + cw_ref[j:j + 1, :] * abuf[pl.ds(CONV_HALO - (CONV_K - 1) + j, tm), :]
    abuf[0:CONV_HALO, :] = abuf[pl.ds(tm, CONV_HALO), :]
    y = _layer_norm(acc, lg_ref[...], lb_ref[...])
    a2_ref[...] = (y * jax.nn.sigmoid(y)).astype(BF16)

    cos = cos_ref[...]
    sin = sin_ref[...]
    lane = lax.broadcasted_iota(I32, (tm, LANE), 1)
    first_half = (lane % HEAD_DIM) < (HEAD_DIM // 2)
    qkv = _mm(hb, wqkv_ref[...])
    for g in range(ATT_WIDTH // LANE):
        sl = slice(g * LANE, (g + 1) * LANE)
        qg = qkv[:, g * LANE:(g + 1) * LANE]
        kg = qkv[:, ATT_WIDTH + g * LANE:ATT_WIDTH + (g + 1) * LANE]
        ig = qkv[:, 3 * ATT_WIDTH + g * LANE:3 * ATT_WIDTH + (g + 1) * LANE]
        q_ref[:, sl] = (_rope_group(qg, cos, sin, first_half) * (LOG2_E * HEAD_DIM ** -0.5)).astype(BF16)
        k_ref[:, sl] = _rope_group(kg, cos, sin, first_half).astype(BF16)
        qi_ref[:, sl] = (_rope_group(ig, cos, sin, first_half) * (IDX_DIM ** -0.5)).astype(BF16)
    v_ref[...] = qkv[:, 2 * ATT_WIDTH:3 * ATT_WIDTH].astype(BF16)

    sm = _mm(hb, wsm_ref[...])
    ki = _rope_group(sm, cos, sin, first_half)
    ki_ref[...] = jnp.where(lane < IDX_DIM, ki, 0.0).astype(BF16)
    wi_ref[...] = sm * (IDX_HEADS ** -0.5)


def _l0_in(h, wglu, wqkv, wsm, cos, sin, cw, cb, lg, lb):
    lp, d = h.shape
    tm = _tile_rows(lp)
    row = lambda w: pl.BlockSpec((tm, w), lambda i: (i, 0))
    full = lambda a: pl.BlockSpec(a.shape, lambda i: (0,) * a.ndim)
    outs = [jax.ShapeDtypeStruct((lp, CONV_CH), BF16)] + [jax.ShapeDtypeStruct((lp, ATT_WIDTH), BF16)] * 4 + [
        jax.ShapeDtypeStruct((lp, LANE), BF16), jax.ShapeDtypeStruct((lp, LANE), F32)]
    return pl.pallas_call(
        _l0_in_kernel,
        grid=(lp // tm,),
        in_specs=[row(d), full(wglu), full(wqkv), full(wsm), row(LANE), row(LANE), full(cw), full(cb), full(lg),
                  full(lb)],
        out_specs=[row(CONV_CH)] + [row(ATT_WIDTH)] * 4 + [row(LANE), row(LANE)],
        out_shape=outs,
        scratch_shapes=[pltpu.VMEM((CONV_HALO + tm, CONV_CH), F32)],
        compiler_params=_params(("arbitrary",), 48),
        name="l0_in",
    )(h, wglu, wqkv, wsm, cos, sin, cw, cb, lg, lb)


def _two_stage_chunks(nch, first, second):
    first(0, 0)

    def body(t, carry):
        c = 2 * t
        first(c + 1, 1)
        second(c, 0, False)
        first(c + 2, 0)
        second(c + 1, 1, False)
        return carry

    n_pairs = (nch - 1) // 2
    lax.fori_loop(0, n_pairs, body, 0)
    c0 = 2 * n_pairs

    @pl.when(nch - c0 == 2)
    def _():
        first(c0 + 1, 1)
        second(c0, 0, False)
        second(c0 + 1, 1, True)

    @pl.when(nch - c0 == 1)
    def _():
        second(c0, 0, True)


def _dsa_kernel(q_ref, qi_ref, wi_ref, k_ref, v_ref, ki_ref, u_ref, o_ref,
                s_ref, qi8_ref, qh_ref, wb_ref, m_ref, acc_ref, pbuf, lohi_ref, *, topk):
    i = pl.program_id(0)
    bq = q_ref.shape[0]
    ck = KEY_CHUNK
    ngrp = ck // LANE
    nch = ((i + 1) * bq + ck - 1) // ck
    lane = lax.broadcasted_iota(I32, (bq, LANE), 1)
    low = lane < HEAD_DIM
    chunk_rows = lambda c: pl.ds(pl.multiple_of(c * ck, ck), ck)

    for g in range(ATT_WIDTH // LANE):
        pair = qi_ref[:, g * LANE:(g + 1) * LANE].astype(F32)
        qi8_ref[pl.ds((2 * g) * bq, bq), :] = jnp.where(low, pair, 0.0).astype(BF16)
        qi8_ref[pl.ds((2 * g + 1) * bq, bq), :] = jnp.where(low, pltpu.roll(pair, HEAD_DIM, 1), 0.0).astype(BF16)
        qp = q_ref[:, g * LANE:(g + 1) * LANE].astype(F32)
        qh_ref[g, 0:bq, :] = jnp.where(low, qp, 0.0).astype(BF16)
        qh_ref[g, bq:2 * bq, :] = jnp.where(low, 0.0, qp).astype(BF16)
    wi = wi_ref[...]
    for h in range(IDX_HEADS):
        wb_ref[h] = jnp.broadcast_to(wi[:, IDX_DIM + h:IDX_DIM + h + 1], (bq, LANE))
    lohi_ref[0] = jnp.full((bq, LANE), jnp.inf, F32)
    lohi_ref[1] = jnp.full((bq, LANE), -jnp.inf, F32)

    def to_key(x):
        bits = lax.bitcast_convert_type(x, I32)
        return bits ^ ((bits >> 31) & jnp.int32(0x7FFFFFFF))

    def index_logits(c, slot):
        pbuf[slot] = lax.dot_general(qi8_ref[...], ki_ref[chunk_rows(c), :], _NT, preferred_element_type=F32)

    def index_keys(c, slot, last):
        smin = lohi_ref[0]
        smax = lohi_ref[1]
        for j in range(ngrp):
            cols = slice(j * LANE, (j + 1) * LANE)
            sc = jnp.zeros((bq, LANE), F32)
            for h in range(IDX_HEADS):
                sc = sc + jnp.maximum(pbuf[slot, h * bq:(h + 1) * bq, cols], 0.0) * wb_ref[h]
            key = to_key(sc)
            if last:
                kpos = c * ck + j * LANE + lane
                qpos = i * bq + lax.broadcasted_iota(I32, (bq, LANE), 0)
                valid = kpos <= qpos
                key = jnp.where(valid, key, jnp.int32(INT_MIN))
                smin = jnp.minimum(smin, jnp.where(valid, sc, jnp.inf))
                smax = jnp.maximum(smax, jnp.where(valid, sc, -jnp.inf))
            else:
                smin = jnp.minimum(smin, sc)
                smax = jnp.maximum(smax, sc)
            s_ref[c, :, cols] = key
        lohi_ref[0] = smin
        lohi_ref[1] = smax

    _two_stage_chunks(nch, index_logits, index_keys)

    def count_ge(cand):
        def body(c, cnt):
            blk = s_ref[c]
            for g in range(ngrp):
                cnt = cnt + jnp.where(blk[:, g * LANE:(g + 1) * LANE] >= cand, 1.0, 0.0)
            return cnt
        cnt = lax.fori_loop(0, nch, body, jnp.zeros((bq, LANE), F32))
        return jnp.sum(cnt, axis=1, keepdims=True)

    kf = float(topk)
    zeros = jnp.zeros((bq, LANE), F32)
    n_valid = (i * bq + lax.broadcasted_iota(I32, (bq, LANE), 0) + 1).astype(F32)
    lo0 = to_key(jnp.min(lohi_ref[0], axis=1, keepdims=True) + zeros)
    hi0 = to_key(jnp.max(lohi_ref[1], axis=1, keepdims=True) + zeros) + 1

    def open_rows(lo, hi, n_lo):
        return jnp.where(n_lo > kf, jnp.where(hi > lo + 1, 1.0, 0.0), 0.0)

    def bisect_cond(st):
        it, _, _, _, any_open = st
        return jnp.logical_and(it < 34, any_open)

    def bisect_body(st):
        it, lo, hi, n_lo, _ = st
        is_open = open_rows(lo, hi, n_lo) > 0.0
        mid = (lo >> 1) + (hi >> 1) + (lo & hi & 1)
        cand = jnp.where(is_open, mid, lo)
        n = count_ge(cand) + zeros
        ge = n >= kf
        lo2 = jnp.where(is_open, jnp.where(ge, cand, lo), lo)
        hi2 = jnp.where(is_open, jnp.where(ge, hi, cand), hi)
        n2 = jnp.where(is_open, jnp.where(ge, n, n_lo), n_lo)
        return it + 1, lo2, hi2, n2, jnp.max(open_rows(lo2, hi2, n2)) > 0.0

    _, lo_f, _, n_lo, _ = lax.while_loop(
        bisect_cond, bisect_body, (jnp.int32(0), lo0, hi0, n_valid, jnp.max(open_rows(lo0, hi0, n_valid)) > 0.0))
    thr = jnp.where(n_valid <= kf, jnp.int32(INT_MIN), lo_f)

    m_ref[...] = jnp.full(m_ref.shape, M_INIT, F32)
    acc_ref[...] = jnp.zeros(acc_ref.shape, F32)
    ones = jnp.ones((ck, LANE), BF16)
    pair_rows = lambda g: slice(g * 2 * bq, (g + 1) * 2 * bq)

    def masked_scores(c, slot, bias):
        bias2 = jnp.concatenate([jnp.concatenate(bias, axis=1)] * 2, axis=0)
        for g in range(ATT_WIDTH // LANE):
            kc = k_ref[chunk_rows(c), g * LANE:(g + 1) * LANE]
            pbuf[slot, pair_rows(g), :] = lax.dot_general(qh_ref[g], kc, _NT, preferred_element_type=F32) + bias2

    def softmax_pv(c, slot, last):
        del last
        for g in range(ATT_WIDTH // LANE):
            sj = [pbuf[slot, pair_rows(g), j * LANE:(j + 1) * LANE] for j in range(ngrp)]
            mx = sj[0]
            for j in range(1, ngrp):
                mx = jnp.maximum(mx, sj[j])
            m_prev = m_ref[g]
            m_new = jnp.maximum(m_prev, jnp.max(mx, axis=1, keepdims=True))
            alpha = jnp.exp2(m_prev - m_new)
            p2 = jnp.concatenate([jnp.exp2(s - m_new) for s in sj], axis=1).astype(BF16)
            v_aug = jnp.concatenate([v_ref[chunk_rows(c), g * LANE:(g + 1) * LANE], ones], axis=1)
            acc_ref[g] = jnp.concatenate([alpha, alpha], axis=1) * acc_ref[g] + _mm(p2, v_aug)
            m_ref[g] = m_new

    has_ties = jnp.max(n_lo) > kf

    @pl.when(jnp.logical_not(has_ties))
    def _():
        floor = jnp.maximum(thr, jnp.int32(INT_MIN + 1))

        def scores(c, slot):
            masked_scores(c, slot, [jnp.where(s_ref[c, :, j * LANE:(j + 1) * LANE] >= floor, 0.0, NEG_MASK)
                                    for j in range(ngrp)])

        _two_stage_chunks(nch, scores, softmax_pv)

    @pl.when(has_ties)
    def _():
        thr1 = thr[:, 0:1]
        n_above = count_ge(thr + 1)
        need = jnp.where(thr1 == INT_MIN, 0.0, kf - n_above)
        thr_b = jnp.broadcast_to(thr1, (bq, ck))
        need_b = jnp.broadcast_to(need, (bq, ck))

        def attend(c, ties_seen):
            keys = s_ref[c]
            eq = keys == thr_b
            prefix = _mm(jnp.where(eq, 1.0, 0.0).astype(BF16), u_ref[...]) + ties_seen
            take_tie = jnp.where(eq, jnp.where(prefix <= need_b, 0.0, NEG_MASK), NEG_MASK)
            bias = jnp.where(keys > thr_b, 0.0, take_tie)
            masked_scores(c, 0, [bias[:, j * LANE:(j + 1) * LANE] for j in range(ngrp)])
            softmax_pv(c, 0, False)
            return prefix[:, ck - 1:ck]

        lax.fori_loop(0, nch, attend, jnp.zeros((bq, 1), F32))

    for g in range(ATT_WIDTH // LANE):
        acc = acc_ref[g]
        o2 = acc[:, :LANE] * (1.0 / acc[:, LANE:])
        o_ref[:, g * LANE:(g + 1) * LANE] = jnp.where(low, o2[:bq], o2[bq:]).astype(BF16)


def _dsa(q, qi, wi, k, v, ki, topk):
    lp = q.shape[0]
    lk = k.shape[0]
    bq, ck = Q_BLOCK, KEY_CHUNK
    assert topk <= ck and lk % ck == 0 and lk >= lp
    u = (np.arange(ck)[:, None] <= np.arange(ck)[None, :]).astype(np.float32)
    u = jnp.asarray(u, BF16)
    row = lambda w: pl.BlockSpec((bq, w), lambda i: (i, 0))
    res = lambda a: pl.BlockSpec(a.shape, lambda i: (0, 0), pipeline_mode=pl.Buffered(1))
    return pl.pallas_call(
        functools.partial(_dsa_kernel, topk=topk),
        grid=(lp // bq,),
        in_specs=[row(ATT_WIDTH), row(ATT_WIDTH), row(LANE), res(k), res(v), res(ki), res(u)],
        out_specs=row(ATT_WIDTH),
        out_shape=jax.ShapeDtypeStruct((lp, ATT_WIDTH), BF16),
        scratch_shapes=[
            pltpu.VMEM((lk // ck, bq, ck), I32),
            pltpu.VMEM((IDX_HEADS * bq, LANE), BF16),
            pltpu.VMEM((ATT_HEADS // 2, 2 * bq, LANE), BF16),
            pltpu.VMEM((IDX_HEADS, bq, LANE), F32),
            pltpu.VMEM((ATT_HEADS // 2, 2 * bq, LANE), F32),
            pltpu.VMEM((ATT_HEADS // 2, 2 * bq, 2 * LANE), F32),
            pltpu.VMEM((2, IDX_HEADS * bq, ck), F32),
            pltpu.VMEM((2, bq, LANE), F32),
        ],
        compiler_params=_params(("arbitrary",), 60),
        name="dsa",
    )(q, qi, wi, k, v, ki, u)


def _l0_out_kernel(a2_ref, o_ref, wa_ref, wo_ref, h_ref, g_ref, b_ref, out_ref):
    m = _mm(a2_ref[...], wa_ref[...]) + _mm(o_ref[...], wo_ref[...])
    out_ref[...] = _layer_norm(DN_ALPHA * h_ref[...] + m, g_ref[...], b_ref[...])


def _l0_out(a2, o, wa, wo, h, g, b):
    lp, d = h.shape
    tm = _tile_rows(lp)
    row = lambda w: pl.BlockSpec((tm, w), lambda i: (i, 0))
    full = lambda a: pl.BlockSpec(a.shape, lambda i: (0,) * a.ndim)
    return pl.pallas_call(
        _l0_out_kernel,
        grid=(lp // tm,),
        in_specs=[row(CONV_CH), row(ATT_WIDTH), full(wa), full(wo), row(d), full(g), full(b)],
        out_specs=row(d),
        out_shape=jax.ShapeDtypeStruct((lp, d), F32),
        compiler_params=_params(("parallel",), 32),
        name="l0_out",
    )(a2, o, wa, wo, h, g, b)


def _l1_kernel(h_ref, win_ref, cw_ref, cb_ref, wa_ref, ba_ref, wx_ref, bx_ref, lam_ref, wout_ref, g_ref, b_ref,
               out_ref, xbuf, hstate):
    i = pl.program_id(0)
    tm = h_ref.shape[0]
    h = h_ref[...]
    z = _mm(h.astype(BF16), win_ref[...])
    gate = z[:, :RNN_WIDTH]

    @pl.when(i == 0)
    def _():
        xbuf[0:RNN_HALO, :] = jnp.zeros((RNN_HALO, RNN_WIDTH), F32)
        hstate[...] = jnp.zeros(hstate.shape, F32)

    xbuf[pl.ds(RNN_HALO, tm), :] = z[:, RNN_WIDTH:]
    xc = jnp.broadcast_to(cb_ref[...], (tm, RNN_WIDTH))
    for j in range(RNN_CONV_K):
        xc = xc + cw_ref[j:j + 1, :] * xbuf[pl.ds(RNN_HALO - (RNN_CONV_K - 1) + j, tm), :]
    xbuf[0:RNN_HALO, :] = xbuf[pl.ds(tm, RNN_HALO), :]

    xcb = xc.astype(BF16)
    ra, ri = [], []
    for n in range(RNN_BLOCKS):
        blk = xcb[:, n * RNN_BLOCK_W:(n + 1) * RNN_BLOCK_W]
        ra.append(_mm(blk, wa_ref[n]))
        ri.append(_mm(blk, wx_ref[n]))
    r = jax.nn.sigmoid(jnp.concatenate(ra, axis=1) + ba_ref[...])
    ig = jax.nn.sigmoid(jnp.concatenate(ri, axis=1) + bx_ref[...])
    nl = -lam_ref[...]
    softplus = jnp.maximum(nl, 0.0) + jnp.log(1.0 + jnp.exp(-jnp.abs(nl)))
    log_a = -RG_C * r * softplus
    a = jnp.exp(log_a)
    u = jnp.sqrt(1.0 - jnp.exp(2.0 * log_a)) * (ig * xc)

    rows = lax.broadcasted_iota(I32, (tm, RNN_WIDTH), 0)
    d = 1
    while d < tm:
        keep = rows >= d
        a_sh = jnp.where(keep, pltpu.roll(a, d, 0), 1.0)
        u_sh = jnp.where(keep, pltpu.roll(u, d, 0), 0.0)
        u = a * u_sh + u
        a = a * a_sh
        d *= 2
    hs = u + a * hstate[0:1, :]
    hstate[0:1, :] = hs[tm - 1:tm, :]

    y = (jax.nn.gelu(gate) * hs).astype(BF16)
    m = _mm(y, wout_ref[...])
    out_ref[...] = _layer_norm(DN_ALPHA * h + m, g_ref[...], b_ref[...])


def _l1_mixer(h, win, cw, cb, wa, ba, wx, bx, lam, wout, g, b):
    lp, d = h.shape
    tm = _tile_rows(lp)
    row = lambda w: pl.BlockSpec((tm, w), lambda i: (i, 0))
    full = lambda a: pl.BlockSpec(a.shape, lambda i: (0,) * a.ndim)
    args = (h, win, cw, cb, wa, ba, wx, bx, lam, wout, g, b)
    return pl.pallas_call(
        _l1_kernel,
        grid=(lp // tm,),
        in_specs=[row(d)] + [full(a) for a in args[1:]],
        out_specs=row(d),
        out_shape=jax.ShapeDtypeStruct((lp, d), F32),
        scratch_shapes=[pltpu.VMEM((RNN_HALO + tm, RNN_WIDTH), F32), pltpu.VMEM((8, RNN_WIDTH), F32)],
        compiler_params=_params(("arbitrary",), 56),
        name="l1_mixer",
    )(*args)


def _router_kernel(h_ref, wr_ref, br_ref, ltri_ref, eid_ref, gate_ref, rank_ref, cnt_ref, carry_ref, *, n_real):
    i = pl.program_id(0)
    tm = h_ref.shape[0]

    @pl.when(i == 0)
    def _():
        carry_ref[...] = jnp.zeros(carry_ref.shape, F32)

    logits = jnp.dot(h_ref[...], wr_ref[...], precision=lax.Precision.HIGHEST,
                     preferred_element_type=F32) + br_ref[...]
    lane = lax.broadcasted_iota(I32, (tm, LANE), 1).astype(F32)
    ninf = -jnp.inf
    big = float(LANE)

    gl = jnp.where(lane < N_GROUPS, logits[:, :LANE], ninf)
    gmax = jnp.max(gl, axis=1, keepdims=True)
    g_p = 1.0 / jnp.sum(jnp.exp(gl - gmax), axis=1, keepdims=True)
    g_idx = jnp.min(jnp.where(gl == gmax, lane, big), axis=1, keepdims=True)

    first = g_idx * EXPERTS_PER_GROUP
    el = logits[:, LANE:]
    m1 = jnp.where(lane >= first, jnp.where(lane < first + EXPERTS_PER_GROUP, el, ninf), ninf)
    t1 = jnp.max(m1, axis=1, keepdims=True)
    i1 = jnp.min(jnp.where(m1 == t1, lane, big), axis=1, keepdims=True)
    m2 = jnp.where(lane == i1, ninf, m1)
    t2 = jnp.max(m2, axis=1, keepdims=True)
    i2 = jnp.min(jnp.where(m2 == t2, lane, big), axis=1, keepdims=True)
    e2 = jnp.exp(t2 - t1)
    den = 1.0 / (1.0 + e2)

    tok = i * tm + lax.broadcasted_iota(I32, (tm, LANE), 0)
    valid = tok < n_real
    oh0 = jnp.where(valid, jnp.where(lane == i1, 1.0, 0.0), 0.0)
    oh1 = jnp.where(valid, jnp.where(lane == i2, 1.0, 0.0), 0.0)
    ohs = oh0 + oh1
    before = _mm(ltri_ref[...], ohs.astype(BF16)) + carry_ref[...]
    carry_ref[...] = carry_ref[...] + jnp.sum(ohs, axis=0, keepdims=True)
    cnt_ref[...] = carry_ref[...]

    eid_ref[:, 0:1] = i1.astype(I32)
    eid_ref[:, 1:2] = i2.astype(I32)
    gate_ref[:, 0:1] = g_p * den
    gate_ref[:, 1:2] = g_p * e2 * den
    rank_ref[:, 0:1] = jnp.sum(oh0 * before, axis=1, keepdims=True).astype(I32)
    rank_ref[:, 1:2] = jnp.sum(oh1 * before, axis=1, keepdims=True).astype(I32)


def _router(h, wr, br, n_real):
    lp, d = h.shape
    tm = _tile_rows(lp)
    ltri = jnp.asarray((np.arange(tm)[:, None] > np.arange(tm)[None, :]).astype(np.float32), BF16)
    row = lambda w: pl.BlockSpec((tm, w), lambda i: (i, 0))
    full = lambda a: pl.BlockSpec(a.shape, lambda i: (0,) * a.ndim)
    return pl.pallas_call(
        functools.partial(_router_kernel, n_real=n_real),
        grid=(lp // tm,),
        in_specs=[row(d), full(wr), full(br), full(ltri)],
        out_specs=[row(2), row(2), row(2), pl.BlockSpec((1, LANE), lambda i: (0, 0))],
        out_shape=[jax.ShapeDtypeStruct((lp, 2), I32), jax.ShapeDtypeStruct((lp, 2), F32),
                   jax.ShapeDtypeStruct((lp, 2), I32), jax.ShapeDtypeStruct((1, LANE), F32)],
        scratch_shapes=[pltpu.VMEM((1, LANE), F32)],
        compiler_params=_params(("arbitrary",), 32),
        name="moe_router",
    )(h, wr, br, ltri)


def _row_copy(src_ref, src_row, dst_ref, dst_row, sem):
    return pltpu.make_async_copy(src_ref.at[pl.ds(src_row, 1), :], dst_ref.at[pl.ds(dst_row, 1), :], sem)


def _dispatch_kernel(dest_ref, h_ref, xb_in_ref, xb_ref, sem):
    del xb_in_ref
    i = pl.program_id(0)
    tm = h_ref.shape[0]

    def issue(r8, carry):
        for u in range(ROW_DMA_UNROLL):
            r = r8 * ROW_DMA_UNROLL + u
            for s in range(2):
                _row_copy(h_ref, r, xb_ref, dest_ref[2 * (i * tm + r) + s], sem).start()
        return carry

    lax.fori_loop(0, tm // ROW_DMA_UNROLL, issue, 0)
    for s in range(2):
        pltpu.make_async_copy(h_ref, xb_ref.at[pl.ds(0, tm), :], sem).wait()


def _dispatch(dest_flat, h, n_rows_out):
    lp, d = h.shape
    tm = _tile_rows(lp)
    xb0 = jnp.zeros((n_rows_out, d), F32)
    grid_spec = pltpu.PrefetchScalarGridSpec(
        num_scalar_prefetch=1,
        grid=(lp // tm,),
        in_specs=[pl.BlockSpec((tm, d), lambda i, dest: (i, 0)), pl.BlockSpec(memory_space=pl.ANY)],
        out_specs=pl.BlockSpec(memory_space=pl.ANY),
        scratch_shapes=[pltpu.SemaphoreType.DMA(())],
    )
    return pl.pallas_call(
        _dispatch_kernel,
        grid_spec=grid_spec,
        out_shape=jax.ShapeDtypeStruct((n_rows_out, d), F32),
        input_output_aliases={2: 0},
        compiler_params=pltpu.CompilerParams(dimension_semantics=("arbitrary",), has_side_effects=True),
        name="moe_dispatch",
    )(dest_flat, h, xb0)


def _experts_kernel(bexp_ref, nused_ref, xb_ref, wg_ref, wu_ref, wd_ref, yb_ref, wg_s, wu_s, wd_s):
    b = pl.program_id(0)
    prev = bexp_ref[jnp.maximum(b - 1, 0)]

    @pl.when((b == 0) | (bexp_ref[b] != prev))
    def _():
        wg_s[...] = wg_ref[...].astype(BF16)
        wu_s[...] = wu_ref[...].astype(BF16)
        wd_s[...] = wd_ref[...].astype(BF16)

    @pl.when(b < nused_ref[0])
    def _():
        x = xb_ref[...].astype(BF16)
        gt = _mm(x, wg_s[...])
        up = _mm(x, wu_s[...])
        mid = (gt * jax.nn.sigmoid(gt) * up).astype(BF16)
        yb_ref[...] = _mm(mid, wd_s[...])

    @pl.when(b >= nused_ref[0])
    def _():
        yb_ref[...] = jnp.zeros(yb_ref.shape, F32)


def _experts(bexp, nused, xb, wg, wu, wd, layer, n_blocks):
    d = xb.shape[1]
    bm = EXPERT_ROWS
    wspec = lambda a: pl.BlockSpec((None, None) + a.shape[2:], lambda b, bexp, nused: (layer, bexp[b], 0, 0))
    grid_spec = pltpu.PrefetchScalarGridSpec(
        num_scalar_prefetch=2,
        grid=(n_blocks,),
        in_specs=[pl.BlockSpec((bm, d), lambda b, bexp, nused: (b, 0)), wspec(wg), wspec(wu), wspec(wd)],
        out_specs=pl.BlockSpec((bm, d), lambda b, bexp, nused: (b, 0)),
        scratch_shapes=[pltpu.VMEM(wg.shape[2:], BF16), pltpu.VMEM(wu.shape[2:], BF16),
                        pltpu.VMEM(wd.shape[2:], BF16)],
    )
    return pl.pallas_call(
        _experts_kernel,
        grid_spec=grid_spec,
        out_shape=jax.ShapeDtypeStruct((n_blocks * bm, d), F32),
        compiler_params=_params(("arbitrary",), 48),
        name="moe_experts",
    )(bexp, nused, xb, wg, wu, wd)


def _combine_kernel(src_ref, yb_ref, gate_ref, h_ref, g_ref, b_ref, out_ref, ybuf, sem):
    i = pl.program_id(0)
    tm = h_ref.shape[0]

    def issue(r8, carry):
        for u in range(ROW_DMA_UNROLL):
            r = r8 * ROW_DMA_UNROLL + u
            for s in range(2):
                _row_copy(yb_ref, src_ref[2 * (i * tm + r) + s], ybuf.at[s], r, sem).start()
        return carry

    lax.fori_loop(0, tm // ROW_DMA_UNROLL, issue, 0)
    for s in range(2):
        pltpu.make_async_copy(yb_ref.at[pl.ds(0, tm), :], ybuf.at[s], sem).wait()
    gate = gate_ref[...]
    y = gate[:, 0:1] * ybuf[0] + gate[:, 1:2] * ybuf[1]
    out_ref[...] = _layer_norm(DN_ALPHA * h_ref[...] + y, g_ref[...], b_ref[...])


def _combine(dest_flat, yb, gate, h, g, b):
    lp, d = h.shape
    tm = _tile_rows(lp)
    full = lambda a: pl.BlockSpec(a.shape, lambda i, dest: (0,) * a.ndim)
    grid_spec = pltpu.PrefetchScalarGridSpec(
        num_scalar_prefetch=1,
        grid=(lp // tm,),
        in_specs=[pl.BlockSpec(memory_space=pl.ANY), pl.BlockSpec((tm, 2), lambda i, dest: (i, 0)),
                  pl.BlockSpec((tm, d), lambda i, dest: (i, 0)), full(g), full(b)],
        out_specs=pl.BlockSpec((tm, d), lambda i, dest: (i, 0)),
        scratch_shapes=[pltpu.VMEM((2, tm, d), F32), pltpu.SemaphoreType.DMA(())],
    )
    return pl.pallas_call(
        _combine_kernel,
        grid_spec=grid_spec,
        out_shape=jax.ShapeDtypeStruct((lp, d), F32),
        compiler_params=_params(("arbitrary",), 32),
        name="moe_combine",
    )(dest_flat, yb, gate, h, g, b)


def _moe(h, n_real, layer, wg, bg, we, be, w_gate, w_up, w_down, ln_g, ln_b):
    lp, d = h.shape
    bm = EXPERT_ROWS
    wr = jnp.zeros((d, 2 * LANE), F32).at[:, :N_GROUPS].set(wg).at[:, LANE:LANE + N_EXPERTS].set(we)
    br = jnp.zeros((1, 2 * LANE), F32).at[0, :N_GROUPS].set(bg).at[0, LANE:LANE + N_EXPERTS].set(be)
    eid, gate, rank, cnt = _router(h, wr, br, n_real)

    counts = cnt[0, :N_EXPERTS].astype(I32)
    padded = (counts + bm - 1) // bm * bm
    pend = jnp.cumsum(padded)
    pstart = pend - padded
    n_blocks = -(-(2 * n_real + N_EXPERTS * (bm - 1)) // bm)
    cap = n_blocks * bm
    tok = jnp.arange(lp, dtype=I32)[:, None]
    valid = tok < n_real
    row = pstart[eid] + rank
    dest_flat = jnp.where(valid, row, cap + 2 * (tok - n_real) + jnp.arange(2, dtype=I32)[None, :]).reshape(-1)
    src_flat = jnp.where(valid, row, 0).reshape(-1)
    block_start = jnp.arange(n_blocks, dtype=I32) * bm
    bexp = jnp.minimum(jnp.sum((pend[None, :] <= block_start[:, None]).astype(I32), axis=1), N_EXPERTS - 1)
    nused = (pend[-1:] // bm).astype(I32)

    xb = _dispatch(dest_flat, h, cap + 2 * (lp - n_real))
    yb = _experts(bexp, nused, xb, w_gate, w_up, w_down, layer, n_blocks)
    return _combine(src_flat, yb, gate, h, ln_g, ln_b)


def kernel(x, meta_tokens, ab_w_in, ab_conv_w, ab_conv_b, ab_ln_g, ab_ln_b, ab_w_out, c_w_in, c_conv_w, c_conv_b, c_gate_a_w, c_gate_a_b, c_gate_x_w, c_gate_x_b, c_lambda, c_w_out, moe_router_group_w, moe_router_group_b, moe_router_expert_w, moe_router_expert_b, moe_w_gate, moe_w_up, moe_w_down, ln_mix_g, ln_mix_b, ln_ffn_g, ln_ffn_b):
    bsz, seq, d = x.shape
    assert bsz == 1, "kernel is written for batch 1"
    n_real = N_META + seq
    lp = -(-n_real // Q_BLOCK) * Q_BLOCK
    lk = -(-lp // KEY_CHUNK) * KEY_CHUNK
    topk = min(TOPK_MAX, seq // 4)
    row2 = lambda a: a.reshape(1, -1)

    h = jnp.concatenate([meta_tokens.astype(x.dtype), x[0], jnp.zeros((lp - n_real, d), x.dtype)], axis=0)

    half = HEAD_DIM // 2
    inv_freq = ROPE_THETA ** (-2.0 * jnp.arange(half, dtype=F32) / HEAD_DIM)
    ang = jnp.arange(lp, dtype=F32)[:, None] * inv_freq[None, :]
    cos = jnp.tile(jnp.cos(ang), (1, 4))
    sin = jnp.tile(jnp.concatenate([-jnp.sin(ang), jnp.sin(ang)], axis=1), (1, 2))

    for layer in range(DEPTH):
        j = layer // 2
        if layer % 2 == 0:
            w_in = ab_w_in[j]
            wglu = w_in[:, :2 * CONV_CH].astype(BF16)
            wqkv = w_in[:, 2 * CONV_CH:2 * CONV_CH + 4 * ATT_WIDTH].astype(BF16)
            wsm = jnp.zeros((d, LANE), F32).at[:, :IDX_DIM + IDX_HEADS].set(
                w_in[:, 2 * CONV_CH + 4 * ATT_WIDTH:]).astype(BF16)
            a2, q, k, v, qi, ki, wi = _l0_in(h, wglu, wqkv, wsm, cos, sin, ab_conv_w[j], row2(ab_conv_b[j]),
                                             row2(ab_ln_g[j]), row2(ab_ln_b[j]))
            pad = lambda t: jnp.pad(t, ((0, lk - lp), (0, 0)))
            o = _dsa(q, qi, wi, pad(k), pad(v), pad(ki), topk)
            w_out = ab_w_out[j].astype(BF16)
            h = _l0_out(a2, o, w_out[:CONV_CH], w_out[CONV_CH:], h, row2(ln_mix_g[layer]), row2(ln_mix_b[layer]))
        else:
            h = _l1_mixer(h, c_w_in[j].astype(BF16), c_conv_w[j], row2(c_conv_b[j]),
                          c_gate_a_w[j].astype(BF16), row2(c_gate_a_b[j]),
                          c_gate_x_w[j].astype(BF16), row2(c_gate_x_b[j]), row2(c_lambda[j]),
                          c_w_out[j].astype(BF16), row2(ln_mix_g[layer]), row2(ln_mix_b[layer]))
        h = _moe(h, n_real, layer, moe_router_group_w[layer], moe_router_group_b[layer],
                 moe_router_expert_w[layer], moe_router_expert_b[layer], moe_w_gate, moe_w_up, moe_w_down,
                 row2(ln_ffn_g[layer]), row2(ln_ffn_b[layer]))
    return h[N_META:n_real][None]
```

```python
import functools

import jax
import jax.numpy as jnp
import numpy as np
from jax import lax
from jax.experimental import pallas as pl
from jax.experimental.pallas import tpu as pltpu

F32 = jnp.float32
BF16 = jnp.bfloat16
I32 = jnp.int32

N_META = 16
CONV_CH = 512
CONV_K = 31
ATT_HEADS = 8
HEAD_DIM = 64
ATT_WIDTH = ATT_HEADS * HEAD_DIM
IDX_HEADS = 8
IDX_DIM = 64
TOPK_MAX = 256
ROPE_THETA = 10000.0
RNN_WIDTH = 1280
RNN_BLOCKS = 10
RNN_BLOCK_W = RNN_WIDTH // RNN_BLOCKS
RNN_CONV_K = 4
RG_C = 8.0
N_GROUPS = 4
EXPERTS_PER_GROUP = 8
N_EXPERTS = N_GROUPS * EXPERTS_PER_GROUP
D_EXPERT = 512
LN_EPS = 1e-5
DEPTH = 2
DN_ALPHA = (2 * DEPTH) ** 0.25

LANE = 128
VMEM_BYTES = 64 << 20

Q_BLOCK = 128
KEY_CHUNK = 512
LIST_DEPTH = 12
CONV_HALO = 32
RNN_HALO = 8
EXPERT_ROWS = 256
ROW_DMA_UNROLL = 8
FLT_MAX = 3.4028234663852886e38
MIN_NORMAL_KEY = 1 << 23
NEG_MASK = -2e30
M_INIT = -1e30
LOG2_E = 1.4426950408889634

_NT = (((1,), (1,)), ((), ()))


def _tile_rows(n):
    for t in (512, 384, 256, 128):
        if n % t == 0:
            return t
    raise ValueError(n)


def _mm(a, b):
    return jnp.dot(a, b, preferred_element_type=F32)


def _layer_norm(x, g, b):
    mu = jnp.mean(x, axis=-1, keepdims=True)
    xc = x - mu
    var = jnp.mean(xc * xc, axis=-1, keepdims=True)
    return xc * lax.rsqrt(var + LN_EPS) * g + b


def _rope_group(t, cos, sin_signed, first_half):
    partner = jnp.where(first_half, pltpu.roll(t, LANE - 32, 1), pltpu.roll(t, 32, 1))
    return t * cos + partner * sin_signed


def _params(sem, vmem_mb):
    return pltpu.CompilerParams(dimension_semantics=sem, vmem_limit_bytes=vmem_mb << 20)


def _l0_in_kernel(h_ref, wglu_ref, wqkv_ref, wsm_ref, cos_ref, sin_ref, cw_ref, cb_ref, lg_ref, lb_ref,
                  a2_ref, q_ref, k_ref, v_ref, qi_ref, ki_ref, wi_ref, abuf):
    i = pl.program_id(0)
    tm = h_ref.shape[0]
    hb = h_ref[...].astype(BF16)

    glu = _mm(hb, wglu_ref[...])
    a = glu[:, :CONV_CH] * jax.nn.sigmoid(glu[:, CONV_CH:])

    @pl.when(i == 0)
    def _():
        abuf[0:CONV_HALO, :] = jnp.zeros((CONV_HALO, CONV_CH), F32)

    abuf[pl.ds(CONV_HALO, tm), :] = a
    acc = jnp.broadcast_to(cb_ref[...], (tm, CONV_CH))
    for j in range(CONV_K):
        acc = acc + cw_ref[j:j + 1, :] * abuf[pl.ds(CONV_HALO - (CONV_K - 1) + j, tm), :]
    abuf[0:CONV_HALO, :] = abuf[pl.ds(tm, CONV_HALO), :]
    y = _layer_norm(acc, lg_ref[...], lb_ref[...])
    a2_ref[...] = (y * jax.nn.sigmoid(y)).astype(BF16)

    cos = cos_ref[...]
    sin = sin_ref[...]
    lane = lax.broadcasted_iota(I32, (tm, LANE), 1)
    first_half = (lane % HEAD_DIM) < (HEAD_DIM // 2)
    qkv = _mm(hb, wqkv_ref[...])
    for g in range(ATT_WIDTH // LANE):
        sl = slice(g * LANE, (g + 1) * LANE)
        qg = qkv[:, g * LANE:(g + 1) * LANE]
        kg = qkv[:, ATT_WIDTH + g * LANE:ATT_WIDTH + (g + 1) * LANE]
        ig = qkv[:, 3 * ATT_WIDTH + g * LANE:3 * ATT_WIDTH + (g + 1) * LANE]
        q_ref[:, sl] = (_rope_group(qg, cos, sin, first_half) * (LOG2_E * HEAD_DIM ** -0.5)).astype(BF16)
        k_ref[:, sl] = _rope_group(kg, cos, sin, first_half).astype(BF16)
        qi_ref[:, sl] = (_rope_group(ig, cos, sin, first_half) * (IDX_DIM ** -0.5)).astype(BF16)
    v_ref[...] = qkv[:, 2 * ATT_WIDTH:3 * ATT_WIDTH].astype(BF16)

    sm = _mm(hb, wsm_ref[...])
    ki = _rope_group(sm, cos, sin, first_half)
    ki_ref[...] = jnp.where(lane < IDX_DIM, ki, 0.0).astype(BF16)
    wi_ref[...] = sm * (IDX_HEADS ** -0.5)


def _l0_in(h, wglu, wqkv, wsm, cos, sin, cw, cb, lg, lb):
    lp, d = h.shape
    tm = _tile_rows(lp)
    row = lambda w: pl.BlockSpec((tm, w), lambda i: (i, 0))
    full = lambda a: pl.BlockSpec(a.shape, lambda i: (0,) * a.ndim)
    outs = [jax.ShapeDtypeStruct((lp, CONV_CH), BF16)] + [jax.ShapeDtypeStruct((lp, ATT_WIDTH), BF16)] * 4 + [
        jax.ShapeDtypeStruct((lp, LANE), BF16), jax.ShapeDtypeStruct((lp, LANE), F32)]
    return pl.pallas_call(
        _l0_in_kernel,
        grid=(lp // tm,),
        in_specs=[row(d), full(wglu), full(wqkv), full(wsm), row(LANE), row(LANE), full(cw), full(cb), full(lg),
                  full(lb)],
        out_specs=[row(CONV_CH)] + [row(ATT_WIDTH)] * 4 + [row(LANE), row(LANE)],
        out_shape=outs,
        scratch_shapes=[pltpu.VMEM((CONV_HALO + tm, CONV_CH), F32)],
        compiler_params=_params(("arbitrary",), 48),
        name="l0_in",
    )(h, wglu, wqkv, wsm, cos, sin, cw, cb, lg, lb)


def _two_stage_chunks(nch, first, second):
    first(0, 0)

    def body(t, carry):
        c = 2 * t
        first(c + 1, 1)
        second(c, 0, False)
        first(c + 2, 0)
        second(c + 1, 1, False)
        return carry

    n_pairs = (nch - 1) // 2
    lax.fori_loop(0, n_pairs, body, 0)
    c0 = 2 * n_pairs

    @pl.when(nch - c0 == 2)
    def _():
        first(c0 + 1, 1)
        second(c0, 0, False)
        second(c0 + 1, 1, True)

    @pl.when(nch - c0 == 1)
    def _():
        second(c0, 0, True)


def _dsa_kernel(q_ref, qi_ref, wi_ref, k_ref, v_ref, ki_ref, u_ref, o_ref,
                s_ref, qi8_ref, qh_ref, wb_ref, m_ref, acc_ref, pbuf, lohi_ref, cand_ref, thr_ref, nlo_ref,
                *, topk):
    i = pl.program_id(0)
    bq = q_ref.shape[0]
    ck = KEY_CHUNK
    ngrp = ck // LANE
    nch = ((i + 1) * bq + ck - 1) // ck
    lane = lax.broadcasted_iota(I32, (bq, LANE), 1)
    low = lane < HEAD_DIM
    chunk_rows = lambda c: pl.ds(pl.multiple_of(c * ck, ck), ck)

    for g in range(ATT_WIDTH // LANE):
        pair = qi_ref[:, g * LANE:(g + 1) * LANE].astype(F32)
        qi8_ref[pl.ds((2 * g) * bq, bq), :] = jnp.where(low, pair, 0.0).astype(BF16)
        qi8_ref[pl.ds((2 * g + 1) * bq, bq), :] = jnp.where(low, pltpu.roll(pair, HEAD_DIM, 1), 0.0).astype(BF16)
        qp = q_ref[:, g * LANE:(g + 1) * LANE].astype(F32)
        qh_ref[g, 0:bq, :] = jnp.where(low, qp, 0.0).astype(BF16)
        qh_ref[g, bq:2 * bq, :] = jnp.where(low, 0.0, qp).astype(BF16)
    wi = wi_ref[...]
    for h in range(IDX_HEADS):
        wb_ref[h] = jnp.broadcast_to(wi[:, IDX_DIM + h:IDX_DIM + h + 1], (bq, LANE))
    lohi_ref[0] = jnp.full((bq, LANE), jnp.inf, F32)
    lohi_ref[1] = jnp.full((bq, LANE), -jnp.inf, F32)

    def to_key(x):
        bits = lax.bitcast_convert_type(x, I32)
        return bits ^ ((bits >> 31) & jnp.int32(0x7FFFFFFF))

    def from_key(k):
        k = jnp.where(k > 0, jnp.where(k < MIN_NORMAL_KEY, MIN_NORMAL_KEY, k),
                      jnp.where(k >= -MIN_NORMAL_KEY, 0, k))
        return lax.bitcast_convert_type(k ^ ((k >> 31) & jnp.int32(0x7FFFFFFF)), F32)

    def index_logits(c, slot):
        pbuf[slot] = lax.dot_general(qi8_ref[...], ki_ref[chunk_rows(c), :], _NT, preferred_element_type=F32)

    def index_scores(c, slot, last):
        smin = lohi_ref[0]
        smax = lohi_ref[1]
        for j in range(ngrp):
            cols = slice(j * LANE, (j + 1) * LANE)
            sc = jnp.zeros((bq, LANE), F32)
            for h in range(IDX_HEADS):
                sc = sc + jnp.maximum(pbuf[slot, h * bq:(h + 1) * bq, cols], 0.0) * wb_ref[h]
            if last:
                kpos = c * ck + j * LANE + lane
                qpos = i * bq + lax.broadcasted_iota(I32, (bq, LANE), 0)
                valid = kpos <= qpos
                smin = jnp.minimum(smin, jnp.where(valid, sc, jnp.inf))
                sc = jnp.where(valid, sc, -jnp.inf)
            else:
                smin = jnp.minimum(smin, sc)
            smax = jnp.maximum(smax, sc)
            s_ref[c, :, cols] = sc
        lohi_ref[0] = smin
        lohi_ref[1] = smax

    _two_stage_chunks(nch, index_logits, index_scores)

    def count_ge(cand):
        cand_f = from_key(cand)

        def body(c, cnt):
            blk = s_ref[c]
            for g in range(ngrp):
                cnt = cnt + jnp.where(blk[:, g * LANE:(g + 1) * LANE] >= cand_f, 1.0, 0.0)
            return cnt
        cnt = lax.fori_loop(0, nch, body, jnp.zeros((bq, LANE), F32))
        return jnp.sum(cnt, axis=1, keepdims=True)

    kf = float(topk)
    zeros = jnp.zeros((bq, LANE), F32)
    n_valid = (i * bq + lax.broadcasted_iota(I32, (bq, LANE), 0) + 1).astype(F32)
    lo0 = to_key(jnp.min(lohi_ref[0], axis=1, keepdims=True) + zeros)
    hi0 = to_key(jnp.max(lohi_ref[1], axis=1, keepdims=True) + zeros) + 1

    def open_rows(lo, hi, n_lo):
        return jnp.where(n_lo > kf, jnp.where(hi > lo + 1, 1.0, 0.0), 0.0)

    def bisect(count_fn):
        def cond(st):
            it, _, _, _, any_open = st
            return jnp.logical_and(it < 34, any_open)

        def body(st):
            it, lo, hi, n_lo, _ = st
            is_open = open_rows(lo, hi, n_lo) > 0.0
            mid = (lo >> 1) + (hi >> 1) + (lo & hi & 1)
            cand = jnp.where(is_open, mid, lo)
            n = count_fn(cand) + zeros
            ge = n >= kf
            lo2 = jnp.where(is_open, jnp.where(ge, cand, lo), lo)
            hi2 = jnp.where(is_open, jnp.where(ge, hi, cand), hi)
            n2 = jnp.where(is_open, jnp.where(ge, n, n_lo), n_lo)
            return it + 1, lo2, hi2, n2, jnp.max(open_rows(lo2, hi2, n2)) > 0.0

        st = lax.while_loop(cond, body, (jnp.int32(0), lo0, hi0, n_valid,
                                         jnp.max(open_rows(lo0, hi0, n_valid)) > 0.0))
        return st[1], st[3]

    def build_lists(slab, carry):
        rows8 = pl.ds(pl.multiple_of(slab * 8, 8), 8)

        def insert_chunk(c, lst):
            lst = list(lst)
            for j in range(ngrp):
                x = s_ref[c, rows8, j * LANE:(j + 1) * LANE]
                for d in range(LIST_DEPTH):
                    top = jnp.maximum(lst[d], x)
                    x = jnp.minimum(lst[d], x)
                    lst[d] = top
            return tuple(lst)

        lst = lax.fori_loop(0, nch, insert_chunk,
                            tuple(jnp.full((8, LANE), -jnp.inf, F32) for _ in range(LIST_DEPTH)))
        for d in range(LIST_DEPTH):
            cand_ref[d, rows8, :] = lst[d]
        return carry

    lax.fori_loop(0, bq // 8, build_lists, 0)

    def count_ge_lists(cand):
        cand_f = from_key(cand)
        cnt = jnp.zeros((bq, LANE), F32)
        for d in range(LIST_DEPTH):
            cnt = cnt + jnp.where(cand_ref[d] >= cand_f, 1.0, 0.0)
        return jnp.sum(cnt, axis=1, keepdims=True)

    lo_l, n_l = bisect(count_ge_lists)
    thr_ref[...] = lo_l
    nlo_ref[...] = n_l
    deepest = jnp.where(cand_ref[LIST_DEPTH - 1] >= from_key(lo_l), 1.0, 0.0)
    lists_short = jnp.max(jnp.where(n_valid > kf, deepest, 0.0)) > 0.0

    @pl.when(lists_short)
    def _():
        lo_a, n_a = bisect(count_ge)
        thr_ref[...] = lo_a
        nlo_ref[...] = n_a

    n_lo = nlo_ref[...]
    take_all = n_valid <= kf
    thr_key = thr_ref[...]
    thr = jnp.where(take_all, -FLT_MAX, from_key(thr_key))

    m_ref[...] = jnp.full(m_ref.shape, M_INIT, F32)
    acc_ref[...] = jnp.zeros(acc_ref.shape, F32)
    ones = jnp.ones((ck, LANE), BF16)
    pair_rows = lambda g: slice(g * 2 * bq, (g + 1) * 2 * bq)

    def masked_scores(c, slot, bias):
        bias2 = jnp.concatenate([jnp.concatenate(bias, axis=1)] * 2, axis=0)
        for g in range(ATT_WIDTH // LANE):
            kc = k_ref[chunk_rows(c), g * LANE:(g + 1) * LANE]
            pbuf[slot, pair_rows(g), :] = lax.dot_general(qh_ref[g], kc, _NT, preferred_element_type=F32) + bias2

    def softmax_pv(c, slot, last):
        del last
        for g in range(ATT_WIDTH // LANE):
            sj = [pbuf[slot, pair_rows(g), j * LANE:(j + 1) * LANE] for j in range(ngrp)]
            mx = sj[0]
            for j in range(1, ngrp):
                mx = jnp.maximum(mx, sj[j])
            m_prev = m_ref[g]
            m_new = jnp.maximum(m_prev, jnp.max(mx, axis=1, keepdims=True))
            alpha = jnp.exp2(m_prev - m_new)
            p2 = jnp.concatenate([jnp.exp2(s - m_new) for s in sj], axis=1).astype(BF16)
            v_aug = jnp.concatenate([v_ref[chunk_rows(c), g * LANE:(g + 1) * LANE], ones], axis=1)
            acc_ref[g] = jnp.concatenate([alpha, alpha], axis=1) * acc_ref[g] + _mm(p2, v_aug)
            m_ref[g] = m_new

    has_ties = jnp.max(n_lo) > kf

    @pl.when(jnp.logical_not(has_ties))
    def _():
        def scores(c, slot):
            masked_scores(c, slot, [jnp.where(s_ref[c, :, j * LANE:(j + 1) * LANE] >= thr, 0.0, NEG_MASK)
                                    for j in range(ngrp)])

        _two_stage_chunks(nch, scores, softmax_pv)

    @pl.when(has_ties)
    def _():
        n_above = count_ge(thr_key + 1)
        need = jnp.where(take_all[:, 0:1], 0.0, kf - n_above)
        thr_b = jnp.broadcast_to(thr[:, 0:1], (bq, ck))
        need_b = jnp.broadcast_to(need, (bq, ck))

        def attend(c, ties_seen):
            sc = s_ref[c]
            eq = sc == thr_b
            prefix = _mm(jnp.where(eq, 1.0, 0.0).astype(BF16), u_ref[...]) + ties_seen
            take_tie = jnp.where(eq, jnp.where(prefix <= need_b, 0.0, NEG_MASK), NEG_MASK)
            bias = jnp.where(sc > thr_b, 0.0, take_tie)
            masked_scores(c, 0, [bias[:, j * LANE:(j + 1) * LANE] for j in range(ngrp)])
            softmax_pv(c, 0, False)
            return prefix[:, ck - 1:ck]

        lax.fori_loop(0, nch, attend, jnp.zeros((bq, 1), F32))

    for g in range(ATT_WIDTH // LANE):
        acc = acc_ref[g]
        o2 = acc[:, :LANE] * (1.0 / acc[:, LANE:])
        o_ref[:, g * LANE:(g + 1) * LANE] = jnp.where(low, o2[:bq], o2[bq:]).astype(BF16)


def _dsa(q, qi, wi, k, v, ki, topk):
    lp = q.shape[0]
    lk = k.shape[0]
    bq, ck = Q_BLOCK, KEY_CHUNK
    assert topk <= ck and lk % ck == 0 and lk >= lp
    u = (np.arange(ck)[:, None] <= np.arange(ck)[None, :]).astype(np.float32)
    u = jnp.asarray(u, BF16)
    row = lambda w: pl.BlockSpec((bq, w), lambda i: (i, 0))
    res = lambda a: pl.BlockSpec(a.shape, lambda i: (0, 0), pipeline_mode=pl.Buffered(1))
    return pl.pallas_call(
        functools.partial(_dsa_kernel, topk=topk),
        grid=(lp // bq,),
        in_specs=[row(ATT_WIDTH), row(ATT_WIDTH), row(LANE), res(k), res(v), res(ki), res(u)],
        out_specs=row(ATT_WIDTH),
        out_shape=jax.ShapeDtypeStruct((lp, ATT_WIDTH), BF16),
        scratch_shapes=[
            pltpu.VMEM((lk // ck, bq, ck), F32),
            pltpu.VMEM((IDX_HEADS * bq, LANE), BF16),
            pltpu.VMEM((ATT_HEADS // 2, 2 * bq, LANE), BF16),
            pltpu.VMEM((IDX_HEADS, bq, LANE), F32),
            pltpu.VMEM((ATT_HEADS // 2, 2 * bq, LANE), F32),
            pltpu.VMEM((ATT_HEADS // 2, 2 * bq, 2 * LANE), F32),
            pltpu.VMEM((2, IDX_HEADS * bq, ck), F32),
            pltpu.VMEM((2, bq, LANE), F32),
            pltpu.VMEM((LIST_DEPTH, bq, LANE), F32),
            pltpu.VMEM((bq, LANE), I32),
            pltpu.VMEM((bq, LANE), F32),
        ],
        compiler_params=_params(("arbitrary",), 60),
        name="dsa",
    )(q, qi, wi, k, v, ki, u)


def _l0_out_kernel(a2_ref, o_ref, wa_ref, wo_ref, h_ref, g_ref, b_ref, out_ref):
    m = _mm(a2_ref[...], wa_ref[...]) + _mm(o_ref[...], wo_ref[...])
    out_ref[...] = _layer_norm(DN_ALPHA * h_ref[...] + m, g_ref[...], b_ref[...])


def _l0_out(a2, o, wa, wo, h, g, b):
    lp, d = h.shape
    tm = _tile_rows(lp)
    row = lambda w: pl.BlockSpec((tm, w), lambda i: (i, 0))
    full = lambda a: pl.BlockSpec(a.shape, lambda i: (0,) * a.ndim)
    return pl.pallas_call(
        _l0_out_kernel,
        grid=(lp // tm,),
        in_specs=[row(CONV_CH), row(ATT_WIDTH), full(wa), full(wo), row(d), full(g), full(b)],
        out_specs=row(d),
        out_shape=jax.ShapeDtypeStruct((lp, d), F32),
        compiler_params=_params(("parallel",), 32),
        name="l0_out",
    )(a2, o, wa, wo, h, g, b)


def _l1_kernel(h_ref, win_ref, cw_ref, cb_ref, wa_ref, ba_ref, wx_ref, bx_ref, lam_ref, wout_ref, g_ref, b_ref,
               out_ref, xbuf, hstate):
    i = pl.program_id(0)
    tm = h_ref.shape[0]
    h = h_ref[...]
    z = _mm(h.astype(BF16), win_ref[...])
    gate = z[:, :RNN_WIDTH]

    @pl.when(i == 0)
    def _():
        xbuf[0:RNN_HALO, :] = jnp.zeros((RNN_HALO, RNN_WIDTH), F32)
        hstate[...] = jnp.zeros(hstate.shape, F32)

    xbuf[pl.ds(RNN_HALO, tm), :] = z[:, RNN_WIDTH:]
    xc = jnp.broadcast_to(cb_ref[...], (tm, RNN_WIDTH))
    for j in range(RNN_CONV_K):
        xc = xc + cw_ref[j:j + 1, :] * xbuf[pl.ds(RNN_HALO - (RNN_CONV_K - 1) + j, tm), :]
    xbuf[0:RNN_HALO, :] = xbuf[pl.ds(tm, RNN_HALO), :]

    xcb = xc.astype(BF16)
    ra, ri = [], []
    for n in range(RNN_BLOCKS):
        blk = xcb[:, n * RNN_BLOCK_W:(n + 1) * RNN_BLOCK_W]
        ra.append(_mm(blk, wa_ref[n]))
        ri.append(_mm(blk, wx_ref[n]))
    r = jax.nn.sigmoid(jnp.concatenate(ra, axis=1) + ba_ref[...])
    ig = jax.nn.sigmoid(jnp.concatenate(ri, axis=1) + bx_ref[...])
    nl = -lam_ref[...]
    softplus = jnp.maximum(nl, 0.0) + jnp.log(1.0 + jnp.exp(-jnp.abs(nl)))
    log_a = -RG_C * r * softplus
    a = jnp.exp(log_a)
    u = jnp.sqrt(1.0 - jnp.exp(2.0 * log_a)) * (ig * xc)

    rows = lax.broadcasted_iota(I32, (tm, RNN_WIDTH), 0)
    d = 1
    while d < tm:
        keep = rows >= d
        a_sh = jnp.where(keep, pltpu.roll(a, d, 0), 1.0)
        u_sh = jnp.where(keep, pltpu.roll(u, d, 0), 0.0)
        u = a * u_sh + u
        a = a * a_sh
        d *= 2
    hs = u + a * hstate[0:1, :]
    hstate[0:1, :] = hs[tm - 1:tm, :]

    y = (jax.nn.gelu(gate) * hs).astype(BF16)
    m = _mm(y, wout_ref[...])
    out_ref[...] = _layer_norm(DN_ALPHA * h + m, g_ref[...], b_ref[...])


def _l1_mixer(h, win, cw, cb, wa, ba, wx, bx, lam, wout, g, b):
    lp, d = h.shape
    tm = _tile_rows(lp)
    row = lambda w: pl.BlockSpec((tm, w), lambda i: (i, 0))
    full = lambda a: pl.BlockSpec(a.shape, lambda i: (0,) * a.ndim)
    args = (h, win, cw, cb, wa, ba, wx, bx, lam, wout, g, b)
    return pl.pallas_call(
        _l1_kernel,
        grid=(lp // tm,),
        in_specs=[row(d)] + [full(a) for a in args[1:]],
        out_specs=row(d),
        out_shape=jax.ShapeDtypeStruct((lp, d), F32),
        scratch_shapes=[pltpu.VMEM((RNN_HALO + tm, RNN_WIDTH), F32), pltpu.VMEM((8, RNN_WIDTH), F32)],
        compiler_params=_params(("arbitrary",), 56),
        name="l1_mixer",
    )(*args)


def _router_kernel(h_ref, wr_ref, br_ref, ltri_ref, eid_ref, gate_ref, rank_ref, cnt_ref, carry_ref, *, n_real):
    i = pl.program_id(0)
    tm = h_ref.shape[0]

    @pl.when(i == 0)
    def _():
        carry_ref[...] = jnp.zeros(carry_ref.shape, F32)

    logits = jnp.dot(h_ref[...], wr_ref[...], precision=lax.Precision.HIGHEST,
                     preferred_element_type=F32) + br_ref[...]
    lane = lax.broadcasted_iota(I32, (tm, LANE), 1).astype(F32)
    ninf = -jnp.inf
    big = float(LANE)

    gl = jnp.where(lane < N_GROUPS, logits[:, :LANE], ninf)
    gmax = jnp.max(gl, axis=1, keepdims=True)
    g_p = 1.0 / jnp.sum(jnp.exp(gl - gmax), axis=1, keepdims=True)
    g_idx = jnp.min(jnp.where(gl == gmax, lane, big), axis=1, keepdims=True)

    first = g_idx * EXPERTS_PER_GROUP
    el = logits[:, LANE:]
    m1 = jnp.where(lane >= first, jnp.where(lane < first + EXPERTS_PER_GROUP, el, ninf), ninf)
    t1 = jnp.max(m1, axis=1, keepdims=True)
    i1 = jnp.min(jnp.where(m1 == t1, lane, big), axis=1, keepdims=True)
    m2 = jnp.where(lane == i1, ninf, m1)
    t2 = jnp.max(m2, axis=1, keepdims=True)
    i2 = jnp.min(jnp.where(m2 == t2, lane, big), axis=1, keepdims=True)
    e2 = jnp.exp(t2 - t1)
    den = 1.0 / (1.0 + e2)

    tok = i * tm + lax.broadcasted_iota(I32, (tm, LANE), 0)
    valid = tok < n_real
    oh0 = jnp.where(valid, jnp.where(lane == i1, 1.0, 0.0), 0.0)
    oh1 = jnp.where(valid, jnp.where(lane == i2, 1.0, 0.0), 0.0)
    ohs = oh0 + oh1
    before = _mm(ltri_ref[...], ohs.astype(BF16)) + carry_ref[...]
    carry_ref[...] = carry_ref[...] + jnp.sum(ohs, axis=0, keepdims=True)
    cnt_ref[...] = carry_ref[...]

    eid_ref[:, 0:1] = i1.astype(I32)
    eid_ref[:, 1:2] = i2.astype(I32)
    gate_ref[:, 0:1] = g_p * den
    gate_ref[:, 1:2] = g_p * e2 * den
    rank_ref[:, 0:1] = jnp.sum(oh0 * before, axis=1, keepdims=True).astype(I32)
    rank_ref[:, 1:2] = jnp.sum(oh1 * before, axis=1, keepdims=True).astype(I32)


def _router(h, wr, br, n_real):
    lp, d = h.shape
    tm = _tile_rows(lp)
    ltri = jnp.asarray((np.arange(tm)[:, None] > np.arange(tm)[None, :]).astype(np.float32), BF16)
    row = lambda w: pl.BlockSpec((tm, w), lambda i: (i, 0))
    full = lambda a: pl.BlockSpec(a.shape, lambda i: (0,) * a.ndim)
    return pl.pallas_call(
        functools.partial(_router_kernel, n_real=n_real),
        grid=(lp // tm,),
        in_specs=[row(d), full(wr), full(br), full(ltri)],
        out_specs=[row(2), row(2), row(2), pl.BlockSpec((1, LANE), lambda i: (0, 0))],
        out_shape=[jax.ShapeDtypeStruct((lp, 2), I32), jax.ShapeDtypeStruct((lp, 2), F32),
                   jax.ShapeDtypeStruct((lp, 2), I32), jax.ShapeDtypeStruct((1, LANE), F32)],
        scratch_shapes=[pltpu.VMEM((1, LANE), F32)],
        compiler_params=_params(("arbitrary",), 32),
        name="moe_router",
    )(h, wr, br, ltri)


def _row_copy(src_ref, src_row, dst_ref, dst_row, sem):
    return pltpu.make_async_copy(src_ref.at[pl.ds(src_row, 1), :], dst_ref.at[pl.ds(dst_row, 1), :], sem)


def _dispatch_kernel(dest_ref, h_ref, xb_in_ref, xb_ref, sem):
    del xb_in_ref
    i = pl.program_id(0)
    tm = h_ref.shape[0]

    def issue(r8, carry):
        for u in range(ROW_DMA_UNROLL):
            r = r8 * ROW_DMA_UNROLL + u
            for s in range(2):
                _row_copy(h_ref, r, xb_ref, dest_ref[2 * (i * tm + r) + s], sem).start()
        return carry

    lax.fori_loop(0, tm // ROW_DMA_UNROLL, issue, 0)
    for s in range(2):
        pltpu.make_async_copy(h_ref, xb_ref.at[pl.ds(0, tm), :], sem).wait()


def _dispatch(dest_flat, h, n_rows_out):
    lp, d = h.shape
    tm = _tile_rows(lp)
    xb0 = jnp.zeros((n_rows_out, d), F32)
    grid_spec = pltpu.PrefetchScalarGridSpec(
        num_scalar_prefetch=1,
        grid=(lp // tm,),
        in_specs=[pl.BlockSpec((tm, d), lambda i, dest: (i, 0)), pl.BlockSpec(memory_space=pl.ANY)],
        out_specs=pl.BlockSpec(memory_space=pl.ANY),
        scratch_shapes=[pltpu.SemaphoreType.DMA(())],
    )
    return pl.pallas_call(
        _dispatch_kernel,
        grid_spec=grid_spec,
        out_shape=jax.ShapeDtypeStruct((n_rows_out, d), F32),
        input_output_aliases={2: 0},
        compiler_params=pltpu.CompilerParams(dimension_semantics=("arbitrary",), has_side_effects=True),
        name="moe_dispatch",
    )(dest_flat, h, xb0)


def _experts_kernel(bexp_ref, nused_ref, xb_ref, wg_ref, wu_ref, wd_ref, yb_ref, wg_s, wu_s, wd_s):
    b = pl.program_id(0)
    prev = bexp_ref[jnp.maximum(b - 1, 0)]

    @pl.when((b == 0) | (bexp_ref[b] != prev))
    def _():
        wg_s[...] = wg_ref[...].astype(BF16)
        wu_s[...] = wu_ref[...].astype(BF16)
        wd_s[...] = wd_ref[...].astype(BF16)

    @pl.when(b < nused_ref[0])
    def _():
        x = xb_ref[...].astype(BF16)
        gt = _mm(x, wg_s[...])
        up = _mm(x, wu_s[...])
        mid = (gt * jax.nn.sigmoid(gt) * up).astype(BF16)
        yb_ref[...] = _mm(mid, wd_s[...])

    @pl.when(b >= nused_ref[0])
    def _():
        yb_ref[...] = jnp.zeros(yb_ref.shape, F32)


def _experts(bexp, nused, xb, wg, wu, wd, layer, n_blocks):
    d = xb.shape[1]
    bm = EXPERT_ROWS
    wspec = lambda a: pl.BlockSpec((None, None) + a.shape[2:], lambda b, bexp, nused: (layer, bexp[b], 0, 0))
    grid_spec = pltpu.PrefetchScalarGridSpec(
        num_scalar_prefetch=2,
        grid=(n_blocks,),
        in_specs=[pl.BlockSpec((bm, d), lambda b, bexp, nused: (b, 0)), wspec(wg), wspec(wu), wspec(wd)],
        out_specs=pl.BlockSpec((bm, d), lambda b, bexp, nused: (b, 0)),
        scratch_shapes=[pltpu.VMEM(wg.shape[2:], BF16), pltpu.VMEM(wu.shape[2:], BF16),
                        pltpu.VMEM(wd.shape[2:], BF16)],
    )
    return pl.pallas_call(
        _experts_kernel,
        grid_spec=grid_spec,
        out_shape=jax.ShapeDtypeStruct((n_blocks * bm, d), F32),
        compiler_params=_params(("arbitrary",), 48),
        name="moe_experts",
    )(bexp, nused, xb, wg, wu, wd)


def _combine_kernel(src_ref, yb_ref, gate_ref, h_ref, g_ref, b_ref, out_ref, ybuf, sem):
    i = pl.program_id(0)
    tm = h_ref.shape[0]

    def issue(r8, carry):
        for u in range(ROW_DMA_UNROLL):
            r = r8 * ROW_DMA_UNROLL + u
            for s in range(2):
                _row_copy(yb_ref, src_ref[2 * (i * tm + r) + s], ybuf.at[s], r, sem).start()
        return carry

    lax.fori_loop(0, tm // ROW_DMA_UNROLL, issue, 0)
    for s in range(2):
        pltpu.make_async_copy(yb_ref.at[pl.ds(0, tm), :], ybuf.at[s], sem).wait()
    gate = gate_ref[...]
    y = gate[:, 0:1] * ybuf[0] + gate[:, 1:2] * ybuf[1]
    out_ref[...] = _layer_norm(DN_ALPHA * h_ref[...] + y, g_ref[...], b_ref[...])


def _combine(dest_flat, yb, gate, h, g, b):
    lp, d = h.shape
    tm = _tile_rows(lp)
    full = lambda a: pl.BlockSpec(a.shape, lambda i, dest: (0,) * a.ndim)
    grid_spec = pltpu.PrefetchScalarGridSpec(
        num_scalar_prefetch=1,
        grid=(lp // tm,),
        in_specs=[pl.BlockSpec(memory_space=pl.ANY), pl.BlockSpec((tm, 2), lambda i, dest: (i, 0)),
                  pl.BlockSpec((tm, d), lambda i, dest: (i, 0)), full(g), full(b)],
        out_specs=pl.BlockSpec((tm, d), lambda i, dest: (i, 0)),
        scratch_shapes=[pltpu.VMEM((2, tm, d), F32), pltpu.SemaphoreType.DMA(())],
    )
    return pl.pallas_call(
        _combine_kernel,
        grid_spec=grid_spec,
        out_shape=jax.ShapeDtypeStruct((lp, d), F32),
        compiler_params=_params(("arbitrary",), 32),
        name="moe_combine",
    )(dest_flat, yb, gate, h, g, b)


def _moe(h, n_real, layer, wg, bg, we, be, w_gate, w_up, w_down, ln_g, ln_b):
    lp, d = h.shape
    bm = EXPERT_ROWS
    wr = jnp.zeros((d, 2 * LANE), F32).at[:, :N_GROUPS].set(wg).at[:, LANE:LANE + N_EXPERTS].set(we)
    br = jnp.zeros((1, 2 * LANE), F32).at[0, :N_GROUPS].set(bg).at[0, LANE:LANE + N_EXPERTS].set(be)
    eid, gate, rank, cnt = _router(h, wr, br, n_real)

    counts = cnt[0, :N_EXPERTS].astype(I32)
    padded = (counts + bm - 1) // bm * bm
    pend = jnp.cumsum(padded)
    pstart = pend - padded
    n_blocks = -(-(2 * n_real + N_EXPERTS * (bm - 1)) // bm)
    cap = n_blocks * bm
    tok = jnp.arange(lp, dtype=I32)[:, None]
    valid = tok < n_real
    row = pstart[eid] + rank
    dest_flat = jnp.where(valid, row, cap + 2 * (tok - n_real) + jnp.arange(2, dtype=I32)[None, :]).reshape(-1)
    src_flat = jnp.where(valid, row, 0).reshape(-1)
    block_start = jnp.arange(n_blocks, dtype=I32) * bm
    bexp = jnp.minimum(jnp.sum((pend[None, :] <= block_start[:, None]).astype(I32), axis=1), N_EXPERTS - 1)
    nused = (pend[-1:] // bm).astype(I32)

    xb = _dispatch(dest_flat, h, cap + 2 * (lp - n_real))
    yb = _experts(bexp, nused, xb, w_gate, w_up, w_down, layer, n_blocks)
    return _combine(src_flat, yb, gate, h, ln_g, ln_b)


def kernel(x, meta_tokens, ab_w_in, ab_conv_w, ab_conv_b, ab_ln_g, ab_ln_b, ab_w_out, c_w_in, c_conv_w, c_conv_b, c_gate_a_w, c_gate_a_b, c_gate_x_w, c_gate_x_b, c_lambda, c_w_out, moe_router_group_w, moe_router_group_b, moe_router_expert_w, moe_router_expert_b, moe_w_gate, moe_w_up, moe_w_down, ln_mix_g, ln_mix_b, ln_ffn_g, ln_ffn_b):
    bsz, seq, d = x.shape
    assert bsz == 1, "kernel is written for batch 1"
    n_real = N_META + seq
    lp = -(-n_real // Q_BLOCK) * Q_BLOCK
    lk = -(-lp // KEY_CHUNK) * KEY_CHUNK
    topk = min(TOPK_MAX, seq // 4)
    row2 = lambda a: a.reshape(1, -1)

    h = jnp.concatenate([meta_tokens.astype(x.dtype), x[0], jnp.zeros((lp - n_real, d), x.dtype)], axis=0)

    half = HEAD_DIM // 2
    inv_freq = ROPE_THETA ** (-2.0 * jnp.arange(half, dtype=F32) / HEAD_DIM)
    ang = jnp.arange(lp, dtype=F32)[:, None] * inv_freq[None, :]
    cos = jnp.tile(jnp.cos(ang), (1, 4))
    sin = jnp.tile(jnp.concatenate([-jnp.sin(ang), jnp.sin(ang)], axis=1), (1, 2))

    for layer in range(DEPTH):
        j = layer // 2
        if layer % 2 == 0:
            w_in = ab_w_in[j]
            wglu = w_in[:, :2 * CONV_CH].astype(BF16)
            wqkv = w_in[:, 2 * CONV_CH:2 * CONV_CH + 4 * ATT_WIDTH].astype(BF16)
            wsm = jnp.zeros((d, LANE), F32).at[:, :IDX_DIM + IDX_HEADS].set(
                w_in[:, 2 * CONV_CH + 4 * ATT_WIDTH:]).astype(BF16)
            a2, q, k, v, qi, ki, wi = _l0_in(h, wglu, wqkv, wsm, cos, sin, ab_conv_w[j], row2(ab_conv_b[j]),
                                             row2(ab_ln_g[j]), row2(ab_ln_b[j]))
            pad = lambda t: jnp.pad(t, ((0, lk - lp), (0, 0)))
            o = _dsa(q, qi, wi, pad(k), pad(v), pad(ki), topk)
            w_out = ab_w_out[j].astype(BF16)
            h = _l0_out(a2, o, w_out[:CONV_CH], w_out[CONV_CH:], h, row2(ln_mix_g[layer]), row2(ln_mix_b[layer]))
        else:
            h = _l1_mixer(h, c_w_in[j].astype(BF16), c_conv_w[j], row2(c_conv_b[j]),
                          c_gate_a_w[j].astype(BF16), row2(c_gate_a_b[j]),
                          c_gate_x_w[j].astype(BF16), row2(c_gate_x_b[j]), row2(c_lambda[j]),
                          c_w_out[j].astype(BF16), row2(ln_mix_g[layer]), row2(ln_mix_b[layer]))
        h = _moe(h, n_real, layer, moe_router_group_w[layer], moe_router_group_b[layer],
                 moe_router_expert_w[layer], moe_router_expert_b[layer], moe_w_gate, moe_w_up, moe_w_down,
                 row2(ln_ffn_g[layer]), row2(ln_ffn_b[layer]))
    return h[N_META:n_real][None]
```

```python
import functools

import jax
import jax.numpy as jnp
import numpy as np
from jax import lax
from jax.experimental import pallas as pl
from jax.experimental.pallas import tpu as pltpu

F32 = jnp.float32
BF16 = jnp.bfloat16
I32 = jnp.int32

N_META = 16
CONV_CH = 512
CONV_K = 31
ATT_HEADS = 8
HEAD_DIM = 64
ATT_WIDTH = ATT_HEADS * HEAD_DIM
IDX_HEADS = 8
IDX_DIM = 64
TOPK_MAX = 256
ROPE_THETA = 10000.0
RNN_WIDTH = 1280
RNN_BLOCKS = 10
RNN_BLOCK_W = RNN_WIDTH // RNN_BLOCKS
RNN_CONV_K = 4
RG_C = 8.0
N_GROUPS = 4
EXPERTS_PER_GROUP = 8
N_EXPERTS = N_GROUPS * EXPERTS_PER_GROUP
D_EXPERT = 512
LN_EPS = 1e-5
DEPTH = 2
DN_ALPHA = (2 * DEPTH) ** 0.25

LANE = 128
VMEM_BYTES = 64 << 20

Q_BLOCK = 128
KEY_CHUNK = 512
LIST_DEPTH = 12
CONV_HALO = 32
RNN_HALO = 8
EXPERT_ROWS = 256
ROW_DMA_UNROLL = 8
FLT_MAX = 3.4028234663852886e38
MIN_NORMAL_KEY = 1 << 23
NEG_MASK = -2e30
M_INIT = -1e30
LOG2_E = 1.4426950408889634

_NT = (((1,), (1,)), ((), ()))


def _tile_rows(n):
    for t in (512, 384, 256, 128):
        if n % t == 0:
            return t
    raise ValueError(n)


def _mm(a, b):
    return jnp.dot(a, b, preferred_element_type=F32)


def _layer_norm(x, g, b):
    mu = jnp.mean(x, axis=-1, keepdims=True)
    xc = x - mu
    var = jnp.mean(xc * xc, axis=-1, keepdims=True)
    return xc * lax.rsqrt(var + LN_EPS) * g + b


def _rope_group(t, cos, sin_signed, first_half):
    partner = jnp.where(first_half, pltpu.roll(t, LANE - 32, 1), pltpu.roll(t, 32, 1))
    return t * cos + partner * sin_signed


def _params(sem, vmem_mb):
    return pltpu.CompilerParams(dimension_semantics=sem, vmem_limit_bytes=vmem_mb << 20)


def _l0_in_kernel(h_ref, wglu_ref, wqkv_ref, wsm_ref, cos_ref, sin_ref, cw_ref, cb_ref, lg_ref, lb_ref,
                  a2_ref, q_ref, k_ref, v_ref, qi_ref, ki_ref, wi_ref, abuf):
    i = pl.program_id(0)
    tm = h_ref.shape[0]
    hb = h_ref[...].astype(BF16)

    glu = _mm(hb, wglu_ref[...])
    a = glu[:, :CONV_CH] * jax.nn.sigmoid(glu[:, CONV_CH:])

    @pl.when(i == 0)
    def _():
        abuf[0:CONV_HALO, :] = jnp.zeros((CONV_HALO, CONV_CH), F32)

    abuf[pl.ds(CONV_HALO, tm), :] = a
    acc = jnp.broadcast_to(cb_ref[...], (tm, CONV_CH))
    for j in range(CONV_K):
        acc = acc + cw_ref[j:j + 1, :] * abuf[pl.ds(CONV_HALO - (CONV_K - 1) + j, tm), :]
    abuf[0:CONV_HALO, :] = abuf[pl.ds(tm, CONV_HALO), :]
    y = _layer_norm(acc, lg_ref[...], lb_ref[...])
    a2_ref[...] = (y * jax.nn.sigmoid(y)).astype(BF16)

    cos = cos_ref[...]
    sin = sin_ref[...]
    lane = lax.broadcasted_iota(I32, (tm, LANE), 1)
    first_half = (lane % HEAD_DIM) < (HEAD_DIM // 2)
    qkv = _mm(hb, wqkv_ref[...])
    for g in range(ATT_WIDTH // LANE):
        sl = slice(g * LANE, (g + 1) * LANE)
        qg = qkv[:, g * LANE:(g + 1) * LANE]
        kg = qkv[:, ATT_WIDTH + g * LANE:ATT_WIDTH + (g + 1) * LANE]
        ig = qkv[:, 3 * ATT_WIDTH + g * LANE:3 * ATT_WIDTH + (g + 1) * LANE]
        q_ref[:, sl] = (_rope_group(qg, cos, sin, first_half) * (LOG2_E * HEAD_DIM ** -0.5)).astype(BF16)
        k_ref[:, sl] = _rope_group(kg, cos, sin, first_half).astype(BF16)
        qi_ref[:, sl] = (_rope_group(ig, cos, sin, first_half) * (IDX_DIM ** -0.5)).astype(BF16)
    v_ref[...] = qkv[:, 2 * ATT_WIDTH:3 * ATT_WIDTH].astype(BF16)

    sm = _mm(hb, wsm_ref[...])
    ki = _rope_group(sm, cos, sin, first_half)
    ki_ref[...] = jnp.where(lane < IDX_DIM, ki, 0.0).astype(BF16)
    wi_ref[...] = sm * (IDX_HEADS ** -0.5)


def _l0_in(h, wglu, wqkv, wsm, cos, sin, cw, cb, lg, lb):
    lp, d = h.shape
    tm = _tile_rows(lp)
    row = lambda w: pl.BlockSpec((tm, w), lambda i: (i, 0))
    full = lambda a: pl.BlockSpec(a.shape, lambda i: (0,) * a.ndim)
    outs = [jax.ShapeDtypeStruct((lp, CONV_CH), BF16)] + [jax.ShapeDtypeStruct((lp, ATT_WIDTH), BF16)] * 4 + [
        jax.ShapeDtypeStruct((lp, LANE), BF16), jax.ShapeDtypeStruct((lp, LANE), F32)]
    return pl.pallas_call(
        _l0_in_kernel,
        grid=(lp // tm,),
        in_specs=[row(d), full(wglu), full(wqkv), full(wsm), row(LANE), row(LANE), full(cw), full(cb), full(lg),
                  full(lb)],
        out_specs=[row(CONV_CH)] + [row(ATT_WIDTH)] * 4 + [row(LANE), row(LANE)],
        out_shape=outs,
        scratch_shapes=[pltpu.VMEM((CONV_HALO + tm, CONV_CH), F32)],
        compiler_params=_params(("arbitrary",), 48),
        name="l0_in",
    )(h, wglu, wqkv, wsm, cos, sin, cw, cb, lg, lb)


def _two_stage_chunks(nch, first, second):
    first(0, 0)

    def body(t, carry):
        c = 2 * t
        first(c + 1, 1)
        second(c, 0, False)
        first(c + 2, 0)
        second(c + 1, 1, False)
        return carry

    n_pairs = (nch - 1) // 2
    lax.fori_loop(0, n_pairs, body, 0)
    c0 = 2 * n_pairs

    @pl.when(nch - c0 == 2)
    def _():
        first(c0 + 1, 1)
        second(c0, 0, False)
        second(c0 + 1, 1, True)

    @pl.when(nch - c0 == 1)
    def _():
        second(c0, 0, True)


def _dsa_kernel(q_ref, qi_ref, wi_ref, k_ref, v_ref, ki_ref, u_ref, o_ref,
                s_ref, qi8_ref, qh_ref, wb_ref, m_ref, acc_ref, pbuf, lohi_ref, cand_ref, thr_ref, nlo_ref,
                *, topk):
    i = pl.program_id(0)
    bq = q_ref.shape[0]
    ck = KEY_CHUNK
    ngrp = ck // LANE
    nch = ((i + 1) * bq + ck - 1) // ck
    lane = lax.broadcasted_iota(I32, (bq, LANE), 1)
    low = lane < HEAD_DIM
    chunk_rows = lambda c: pl.ds(pl.multiple_of(c * ck, ck), ck)

    for g in range(ATT_WIDTH // LANE):
        pair = qi_ref[:, g * LANE:(g + 1) * LANE].astype(F32)
        qi8_ref[pl.ds((2 * g) * bq, bq), :] = jnp.where(low, pair, 0.0).astype(BF16)
        qi8_ref[pl.ds((2 * g + 1) * bq, bq), :] = jnp.where(low, pltpu.roll(pair, HEAD_DIM, 1), 0.0).astype(BF16)
        qp = q_ref[:, g * LANE:(g + 1) * LANE].astype(F32)
        qh_ref[g, 0:bq, :] = jnp.where(low, qp, 0.0).astype(BF16)
        qh_ref[g, bq:2 * bq, :] = jnp.where(low, 0.0, qp).astype(BF16)
    wi = wi_ref[...]
    for h in range(IDX_HEADS):
        wb_ref[h] = jnp.broadcast_to(wi[:, IDX_DIM + h:IDX_DIM + h + 1], (bq, LANE))
    lohi_ref[0] = jnp.full((bq, LANE), jnp.inf, F32)
    lohi_ref[1] = jnp.full((bq, LANE), -jnp.inf, F32)

    def to_key(x):
        bits = lax.bitcast_convert_type(x, I32)
        return bits ^ ((bits >> 31) & jnp.int32(0x7FFFFFFF))

    def from_key(k):
        k = jnp.where(k > 0, jnp.where(k < MIN_NORMAL_KEY, MIN_NORMAL_KEY, k),
                      jnp.where(k >= -MIN_NORMAL_KEY, 0, k))
        return lax.bitcast_convert_type(k ^ ((k >> 31) & jnp.int32(0x7FFFFFFF)), F32)

    def index_logits(c, slot):
        pbuf[slot] = lax.dot_general(qi8_ref[...], ki_ref[chunk_rows(c), :], _NT, preferred_element_type=F32)

    cand_ref[...] = jnp.full(cand_ref.shape, -jnp.inf, F32)
    lane8 = lax.broadcasted_iota(I32, (8, LANE), 1)
    row8 = lax.broadcasted_iota(I32, (8, LANE), 0)

    def index_scores(c, slot, last):
        for slab in range(bq // 8):
            rows = slice(slab * 8, (slab + 1) * 8)
            smin = lohi_ref[0, rows, :]
            smax = lohi_ref[1, rows, :]
            lst = [cand_ref[d, rows, :] for d in range(LIST_DEPTH)]
            for j in range(ngrp):
                cols = slice(j * LANE, (j + 1) * LANE)
                sc = jnp.zeros((8, LANE), F32)
                for h in range(IDX_HEADS):
                    hrows = slice(h * bq + slab * 8, h * bq + (slab + 1) * 8)
                    sc = sc + jnp.maximum(pbuf[slot, hrows, cols], 0.0) * wb_ref[h, rows, :]
                if last:
                    valid = (c * ck + j * LANE + lane8) <= (i * bq + slab * 8 + row8)
                    smin = jnp.minimum(smin, jnp.where(valid, sc, jnp.inf))
                    sc = jnp.where(valid, sc, -jnp.inf)
                else:
                    smin = jnp.minimum(smin, sc)
                smax = jnp.maximum(smax, sc)
                s_ref[c, rows, cols] = sc
                x = sc
                for d in range(LIST_DEPTH):
                    top = jnp.maximum(lst[d], x)
                    x = jnp.minimum(lst[d], x)
                    lst[d] = top
            for d in range(LIST_DEPTH):
                cand_ref[d, rows, :] = lst[d]
            lohi_ref[0, rows, :] = smin
            lohi_ref[1, rows, :] = smax

    _two_stage_chunks(nch, index_logits, index_scores)

    def count_ge(cand):
        cand_f = from_key(cand)

        def body(c, cnt):
            blk = s_ref[c]
            for g in range(ngrp):
                cnt = cnt + jnp.where(blk[:, g * LANE:(g + 1) * LANE] >= cand_f, 1.0, 0.0)
            return cnt
        cnt = lax.fori_loop(0, nch, body, jnp.zeros((bq, LANE), F32))
        return jnp.sum(cnt, axis=1, keepdims=True)

    kf = float(topk)
    zeros = jnp.zeros((bq, LANE), F32)
    n_valid = (i * bq + lax.broadcasted_iota(I32, (bq, LANE), 0) + 1).astype(F32)
    lo0 = to_key(jnp.min(lohi_ref[0], axis=1, keepdims=True) + zeros)
    hi0 = to_key(jnp.max(lohi_ref[1], axis=1, keepdims=True) + zeros) + 1

    def open_rows(lo, hi, n_lo):
        return jnp.where(n_lo > kf, jnp.where(hi > lo + 1, 1.0, 0.0), 0.0)

    def bisect(count_fn, steps):
        def cond(st):
            it, _, _, _, any_open = st
            return jnp.logical_and(it < 34, any_open)

        def body(st):
            it, lo, hi, n_lo, _ = st
            for _ in range(steps):
                is_open = open_rows(lo, hi, n_lo) > 0.0
                mid = (lo >> 1) + (hi >> 1) + (lo & hi & 1)
                cand = jnp.where(is_open, mid, lo)
                n = count_fn(cand) + zeros
                ge = n >= kf
                lo, hi, n_lo = (jnp.where(is_open, jnp.where(ge, cand, lo), lo),
                                jnp.where(is_open, jnp.where(ge, hi, cand), hi),
                                jnp.where(is_open, jnp.where(ge, n, n_lo), n_lo))
            return it + steps, lo, hi, n_lo, jnp.max(open_rows(lo, hi, n_lo)) > 0.0

        st = lax.while_loop(cond, body, (jnp.int32(0), lo0, hi0, n_valid,
                                         jnp.max(open_rows(lo0, hi0, n_valid)) > 0.0))
        return st[1], st[3]

    def count_ge_lists(cand):
        cand_f = from_key(cand)
        cnt = jnp.zeros((bq, LANE), F32)
        for d in range(LIST_DEPTH):
            cnt = cnt + jnp.where(cand_ref[d] >= cand_f, 1.0, 0.0)
        return jnp.sum(cnt, axis=1, keepdims=True)

    lo_l, n_l = bisect(count_ge_lists, 2)
    thr_ref[...] = lo_l
    nlo_ref[...] = n_l
    deepest = jnp.where(cand_ref[LIST_DEPTH - 1] >= from_key(lo_l), 1.0, 0.0)
    lists_short = jnp.max(jnp.where(n_valid > kf, deepest, 0.0)) > 0.0

    @pl.when(lists_short)
    def _():
        lo_a, n_a = bisect(count_ge, 1)
        thr_ref[...] = lo_a
        nlo_ref[...] = n_a

    n_lo = nlo_ref[...]
    take_all = n_valid <= kf
    thr_key = thr_ref[...]
    thr = jnp.where(take_all, -FLT_MAX, from_key(thr_key))

    m_ref[...] = jnp.full(m_ref.shape, M_INIT, F32)
    acc_ref[...] = jnp.zeros(acc_ref.shape, F32)
    ones = jnp.ones((ck, LANE), BF16)
    pair_rows = lambda g: slice(g * 2 * bq, (g + 1) * 2 * bq)

    def masked_scores(c, slot, bias):
        bias2 = jnp.concatenate([jnp.concatenate(bias, axis=1)] * 2, axis=0)
        for g in range(ATT_WIDTH // LANE):
            kc = k_ref[chunk_rows(c), g * LANE:(g + 1) * LANE]
            pbuf[slot, pair_rows(g), :] = lax.dot_general(qh_ref[g], kc, _NT, preferred_element_type=F32) + bias2

    def softmax_pv(c, slot, last):
        del last
        for g in range(ATT_WIDTH // LANE):
            sj = [pbuf[slot, pair_rows(g), j * LANE:(j + 1) * LANE] for j in range(ngrp)]
            mx = sj[0]
            for j in range(1, ngrp):
                mx = jnp.maximum(mx, sj[j])
            m_prev = m_ref[g]
            m_new = jnp.maximum(m_prev, jnp.max(mx, axis=1, keepdims=True))
            alpha = jnp.exp2(m_prev - m_new)
            p2 = jnp.concatenate([jnp.exp2(s - m_new) for s in sj], axis=1).astype(BF16)
            v_aug = jnp.concatenate([v_ref[chunk_rows(c), g * LANE:(g + 1) * LANE], ones], axis=1)
            acc_ref[g] = jnp.concatenate([alpha, alpha], axis=1) * acc_ref[g] + _mm(p2, v_aug)
            m_ref[g] = m_new

    has_ties = jnp.max(n_lo) > kf

    @pl.when(jnp.logical_not(has_ties))
    def _():
        def scores(c, slot):
            masked_scores(c, slot, [jnp.where(s_ref[c, :, j * LANE:(j + 1) * LANE] >= thr, 0.0, NEG_MASK)
                                    for j in range(ngrp)])

        _two_stage_chunks(nch, scores, softmax_pv)

    @pl.when(has_ties)
    def _():
        n_above = count_ge(thr_key + 1)
        need = jnp.where(take_all[:, 0:1], 0.0, kf - n_above)
        thr_b = jnp.broadcast_to(thr[:, 0:1], (bq, ck))
        need_b = jnp.broadcast_to(need, (bq, ck))

        def attend(c, ties_seen):
            sc = s_ref[c]
            eq = sc == thr_b
            prefix = _mm(jnp.where(eq, 1.0, 0.0).astype(BF16), u_ref[...]) + ties_seen
            take_tie = jnp.where(eq, jnp.where(prefix <= need_b, 0.0, NEG_MASK), NEG_MASK)
            bias = jnp.where(sc > thr_b, 0.0, take_tie)
            masked_scores(c, 0, [bias[:, j * LANE:(j + 1) * LANE] for j in range(ngrp)])
            softmax_pv(c, 0, False)
            return prefix[:, ck - 1:ck]

        lax.fori_loop(0, nch, attend, jnp.zeros((bq, 1), F32))

    for g in range(ATT_WIDTH // LANE):
        acc = acc_ref[g]
        o2 = acc[:, :LANE] * (1.0 / acc[:, LANE:])
        o_ref[:, g * LANE:(g + 1) * LANE] = jnp.where(low, o2[:bq], o2[bq:]).astype(BF16)


def _dsa(q, qi, wi, k, v, ki, topk):
    lp = q.shape[0]
    lk = k.shape[0]
    bq, ck = Q_BLOCK, KEY_CHUNK
    assert topk <= ck and lk % ck == 0 and lk >= lp
    u = (np.arange(ck)[:, None] <= np.arange(ck)[None, :]).astype(np.float32)
    u = jnp.asarray(u, BF16)
    row = lambda w: pl.BlockSpec((bq, w), lambda i: (i, 0))
    res = lambda a: pl.BlockSpec(a.shape, lambda i: (0, 0), pipeline_mode=pl.Buffered(1))
    return pl.pallas_call(
        functools.partial(_dsa_kernel, topk=topk),
        grid=(lp // bq,),
        in_specs=[row(ATT_WIDTH), row(ATT_WIDTH), row(LANE), res(k), res(v), res(ki), res(u)],
        out_specs=row(ATT_WIDTH),
        out_shape=jax.ShapeDtypeStruct((lp, ATT_WIDTH), BF16),
        scratch_shapes=[
            pltpu.VMEM((lk // ck, bq, ck), F32),
            pltpu.VMEM((IDX_HEADS * bq, LANE), BF16),
            pltpu.VMEM((ATT_HEADS // 2, 2 * bq, LANE), BF16),
            pltpu.VMEM((IDX_HEADS, bq, LANE), F32),
            pltpu.VMEM((ATT_HEADS // 2, 2 * bq, LANE), F32),
            pltpu.VMEM((ATT_HEADS // 2, 2 * bq, 2 * LANE), F32),
            pltpu.VMEM((2, IDX_HEADS * bq, ck), F32),
            pltpu.VMEM((2, bq, LANE), F32),
            pltpu.VMEM((LIST_DEPTH, bq, LANE), F32),
            pltpu.VMEM((bq, LANE), I32),
            pltpu.VMEM((bq, LANE), F32),
        ],
        compiler_params=_params(("arbitrary",), 60),
        name="dsa",
    )(q, qi, wi, k, v, ki, u)


def _l0_out_kernel(a2_ref, o_ref, wa_ref, wo_ref, h_ref, g_ref, b_ref, out_ref):
    m = _mm(a2_ref[...], wa_ref[...]) + _mm(o_ref[...], wo_ref[...])
    out_ref[...] = _layer_norm(DN_ALPHA * h_ref[...] + m, g_ref[...], b_ref[...])


def _l0_out(a2, o, wa, wo, h, g, b):
    lp, d = h.shape
    tm = _tile_rows(lp)
    row = lambda w: pl.BlockSpec((tm, w), lambda i: (i, 0))
    full = lambda a: pl.BlockSpec(a.shape, lambda i: (0,) * a.ndim)
    return pl.pallas_call(
        _l0_out_kernel,
        grid=(lp // tm,),
        in_specs=[row(CONV_CH), row(ATT_WIDTH), full(wa), full(wo), row(d), full(g), full(b)],
        out_specs=row(d),
        out_shape=jax.ShapeDtypeStruct((lp, d), F32),
        compiler_params=_params(("parallel",), 32),
        name="l0_out",
    )(a2, o, wa, wo, h, g, b)


def _l1_kernel(h_ref, win_ref, cw_ref, cb_ref, wa_ref, ba_ref, wx_ref, bx_ref, lam_ref, wout_ref, g_ref, b_ref,
               out_ref, xbuf, hstate):
    i = pl.program_id(0)
    tm = h_ref.shape[0]
    h = h_ref[...]
    z = _mm(h.astype(BF16), win_ref[...])
    gate = z[:, :RNN_WIDTH]

    @pl.when(i == 0)
    def _():
        xbuf[0:RNN_HALO, :] = jnp.zeros((RNN_HALO, RNN_WIDTH), F32)
        hstate[...] = jnp.zeros(hstate.shape, F32)

    xbuf[pl.ds(RNN_HALO, tm), :] = z[:, RNN_WIDTH:]
    xc = jnp.broadcast_to(cb_ref[...], (tm, RNN_WIDTH))
    for j in range(RNN_CONV_K):
        xc = xc + cw_ref[j:j + 1, :] * xbuf[pl.ds(RNN_HALO - (RNN_CONV_K - 1) + j, tm), :]
    xbuf[0:RNN_HALO, :] = xbuf[pl.ds(tm, RNN_HALO), :]

    xcb = xc.astype(BF16)
    ra, ri = [], []
    for n in range(RNN_BLOCKS):
        blk = xcb[:, n * RNN_BLOCK_W:(n + 1) * RNN_BLOCK_W]
        ra.append(_mm(blk, wa_ref[n]))
        ri.append(_mm(blk, wx_ref[n]))
    r = jax.nn.sigmoid(jnp.concatenate(ra, axis=1) + ba_ref[...])
    ig = jax.nn.sigmoid(jnp.concatenate(ri, axis=1) + bx_ref[...])
    nl = -lam_ref[...]
    softplus = jnp.maximum(nl, 0.0) + jnp.log(1.0 + jnp.exp(-jnp.abs(nl)))
    log_a = -RG_C * r * softplus
    a = jnp.exp(log_a)
    u = jnp.sqrt(1.0 - jnp.exp(2.0 * log_a)) * (ig * xc)

    rows = lax.broadcasted_iota(I32, (tm, RNN_WIDTH), 0)
    d = 1
    while d < tm:
        keep = rows >= d
        a_sh = jnp.where(keep, pltpu.roll(a, d, 0), 1.0)
        u_sh = jnp.where(keep, pltpu.roll(u, d, 0), 0.0)
        u = a * u_sh + u
        a = a * a_sh
        d *= 2
    hs = u + a * hstate[0:1, :]
    hstate[0:1, :] = hs[tm - 1:tm, :]

    y = (jax.nn.gelu(gate) * hs).astype(BF16)
    m = _mm(y, wout_ref[...])
    out_ref[...] = _layer_norm(DN_ALPHA * h + m, g_ref[...], b_ref[...])


def _l1_mixer(h, win, cw, cb, wa, ba, wx, bx, lam, wout, g, b):
    lp, d = h.shape
    tm = _tile_rows(lp)
    row = lambda w: pl.BlockSpec((tm, w), lambda i: (i, 0))
    full = lambda a: pl.BlockSpec(a.shape, lambda i: (0,) * a.ndim)
    args = (h, win, cw, cb, wa, ba, wx, bx, lam, wout, g, b)
    return pl.pallas_call(
        _l1_kernel,
        grid=(lp // tm,),
        in_specs=[row(d)] + [full(a) for a in args[1:]],
        out_specs=row(d),
        out_shape=jax.ShapeDtypeStruct((lp, d), F32),
        scratch_shapes=[pltpu.VMEM((RNN_HALO + tm, RNN_WIDTH), F32), pltpu.VMEM((8, RNN_WIDTH), F32)],
        compiler_params=_params(("arbitrary",), 56),
        name="l1_mixer",
    )(*args)


def _router_kernel(h_ref, wr_ref, br_ref, ltri_ref, eid_ref, gate_ref, rank_ref, cnt_ref, carry_ref, *, n_real):
    i = pl.program_id(0)
    tm = h_ref.shape[0]

    @pl.when(i == 0)
    def _():
        carry_ref[...] = jnp.zeros(carry_ref.shape, F32)

    logits = jnp.dot(h_ref[...], wr_ref[...], precision=lax.Precision.HIGHEST,
                     preferred_element_type=F32) + br_ref[...]
    lane = lax.broadcasted_iota(I32, (tm, LANE), 1).astype(F32)
    ninf = -jnp.inf
    big = float(LANE)

    gl = jnp.where(lane < N_GROUPS, logits[:, :LANE], ninf)
    gmax = jnp.max(gl, axis=1, keepdims=True)
    g_p = 1.0 / jnp.sum(jnp.exp(gl - gmax), axis=1, keepdims=True)
    g_idx = jnp.min(jnp.where(gl == gmax, lane, big), axis=1, keepdims=True)

    first = g_idx * EXPERTS_PER_GROUP
    el = logits[:, LANE:]
    m1 = jnp.where(lane >= first, jnp.where(lane < first + EXPERTS_PER_GROUP, el, ninf), ninf)
    t1 = jnp.max(m1, axis=1, keepdims=True)
    i1 = jnp.min(jnp.where(m1 == t1, lane, big), axis=1, keepdims=True)
    m2 = jnp.where(lane == i1, ninf, m1)
    t2 = jnp.max(m2, axis=1, keepdims=True)
    i2 = jnp.min(jnp.where(m2 == t2, lane, big), axis=1, keepdims=True)
    e2 = jnp.exp(t2 - t1)
    den = 1.0 / (1.0 + e2)

    tok = i * tm + lax.broadcasted_iota(I32, (tm, LANE), 0)
    valid = tok < n_real
    oh0 = jnp.where(valid, jnp.where(lane == i1, 1.0, 0.0), 0.0)
    oh1 = jnp.where(valid, jnp.where(lane == i2, 1.0, 0.0), 0.0)
    ohs = oh0 + oh1
    before = _mm(ltri_ref[...], ohs.astype(BF16)) + carry_ref[...]
    carry_ref[...] = carry_ref[...] + jnp.sum(ohs, axis=0, keepdims=True)
    cnt_ref[...] = carry_ref[...]

    eid_ref[:, 0:1] = i1.astype(I32)
    eid_ref[:, 1:2] = i2.astype(I32)
    gate_ref[:, 0:1] = g_p * den
    gate_ref[:, 1:2] = g_p * e2 * den
    rank_ref[:, 0:1] = jnp.sum(oh0 * before, axis=1, keepdims=True).astype(I32)
    rank_ref[:, 1:2] = jnp.sum(oh1 * before, axis=1, keepdims=True).astype(I32)


def _router(h, wr, br, n_real):
    lp, d = h.shape
    tm = _tile_rows(lp)
    ltri = jnp.asarray((np.arange(tm)[:, None] > np.arange(tm)[None, :]).astype(np.float32), BF16)
    row = lambda w: pl.BlockSpec((tm, w), lambda i: (i, 0))
    full = lambda a: pl.BlockSpec(a.shape, lambda i: (0,) * a.ndim)
    return pl.pallas_call(
        functools.partial(_router_kernel, n_real=n_real),
        grid=(lp // tm,),
        in_specs=[row(d), full(wr), full(br), full(ltri)],
        out_specs=[row(2), row(2), row(2), pl.BlockSpec((1, LANE), lambda i: (0, 0))],
        out_shape=[jax.ShapeDtypeStruct((lp, 2), I32), jax.ShapeDtypeStruct((lp, 2), F32),
                   jax.ShapeDtypeStruct((lp, 2), I32), jax.ShapeDtypeStruct((1, LANE), F32)],
        scratch_shapes=[pltpu.VMEM((1, LANE), F32)],
        compiler_params=_params(("arbitrary",), 32),
        name="moe_router",
    )(h, wr, br, ltri)


def _row_copy(src_ref, src_row, dst_ref, dst_row, sem):
    return pltpu.make_async_copy(src_ref.at[pl.ds(src_row, 1), :], dst_ref.at[pl.ds(dst_row, 1), :], sem)


def _dispatch_kernel(dest_ref, h_ref, xb_in_ref, xb_ref, sem):
    del xb_in_ref
    i = pl.program_id(0)
    tm = h_ref.shape[0]

    def issue(r8, carry):
        for u in range(ROW_DMA_UNROLL):
            r = r8 * ROW_DMA_UNROLL + u
            for s in range(2):
                _row_copy(h_ref, r, xb_ref, dest_ref[2 * (i * tm + r) + s], sem).start()
        return carry

    lax.fori_loop(0, tm // ROW_DMA_UNROLL, issue, 0)
    for s in range(2):
        pltpu.make_async_copy(h_ref, xb_ref.at[pl.ds(0, tm), :], sem).wait()


def _dispatch(dest_flat, h, n_rows_out):
    lp, d = h.shape
    tm = _tile_rows(lp)
    xb0 = jnp.zeros((n_rows_out, d), F32)
    grid_spec = pltpu.PrefetchScalarGridSpec(
        num_scalar_prefetch=1,
        grid=(lp // tm,),
        in_specs=[pl.BlockSpec((tm, d), lambda i, dest: (i, 0)), pl.BlockSpec(memory_space=pl.ANY)],
        out_specs=pl.BlockSpec(memory_space=pl.ANY),
        scratch_shapes=[pltpu.SemaphoreType.DMA(())],
    )
    return pl.pallas_call(
        _dispatch_kernel,
        grid_spec=grid_spec,
        out_shape=jax.ShapeDtypeStruct((n_rows_out, d), F32),
        input_output_aliases={2: 0},
        compiler_params=pltpu.CompilerParams(dimension_semantics=("arbitrary",), has_side_effects=True),
        name="moe_dispatch",
    )(dest_flat, h, xb0)


def _experts_kernel(bexp_ref, nused_ref, xb_ref, wg_ref, wu_ref, wd_ref, yb_ref, wg_s, wu_s, wd_s):
    b = pl.program_id(0)
    prev = bexp_ref[jnp.maximum(b - 1, 0)]

    @pl.when((b == 0) | (bexp_ref[b] != prev))
    def _():
        wg_s[...] = wg_ref[...].astype(BF16)
        wu_s[...] = wu_ref[...].astype(BF16)
        wd_s[...] = wd_ref[...].astype(BF16)

    @pl.when(b < nused_ref[0])
    def _():
        x = xb_ref[...].astype(BF16)
        gt = _mm(x, wg_s[...])
        up = _mm(x, wu_s[...])
        mid = (gt * jax.nn.sigmoid(gt) * up).astype(BF16)
        yb_ref[...] = _mm(mid, wd_s[...])

    @pl.when(b >= nused_ref[0])
    def _():
        yb_ref[...] = jnp.zeros(yb_ref.shape, F32)


def _experts(bexp, nused, xb, wg, wu, wd, layer, n_blocks):
    d = xb.shape[1]
    bm = EXPERT_ROWS
    wspec = lambda a: pl.BlockSpec((None, None) + a.shape[2:], lambda b, bexp, nused: (layer, bexp[b], 0, 0))
    grid_spec = pltpu.PrefetchScalarGridSpec(
        num_scalar_prefetch=2,
        grid=(n_blocks,),
        in_specs=[pl.BlockSpec((bm, d), lambda b, bexp, nused: (b, 0)), wspec(wg), wspec(wu), wspec(wd)],
        out_specs=pl.BlockSpec((bm, d), lambda b, bexp, nused: (b, 0)),
        scratch_shapes=[pltpu.VMEM(wg.shape[2:], BF16), pltpu.VMEM(wu.shape[2:], BF16),
                        pltpu.VMEM(wd.shape[2:], BF16)],
    )
    return pl.pallas_call(
        _experts_kernel,
        grid_spec=grid_spec,
        out_shape=jax.ShapeDtypeStruct((n_blocks * bm, d), F32),
        compiler_params=_params(("arbitrary",), 48),
        name="moe_experts",
    )(bexp, nused, xb, wg, wu, wd)


def _combine_kernel(src_ref, yb_ref, gate_ref, h_ref, g_ref, b_ref, out_ref, ybuf, sem):
    i = pl.program_id(0)
    tm = h_ref.shape[0]

    def issue(r8, carry):
        for u in range(ROW_DMA_UNROLL):
            r = r8 * ROW_DMA_UNROLL + u
            for s in range(2):
                _row_copy(yb_ref, src_ref[2 * (i * tm + r) + s], ybuf.at[s], r, sem).start()
        return carry

    lax.fori_loop(0, tm // ROW_DMA_UNROLL, issue, 0)
    for s in range(2):
        pltpu.make_async_copy(yb_ref.at[pl.ds(0, tm), :], ybuf.at[s], sem).wait()
    gate = gate_ref[...]
    y = gate[:, 0:1] * ybuf[0] + gate[:, 1:2] * ybuf[1]
    out_ref[...] = _layer_norm(DN_ALPHA * h_ref[...] + y, g_ref[...], b_ref[...])


def _combine(dest_flat, yb, gate, h, g, b):
    lp, d = h.shape
    tm = _tile_rows(lp)
    full = lambda a: pl.BlockSpec(a.shape, lambda i, dest: (0,) * a.ndim)
    grid_spec = pltpu.PrefetchScalarGridSpec(
        num_scalar_prefetch=1,
        grid=(lp // tm,),
        in_specs=[pl.BlockSpec(memory_space=pl.ANY), pl.BlockSpec((tm, 2), lambda i, dest: (i, 0)),
                  pl.BlockSpec((tm, d), lambda i, dest: (i, 0)), full(g), full(b)],
        out_specs=pl.BlockSpec((tm, d), lambda i, dest: (i, 0)),
        scratch_shapes=[pltpu.VMEM((2, tm, d), F32), pltpu.SemaphoreType.DMA(())],
    )
    return pl.pallas_call(
        _combine_kernel,
        grid_spec=grid_spec,
        out_shape=jax.ShapeDtypeStruct((lp, d), F32),
        compiler_params=_params(("arbitrary",), 32),
        name="moe_combine",
    )(dest_flat, yb, gate, h, g, b)


def _moe(h, n_real, layer, wg, bg, we, be, w_gate, w_up, w_down, ln_g, ln_b):
    lp, d = h.shape
    bm = EXPERT_ROWS
    wr = jnp.zeros((d, 2 * LANE), F32).at[:, :N_GROUPS].set(wg).at[:, LANE:LANE + N_EXPERTS].set(we)
    br = jnp.zeros((1, 2 * LANE), F32).at[0, :N_GROUPS].set(bg).at[0, LANE:LANE + N_EXPERTS].set(be)
    eid, gate, rank, cnt = _router(h, wr, br, n_real)

    counts = cnt[0, :N_EXPERTS].astype(I32)
    padded = (counts + bm - 1) // bm * bm
    pend = jnp.cumsum(padded)
    pstart = pend - padded
    n_blocks = -(-(2 * n_real + N_EXPERTS * (bm - 1)) // bm)
    cap = n_blocks * bm
    tok = jnp.arange(lp, dtype=I32)[:, None]
    valid = tok < n_real
    row = pstart[eid] + rank
    dest_flat = jnp.where(valid, row, cap + 2 * (tok - n_real) + jnp.arange(2, dtype=I32)[None, :]).reshape(-1)
    src_flat = jnp.where(valid, row, 0).reshape(-1)
    block_start = jnp.arange(n_blocks, dtype=I32) * bm
    bexp = jnp.minimum(jnp.sum((pend[None, :] <= block_start[:, None]).astype(I32), axis=1), N_EXPERTS - 1)
    nused = (pend[-1:] // bm).astype(I32)

    xb = _dispatch(dest_flat, h, cap + 2 * (lp - n_real))
    yb = _experts(bexp, nused, xb, w_gate, w_up, w_down, layer, n_blocks)
    return _combine(src_flat, yb, gate, h, ln_g, ln_b)


def kernel(x, meta_tokens, ab_w_in, ab_conv_w, ab_conv_b, ab_ln_g, ab_ln_b, ab_w_out, c_w_in, c_conv_w, c_conv_b, c_gate_a_w, c_gate_a_b, c_gate_x_w, c_gate_x_b, c_lambda, c_w_out, moe_router_group_w, moe_router_group_b, moe_router_expert_w, moe_router_expert_b, moe_w_gate, moe_w_up, moe_w_down, ln_mix_g, ln_mix_b, ln_ffn_g, ln_ffn_b):
    bsz, seq, d = x.shape
    assert bsz == 1, "kernel is written for batch 1"
    n_real = N_META + seq
    lp = -(-n_real // Q_BLOCK) * Q_BLOCK
    lk = -(-lp // KEY_CHUNK) * KEY_CHUNK
    topk = min(TOPK_MAX, seq // 4)
    row2 = lambda a: a.reshape(1, -1)

    h = jnp.concatenate([meta_tokens.astype(x.dtype), x[0], jnp.zeros((lp - n_real, d), x.dtype)], axis=0)

    half = HEAD_DIM // 2
    inv_freq = ROPE_THETA ** (-2.0 * jnp.arange(half, dtype=F32) / HEAD_DIM)
    ang = jnp.arange(lp, dtype=F32)[:, None] * inv_freq[None, :]
    cos = jnp.tile(jnp.cos(ang), (1, 4))
    sin = jnp.tile(jnp.concatenate([-jnp.sin(ang), jnp.sin(ang)], axis=1), (1, 2))

    for layer in range(DEPTH):
        j = layer // 2
        if layer % 2 == 0:
            w_in = ab_w_in[j]
            wglu = w_in[:, :2 * CONV_CH].astype(BF16)
            wqkv = w_in[:, 2 * CONV_CH:2 * CONV_CH + 4 * ATT_WIDTH].astype(BF16)
            wsm = jnp.zeros((d, LANE), F32).at[:, :IDX_DIM + IDX_HEADS].set(
                w_in[:, 2 * CONV_CH + 4 * ATT_WIDTH:]).astype(BF16)
            a2, q, k, v, qi, ki, wi = _l0_in(h, wglu, wqkv, wsm, cos, sin, ab_conv_w[j], row2(ab_conv_b[j]),
                                             row2(ab_ln_g[j]), row2(ab_ln_b[j]))
            pad = lambda t: jnp.pad(t, ((0, lk - lp), (0, 0)))
            o = _dsa(q, qi, wi, pad(k), pad(v), pad(ki), topk)
            w_out = ab_w_out[j].astype(BF16)
            h = _l0_out(a2, o, w_out[:CONV_CH], w_out[CONV_CH:], h, row2(ln_mix_g[layer]), row2(ln_mix_b[layer]))
        else:
            h = _l1_mixer(h, c_w_in[j].astype(BF16), c_conv_w[j], row2(c_conv_b[j]),
                          c_gate_a_w[j].astype(BF16), row2(c_gate_a_b[j]),
                          c_gate_x_w[j].astype(BF16), row2(c_gate_x_b[j]), row2(c_lambda[j]),
                          c_w_out[j].astype(BF16), row2(ln_mix_g[layer]), row2(ln_mix_b[layer]))
        h = _moe(h, n_real, layer, moe_router_group_w[layer], moe_router_group_b[layer],
                 moe_router_expert_w[layer], moe_router_expert_b[layer], moe_w_gate, moe_w_up, moe_w_down,
                 row2(ln_ffn_g[layer]), row2(ln_ffn_b[layer]))
    return h[N_META:n_real][None]
```

```python
import functools

import jax
import jax.numpy as jnp
import numpy as np
from jax import lax
from jax.experimental import pallas as pl
from jax.experimental.pallas import tpu as pltpu

F32 = jnp.float32
BF16 = jnp.bfloat16
I32 = jnp.int32

N_META = 16
CONV_CH = 512
CONV_K = 31
ATT_HEADS = 8
HEAD_DIM = 64
ATT_WIDTH = ATT_HEADS * HEAD_DIM
IDX_HEADS = 8
IDX_DIM = 64
TOPK_MAX = 256
ROPE_THETA = 10000.0
RNN_WIDTH = 1280
RNN_BLOCKS = 10
RNN_BLOCK_W = RNN_WIDTH // RNN_BLOCKS
RNN_CONV_K = 4
RG_C = 8.0
N_GROUPS = 4
EXPERTS_PER_GROUP = 8
N_EXPERTS = N_GROUPS * EXPERTS_PER_GROUP
D_EXPERT = 512
LN_EPS = 1e-5
DEPTH = 2
DN_ALPHA = (2 * DEPTH) ** 0.25

LANE = 128
VMEM_BYTES = 64 << 20

Q_BLOCK = 128
KEY_CHUNK = 512
LIST_DEPTH = 12
CONV_HALO = 32
RNN_HALO = 8
EXPERT_ROWS = 256
ROW_DMA_UNROLL = 8
FLT_MAX = 3.4028234663852886e38
MIN_NORMAL_KEY = 1 << 23
NEG_MASK = -2e30
M_INIT = -1e30
LOG2_E = 1.4426950408889634

_NT = (((1,), (1,)), ((), ()))


def _tile_rows(n):
    for t in (512, 384, 256, 128):
        if n % t == 0:
            return t
    raise ValueError(n)


def _mm(a, b):
    return jnp.dot(a, b, preferred_element_type=F32)


def _layer_norm(x, g, b):
    mu = jnp.mean(x, axis=-1, keepdims=True)
    xc = x - mu
    var = jnp.mean(xc * xc, axis=-1, keepdims=True)
    return xc * lax.rsqrt(var + LN_EPS) * g + b


def _rope_group(t, cos, sin_signed, first_half):
    partner = jnp.where(first_half, pltpu.roll(t, LANE - 32, 1), pltpu.roll(t, 32, 1))
    return t * cos + partner * sin_signed


def _params(sem, vmem_mb):
    return pltpu.CompilerParams(dimension_semantics=sem, vmem_limit_bytes=vmem_mb << 20)


def _l0_in_kernel(h_ref, wglu_ref, wqkv_ref, wsm_ref, cos_ref, sin_ref, cw_ref, cb_ref, lg_ref, lb_ref,
                  a2_ref, q_ref, k_ref, v_ref, qi_ref, ki_ref, wi_ref, abuf, shift_buf):
    i = pl.program_id(0)
    tm = h_ref.shape[0]
    hb = h_ref[...].astype(BF16)

    glu = _mm(hb, wglu_ref[...])
    a = glu[:, :CONV_CH] * jax.nn.sigmoid(glu[:, CONV_CH:])

    @pl.when(i == 0)
    def _():
        abuf[0:CONV_HALO, :] = jnp.zeros((CONV_HALO, CONV_CH), F32)

    abuf[pl.ds(CONV_HALO, tm), :] = a
    acc = jnp.broadcast_to(cb_ref[...], (tm, CONV_CH))
    first_off = CONV_HALO - (CONV_K - 1)
    for r in range(8):
        taps = [j for j in range(CONV_K) if (first_off + j) % 8 == r]
        n_rows = tm if r == 0 else tm + 8
        part = None
        for j in taps:
            base = (first_off + j) - r
            term = cw_ref[j:j + 1, :] * abuf[pl.ds(base, n_rows), :]
            part = term if part is None else part + term
        if r == 0:
            acc = acc + part
        else:
            shift_buf[...] = part
            acc = acc + shift_buf[pl.ds(r, tm), :]
    abuf[0:CONV_HALO, :] = abuf[pl.ds(tm, CONV_HALO), :]
    y = _layer_norm(acc, lg_ref[...], lb_ref[...])
    a2_ref[...] = (y * jax.nn.sigmoid(y)).astype(BF16)

    cos = cos_ref[...]
    sin = sin_ref[...]
    lane = lax.broadcasted_iota(I32, (tm, LANE), 1)
    first_half = (lane % HEAD_DIM) < (HEAD_DIM // 2)
    qkv = _mm(hb, wqkv_ref[...])
    for g in range(ATT_WIDTH // LANE):
        sl = slice(g * LANE, (g + 1) * LANE)
        qg = qkv[:, g * LANE:(g + 1) * LANE]
        kg = qkv[:, ATT_WIDTH + g * LANE:ATT_WIDTH + (g + 1) * LANE]
        ig = qkv[:, 3 * ATT_WIDTH + g * LANE:3 * ATT_WIDTH + (g + 1) * LANE]
        q_ref[:, sl] = (_rope_group(qg, cos, sin, first_half) * (LOG2_E * HEAD_DIM ** -0.5)).astype(BF16)
        k_ref[:, sl] = _rope_group(kg, cos, sin, first_half).astype(BF16)
        qi_ref[:, sl] = (_rope_group(ig, cos, sin, first_half) * (IDX_DIM ** -0.5)).astype(BF16)
    v_ref[...] = qkv[:, 2 * ATT_WIDTH:3 * ATT_WIDTH].astype(BF16)

    sm = _mm(hb, wsm_ref[...])
    ki = _rope_group(sm, cos, sin, first_half)
    ki_ref[...] = jnp.where(lane < IDX_DIM, ki, 0.0).astype(BF16)
    wi_ref[...] = sm * (IDX_HEADS ** -0.5)


def _l0_in(h, wglu, wqkv, wsm, cos, sin, cw, cb, lg, lb):
    lp, d = h.shape
    tm = _tile_rows(lp)
    row = lambda w: pl.BlockSpec((tm, w), lambda i: (i, 0))
    full = lambda a: pl.BlockSpec(a.shape, lambda i: (0,) * a.ndim)
    outs = [jax.ShapeDtypeStruct((lp, CONV_CH), BF16)] + [jax.ShapeDtypeStruct((lp, ATT_WIDTH), BF16)] * 4 + [
        jax.ShapeDtypeStruct((lp, LANE), BF16), jax.ShapeDtypeStruct((lp, LANE), F32)]
    return pl.pallas_call(
        _l0_in_kernel,
        grid=(lp // tm,),
        in_specs=[row(d), full(wglu), full(wqkv), full(wsm), row(LANE), row(LANE), full(cw), full(cb), full(lg),
                  full(lb)],
        out_specs=[row(CONV_CH)] + [row(ATT_WIDTH)] * 4 + [row(LANE), row(LANE)],
        out_shape=outs,
        scratch_shapes=[pltpu.VMEM((CONV_HALO + tm, CONV_CH), F32), pltpu.VMEM((tm + 8, CONV_CH), F32)],
        compiler_params=_params(("arbitrary",), 48),
        name="l0_in",
    )(h, wglu, wqkv, wsm, cos, sin, cw, cb, lg, lb)


def _two_stage_chunks(nch, first, second):
    first(0, 0)

    def body(t, carry):
        c = 2 * t
        first(c + 1, 1)
        second(c, 0, False)
        first(c + 2, 0)
        second(c + 1, 1, False)
        return carry

    n_pairs = (nch - 1) // 2
    lax.fori_loop(0, n_pairs, body, 0)
    c0 = 2 * n_pairs

    @pl.when(nch - c0 == 2)
    def _():
        first(c0 + 1, 1)
        second(c0, 0, False)
        second(c0 + 1, 1, True)

    @pl.when(nch - c0 == 1)
    def _():
        second(c0, 0, True)


def _dsa_kernel(q_ref, qi_ref, wi_ref, k_ref, v_ref, ki_ref, u_ref, o_ref,
                s_ref, qi8_ref, qh_ref, wb_ref, m_ref, acc_ref, pbuf, lohi_ref, cand_ref, thr_ref, nlo_ref,
                *, topk):
    i = pl.program_id(0)
    bq = q_ref.shape[0]
    ck = KEY_CHUNK
    ngrp = ck // LANE
    nch = ((i + 1) * bq + ck - 1) // ck
    lane = lax.broadcasted_iota(I32, (bq, LANE), 1)
    low = lane < HEAD_DIM
    chunk_rows = lambda c: pl.ds(pl.multiple_of(c * ck, ck), ck)

    for g in range(ATT_WIDTH // LANE):
        pair = qi_ref[:, g * LANE:(g + 1) * LANE].astype(F32)
        qi8_ref[pl.ds((2 * g) * bq, bq), :] = jnp.where(low, pair, 0.0).astype(BF16)
        qi8_ref[pl.ds((2 * g + 1) * bq, bq), :] = jnp.where(low, pltpu.roll(pair, HEAD_DIM, 1), 0.0).astype(BF16)
        qp = q_ref[:, g * LANE:(g + 1) * LANE].astype(F32)
        qh_ref[g, 0:bq, :] = jnp.where(low, qp, 0.0).astype(BF16)
        qh_ref[g, bq:2 * bq, :] = jnp.where(low, 0.0, qp).astype(BF16)
    wi = wi_ref[...]
    for h in range(IDX_HEADS):
        wb_ref[h] = jnp.broadcast_to(wi[:, IDX_DIM + h:IDX_DIM + h + 1], (bq, LANE))
    lohi_ref[0] = jnp.full((bq, LANE), jnp.inf, F32)
    lohi_ref[1] = jnp.full((bq, LANE), -jnp.inf, F32)

    def to_key(x):
        bits = lax.bitcast_convert_type(x, I32)
        return bits ^ ((bits >> 31) & jnp.int32(0x7FFFFFFF))

    def from_key(k):
        k = jnp.where(k > 0, jnp.where(k < MIN_NORMAL_KEY, MIN_NORMAL_KEY, k),
                      jnp.where(k >= -MIN_NORMAL_KEY, 0, k))
        return lax.bitcast_convert_type(k ^ ((k >> 31) & jnp.int32(0x7FFFFFFF)), F32)

    def index_logits(c, slot):
        pbuf[slot] = lax.dot_general(qi8_ref[...], ki_ref[chunk_rows(c), :], _NT, preferred_element_type=F32)

    cand_ref[...] = jnp.full(cand_ref.shape, -jnp.inf, F32)
    lane8 = lax.broadcasted_iota(I32, (8, LANE), 1)
    row8 = lax.broadcasted_iota(I32, (8, LANE), 0)

    def index_scores(c, slot, last):
        for slab in range(bq // 8):
            rows = slice(slab * 8, (slab + 1) * 8)
            smin = lohi_ref[0, rows, :]
            smax = lohi_ref[1, rows, :]
            lst = [cand_ref[d, rows, :] for d in range(LIST_DEPTH)]
            for j in range(ngrp):
                cols = slice(j * LANE, (j + 1) * LANE)
                sc = jnp.zeros((8, LANE), F32)
                for h in range(IDX_HEADS):
                    hrows = slice(h * bq + slab * 8, h * bq + (slab + 1) * 8)
                    sc = sc + jnp.maximum(pbuf[slot, hrows, cols], 0.0) * wb_ref[h, rows, :]
                if last:
                    valid = (c * ck + j * LANE + lane8) <= (i * bq + slab * 8 + row8)
                    smin = jnp.minimum(smin, jnp.where(valid, sc, jnp.inf))
                    sc = jnp.where(valid, sc, -jnp.inf)
                else:
                    smin = jnp.minimum(smin, sc)
                smax = jnp.maximum(smax, sc)
                s_ref[c, rows, cols] = sc
                x = sc
                for d in range(LIST_DEPTH):
                    top = jnp.maximum(lst[d], x)
                    x = jnp.minimum(lst[d], x)
                    lst[d] = top
            for d in range(LIST_DEPTH):
                cand_ref[d, rows, :] = lst[d]
            lohi_ref[0, rows, :] = smin
            lohi_ref[1, rows, :] = smax

    _two_stage_chunks(nch, index_logits, index_scores)

    def count_ge(cand):
        cand_f = from_key(cand)

        def body(c, cnt):
            blk = s_ref[c]
            for g in range(ngrp):
                cnt = cnt + jnp.where(blk[:, g * LANE:(g + 1) * LANE] >= cand_f, 1.0, 0.0)
            return cnt
        cnt = lax.fori_loop(0, nch, body, jnp.zeros((bq, LANE), F32))
        return jnp.sum(cnt, axis=1, keepdims=True)

    kf = float(topk)
    zeros = jnp.zeros((bq, LANE), F32)
    n_valid = (i * bq + lax.broadcasted_iota(I32, (bq, LANE), 0) + 1).astype(F32)
    lo0 = to_key(jnp.min(lohi_ref[0], axis=1, keepdims=True) + zeros)
    hi0 = to_key(jnp.max(lohi_ref[1], axis=1, keepdims=True) + zeros) + 1

    def open_rows(lo, hi, n_lo):
        return jnp.where(n_lo > kf, jnp.where(hi > lo + 1, 1.0, 0.0), 0.0)

    def bisect(count_fn, steps):
        def cond(st):
            it, _, _, _, any_open = st
            return jnp.logical_and(it < 34, any_open)

        def body(st):
            it, lo, hi, n_lo, _ = st
            for _ in range(steps):
                is_open = open_rows(lo, hi, n_lo) > 0.0
                mid = (lo >> 1) + (hi >> 1) + (lo & hi & 1)
                cand = jnp.where(is_open, mid, lo)
                n = count_fn(cand) + zeros
                ge = n >= kf
                lo, hi, n_lo = (jnp.where(is_open, jnp.where(ge, cand, lo), lo),
                                jnp.where(is_open, jnp.where(ge, hi, cand), hi),
                                jnp.where(is_open, jnp.where(ge, n, n_lo), n_lo))
            return it + steps, lo, hi, n_lo, jnp.max(open_rows(lo, hi, n_lo)) > 0.0

        st = lax.while_loop(cond, body, (jnp.int32(0), lo0, hi0, n_valid,
                                         jnp.max(open_rows(lo0, hi0, n_valid)) > 0.0))
        return st[1], st[3]

    def count_ge_lists(cand):
        cand_f = from_key(cand)
        cnt = jnp.zeros((bq, LANE), F32)
        for d in range(LIST_DEPTH):
            cnt = cnt + jnp.where(cand_ref[d] >= cand_f, 1.0, 0.0)
        return jnp.sum(cnt, axis=1, keepdims=True)

    lo_l, n_l = bisect(count_ge_lists, 2)
    thr_ref[...] = lo_l
    nlo_ref[...] = n_l
    deepest = jnp.where(cand_ref[LIST_DEPTH - 1] >= from_key(lo_l), 1.0, 0.0)
    lists_short = jnp.max(jnp.where(n_valid > kf, deepest, 0.0)) > 0.0

    @pl.when(lists_short)
    def _():
        lo_a, n_a = bisect(count_ge, 1)
        thr_ref[...] = lo_a
        nlo_ref[...] = n_a

    n_lo = nlo_ref[...]
    take_all = n_valid <= kf
    thr_key = thr_ref[...]
    thr = jnp.where(take_all, -FLT_MAX, from_key(thr_key))

    m_ref[...] = jnp.full(m_ref.shape, M_INIT, F32)
    acc_ref[...] = jnp.zeros(acc_ref.shape, F32)
    ones = jnp.ones((ck, LANE), BF16)
    pair_rows = lambda g: slice(g * 2 * bq, (g + 1) * 2 * bq)

    def masked_scores(c, slot, bias):
        bias2 = jnp.concatenate([jnp.concatenate(bias, axis=1)] * 2, axis=0)
        for g in range(ATT_WIDTH // LANE):
            kc = k_ref[chunk_rows(c), g * LANE:(g + 1) * LANE]
            pbuf[slot, pair_rows(g), :] = lax.dot_general(qh_ref[g], kc, _NT, preferred_element_type=F32) + bias2

    def softmax_pv(c, slot, last):
        del last
        for g in range(ATT_WIDTH // LANE):
            sj = [pbuf[slot, pair_rows(g), j * LANE:(j + 1) * LANE] for j in range(ngrp)]
            mx = sj[0]
            for j in range(1, ngrp):
                mx = jnp.maximum(mx, sj[j])
            m_prev = m_ref[g]
            m_new = jnp.maximum(m_prev, jnp.max(mx, axis=1, keepdims=True))
            alpha = jnp.exp2(m_prev - m_new)
            p2 = jnp.concatenate([jnp.exp2(s - m_new) for s in sj], axis=1).astype(BF16)
            v_aug = jnp.concatenate([v_ref[chunk_rows(c), g * LANE:(g + 1) * LANE], ones], axis=1)
            acc_ref[g] = jnp.concatenate([alpha, alpha], axis=1) * acc_ref[g] + _mm(p2, v_aug)
            m_ref[g] = m_new

    has_ties = jnp.max(n_lo) > kf

    @pl.when(jnp.logical_not(has_ties))
    def _():
        def scores(c, slot):
            masked_scores(c, slot, [jnp.where(s_ref[c, :, j * LANE:(j + 1) * LANE] >= thr, 0.0, NEG_MASK)
                                    for j in range(ngrp)])

        _two_stage_chunks(nch, scores, softmax_pv)

    @pl.when(has_ties)
    def _():
        n_above = count_ge(thr_key + 1)
        need = jnp.where(take_all[:, 0:1], 0.0, kf - n_above)
        thr_b = jnp.broadcast_to(thr[:, 0:1], (bq, ck))
        need_b = jnp.broadcast_to(need, (bq, ck))

        def attend(c, ties_seen):
            sc = s_ref[c]
            eq = sc == thr_b
            prefix = _mm(jnp.where(eq, 1.0, 0.0).astype(BF16), u_ref[...]) + ties_seen
            take_tie = jnp.where(eq, jnp.where(prefix <= need_b, 0.0, NEG_MASK), NEG_MASK)
            bias = jnp.where(sc > thr_b, 0.0, take_tie)
            masked_scores(c, 0, [bias[:, j * LANE:(j + 1) * LANE] for j in range(ngrp)])
            softmax_pv(c, 0, False)
            return prefix[:, ck - 1:ck]

        lax.fori_loop(0, nch, attend, jnp.zeros((bq, 1), F32))

    for g in range(ATT_WIDTH // LANE):
        acc = acc_ref[g]
        o2 = acc[:, :LANE] * (1.0 / acc[:, LANE:])
        o_ref[:, g * LANE:(g + 1) * LANE] = jnp.where(low, o2[:bq], o2[bq:]).astype(BF16)


def _dsa(q, qi, wi, k, v, ki, topk):
    lp = q.shape[0]
    lk = k.shape[0]
    bq, ck = Q_BLOCK, KEY_CHUNK
    assert topk <= ck and lk % ck == 0 and lk >= lp
    u = (np.arange(ck)[:, None] <= np.arange(ck)[None, :]).astype(np.float32)
    u = jnp.asarray(u, BF16)
    row = lambda w: pl.BlockSpec((bq, w), lambda i: (i, 0))
    res = lambda a: pl.BlockSpec(a.shape, lambda i: (0, 0), pipeline_mode=pl.Buffered(1))
    return pl.pallas_call(
        functools.partial(_dsa_kernel, topk=topk),
        grid=(lp // bq,),
        in_specs=[row(ATT_WIDTH), row(ATT_WIDTH), row(LANE), res(k), res(v), res(ki), res(u)],
        out_specs=row(ATT_WIDTH),
        out_shape=jax.ShapeDtypeStruct((lp, ATT_WIDTH), BF16),
        scratch_shapes=[
            pltpu.VMEM((lk // ck, bq, ck), F32),
            pltpu.VMEM((IDX_HEADS * bq, LANE), BF16),
            pltpu.VMEM((ATT_HEADS // 2, 2 * bq, LANE), BF16),
            pltpu.VMEM((IDX_HEADS, bq, LANE), F32),
            pltpu.VMEM((ATT_HEADS // 2, 2 * bq, LANE), F32),
            pltpu.VMEM((ATT_HEADS // 2, 2 * bq, 2 * LANE), F32),
            pltpu.VMEM((2, IDX_HEADS * bq, ck), F32),
            pltpu.VMEM((2, bq, LANE), F32),
            pltpu.VMEM((LIST_DEPTH, bq, LANE), F32),
            pltpu.VMEM((bq, LANE), I32),
            pltpu.VMEM((bq, LANE), F32),
        ],
        compiler_params=_params(("arbitrary",), 60),
        name="dsa",
    )(q, qi, wi, k, v, ki, u)


def _l0_out_kernel(a2_ref, o_ref, wa_ref, wo_ref, h_ref, g_ref, b_ref, out_ref):
    m = _mm(a2_ref[...], wa_ref[...]) + _mm(o_ref[...], wo_ref[...])
    out_ref[...] = _layer_norm(DN_ALPHA * h_ref[...] + m, g_ref[...], b_ref[...])


def _l0_out(a2, o, wa, wo, h, g, b):
    lp, d = h.shape
    tm = _tile_rows(lp)
    row = lambda w: pl.BlockSpec((tm, w), lambda i: (i, 0))
    full = lambda a: pl.BlockSpec(a.shape, lambda i: (0,) * a.ndim)
    return pl.pallas_call(
        _l0_out_kernel,
        grid=(lp // tm,),
        in_specs=[row(CONV_CH), row(ATT_WIDTH), full(wa), full(wo), row(d), full(g), full(b)],
        out_specs=row(d),
        out_shape=jax.ShapeDtypeStruct((lp, d), F32),
        compiler_params=_params(("parallel",), 32),
        name="l0_out",
    )(a2, o, wa, wo, h, g, b)


def _l1_kernel(h_ref, win_ref, cw_ref, cb_ref, wa_ref, ba_ref, wx_ref, bx_ref, lam_ref, wout_ref, g_ref, b_ref,
               out_ref, xbuf, hstate):
    i = pl.program_id(0)
    tm = h_ref.shape[0]
    h = h_ref[...]
    z = _mm(h.astype(BF16), win_ref[...])
    gate = z[:, :RNN_WIDTH]

    @pl.when(i == 0)
    def _():
        xbuf[0:RNN_HALO, :] = jnp.zeros((RNN_HALO, RNN_WIDTH), F32)
        hstate[...] = jnp.zeros(hstate.shape, F32)

    xbuf[pl.ds(RNN_HALO, tm), :] = z[:, RNN_WIDTH:]
    xc = jnp.broadcast_to(cb_ref[...], (tm, RNN_WIDTH))
    for j in range(RNN_CONV_K):
        xc = xc + cw_ref[j:j + 1, :] * xbuf[pl.ds(RNN_HALO - (RNN_CONV_K - 1) + j, tm), :]
    xbuf[0:RNN_HALO, :] = xbuf[pl.ds(tm, RNN_HALO), :]

    xcb = xc.astype(BF16)
    ra, ri = [], []
    for n in range(RNN_BLOCKS):
        blk = xcb[:, n * RNN_BLOCK_W:(n + 1) * RNN_BLOCK_W]
        ra.append(_mm(blk, wa_ref[n]))
        ri.append(_mm(blk, wx_ref[n]))
    r = jax.nn.sigmoid(jnp.concatenate(ra, axis=1) + ba_ref[...])
    ig = jax.nn.sigmoid(jnp.concatenate(ri, axis=1) + bx_ref[...])
    nl = -lam_ref[...]
    softplus = jnp.maximum(nl, 0.0) + jnp.log(1.0 + jnp.exp(-jnp.abs(nl)))
    log_a = -RG_C * r * softplus
    a = jnp.exp(log_a)
    gap = 1.0 - a * a
    u = jnp.where(gap > 0.0, gap * lax.rsqrt(gap), 0.0) * (ig * xc)

    in_group = lax.broadcasted_iota(I32, (tm, RNN_WIDTH), 0) % 8
    for d in (1, 2, 4):
        keep = in_group >= d
        a_sh = jnp.where(keep, pltpu.roll(a, d, 0), 1.0)
        u_sh = jnp.where(keep, pltpu.roll(u, d, 0), 0.0)
        u = a * u_sh + u
        a = a * a_sh
    h_prev = hstate[0:1, :]
    groups = []
    for g in range(tm // 8):
        h_g = u[g * 8:(g + 1) * 8, :] + a[g * 8:(g + 1) * 8, :] * h_prev
        groups.append(h_g)
        h_prev = h_g[7:8, :]
    hs = jnp.concatenate(groups, axis=0)
    hstate[0:1, :] = h_prev

    y = (jax.nn.gelu(gate) * hs).astype(BF16)
    m = _mm(y, wout_ref[...])
    out_ref[...] = _layer_norm(DN_ALPHA * h + m, g_ref[...], b_ref[...])


def _l1_mixer(h, win, cw, cb, wa, ba, wx, bx, lam, wout, g, b):
    lp, d = h.shape
    tm = _tile_rows(lp)
    row = lambda w: pl.BlockSpec((tm, w), lambda i: (i, 0))
    full = lambda a: pl.BlockSpec(a.shape, lambda i: (0,) * a.ndim)
    args = (h, win, cw, cb, wa, ba, wx, bx, lam, wout, g, b)
    return pl.pallas_call(
        _l1_kernel,
        grid=(lp // tm,),
        in_specs=[row(d)] + [full(a) for a in args[1:]],
        out_specs=row(d),
        out_shape=jax.ShapeDtypeStruct((lp, d), F32),
        scratch_shapes=[pltpu.VMEM((RNN_HALO + tm, RNN_WIDTH), F32), pltpu.VMEM((8, RNN_WIDTH), F32)],
        compiler_params=_params(("arbitrary",), 56),
        name="l1_mixer",
    )(*args)


def _router_kernel(h_ref, wr_ref, br_ref, ltri_ref, eid_ref, gate_ref, rank_ref, cnt_ref, carry_ref, *, n_real):
    i = pl.program_id(0)
    tm = h_ref.shape[0]

    @pl.when(i == 0)
    def _():
        carry_ref[...] = jnp.zeros(carry_ref.shape, F32)

    logits = jnp.dot(h_ref[...], wr_ref[...], precision=lax.Precision.HIGHEST,
                     preferred_element_type=F32) + br_ref[...]
    lane = lax.broadcasted_iota(I32, (tm, LANE), 1).astype(F32)
    ninf = -jnp.inf
    big = float(LANE)

    gl = jnp.where(lane < N_GROUPS, logits[:, :LANE], ninf)
    gmax = jnp.max(gl, axis=1, keepdims=True)
    g_p = 1.0 / jnp.sum(jnp.exp(gl - gmax), axis=1, keepdims=True)
    g_idx = jnp.min(jnp.where(gl == gmax, lane, big), axis=1, keepdims=True)

    first = g_idx * EXPERTS_PER_GROUP
    el = logits[:, LANE:]
    m1 = jnp.where(lane >= first, jnp.where(lane < first + EXPERTS_PER_GROUP, el, ninf), ninf)
    t1 = jnp.max(m1, axis=1, keepdims=True)
    i1 = jnp.min(jnp.where(m1 == t1, lane, big), axis=1, keepdims=True)
    m2 = jnp.where(lane == i1, ninf, m1)
    t2 = jnp.max(m2, axis=1, keepdims=True)
    i2 = jnp.min(jnp.where(m2 == t2, lane, big), axis=1, keepdims=True)
    e2 = jnp.exp(t2 - t1)
    den = 1.0 / (1.0 + e2)

    tok = i * tm + lax.broadcasted_iota(I32, (tm, LANE), 0)
    valid = tok < n_real
    oh0 = jnp.where(valid, jnp.where(lane == i1, 1.0, 0.0), 0.0)
    oh1 = jnp.where(valid, jnp.where(lane == i2, 1.0, 0.0), 0.0)
    ohs = oh0 + oh1
    before = _mm(ltri_ref[...], ohs.astype(BF16)) + carry_ref[...]
    carry_ref[...] = carry_ref[...] + jnp.sum(ohs, axis=0, keepdims=True)
    cnt_ref[...] = carry_ref[...]

    eid_ref[:, 0:1] = i1.astype(I32)
    eid_ref[:, 1:2] = i2.astype(I32)
    gate_ref[:, 0:1] = g_p * den
    gate_ref[:, 1:2] = g_p * e2 * den
    rank_ref[:, 0:1] = jnp.sum(oh0 * before, axis=1, keepdims=True).astype(I32)
    rank_ref[:, 1:2] = jnp.sum(oh1 * before, axis=1, keepdims=True).astype(I32)


def _router(h, wr, br, n_real):
    lp, d = h.shape
    tm = _tile_rows(lp)
    ltri = jnp.asarray((np.arange(tm)[:, None] > np.arange(tm)[None, :]).astype(np.float32), BF16)
    row = lambda w: pl.BlockSpec((tm, w), lambda i: (i, 0))
    full = lambda a: pl.BlockSpec(a.shape, lambda i: (0,) * a.ndim)
    return pl.pallas_call(
        functools.partial(_router_kernel, n_real=n_real),
        grid=(lp // tm,),
        in_specs=[row(d), full(wr), full(br), full(ltri)],
        out_specs=[row(2), row(2), row(2), pl.BlockSpec((1, LANE), lambda i: (0, 0))],
        out_shape=[jax.ShapeDtypeStruct((lp, 2), I32), jax.ShapeDtypeStruct((lp, 2), F32),
                   jax.ShapeDtypeStruct((lp, 2), I32), jax.ShapeDtypeStruct((1, LANE), F32)],
        scratch_shapes=[pltpu.VMEM((1, LANE), F32)],
        compiler_params=_params(("arbitrary",), 32),
        name="moe_router",
    )(h, wr, br, ltri)


def _row_copy(src_ref, src_row, dst_ref, dst_row, sem):
    return pltpu.make_async_copy(src_ref.at[pl.ds(src_row, 1), :], dst_ref.at[pl.ds(dst_row, 1), :], sem)


def _dispatch_kernel(dest_ref, h_ref, xb_in_ref, xb_ref, sem):
    del xb_in_ref
    i = pl.program_id(0)
    tm = h_ref.shape[0]

    def issue(r8, carry):
        for u in range(ROW_DMA_UNROLL):
            r = r8 * ROW_DMA_UNROLL + u
            for s in range(2):
                _row_copy(h_ref, r, xb_ref, dest_ref[2 * (i * tm + r) + s], sem).start()
        return carry

    lax.fori_loop(0, tm // ROW_DMA_UNROLL, issue, 0)
    for s in range(2):
        pltpu.make_async_copy(h_ref, xb_ref.at[pl.ds(0, tm), :], sem).wait()


def _dispatch(dest_flat, h, n_rows_out):
    lp, d = h.shape
    tm = _tile_rows(lp)
    xb0 = jnp.zeros((n_rows_out, d), F32)
    grid_spec = pltpu.PrefetchScalarGridSpec(
        num_scalar_prefetch=1,
        grid=(lp // tm,),
        in_specs=[pl.BlockSpec((tm, d), lambda i, dest: (i, 0)), pl.BlockSpec(memory_space=pl.ANY)],
        out_specs=pl.BlockSpec(memory_space=pl.ANY),
        scratch_shapes=[pltpu.SemaphoreType.DMA(())],
    )
    return pl.pallas_call(
        _dispatch_kernel,
        grid_spec=grid_spec,
        out_shape=jax.ShapeDtypeStruct((n_rows_out, d), F32),
        input_output_aliases={2: 0},
        compiler_params=pltpu.CompilerParams(dimension_semantics=("arbitrary",), has_side_effects=True),
        name="moe_dispatch",
    )(dest_flat, h, xb0)


def _experts_kernel(bexp_ref, nused_ref, xb_ref, wg_ref, wu_ref, wd_ref, yb_ref, wg_s, wu_s, wd_s):
    b = pl.program_id(0)
    prev = bexp_ref[jnp.maximum(b - 1, 0)]

    @pl.when((b == 0) | (bexp_ref[b] != prev))
    def _():
        wg_s[...] = wg_ref[...].astype(BF16)
        wu_s[...] = wu_ref[...].astype(BF16)
        wd_s[...] = wd_ref[...].astype(BF16)

    @pl.when(b < nused_ref[0])
    def _():
        x = xb_ref[...].astype(BF16)
        gt = _mm(x, wg_s[...])
        up = _mm(x, wu_s[...])
        mid = (gt * jax.nn.sigmoid(gt) * up).astype(BF16)
        yb_ref[...] = _mm(mid, wd_s[...])

    @pl.when(b >= nused_ref[0])
    def _():
        yb_ref[...] = jnp.zeros(yb_ref.shape, F32)


def _experts(bexp, nused, xb, wg, wu, wd, layer, n_blocks):
    d = xb.shape[1]
    bm = EXPERT_ROWS
    wspec = lambda a: pl.BlockSpec((None, None) + a.shape[2:], lambda b, bexp, nused: (layer, bexp[b], 0, 0))
    grid_spec = pltpu.PrefetchScalarGridSpec(
        num_scalar_prefetch=2,
        grid=(n_blocks,),
        in_specs=[pl.BlockSpec((bm, d), lambda b, bexp, nused: (b, 0)), wspec(wg), wspec(wu), wspec(wd)],
        out_specs=pl.BlockSpec((bm, d), lambda b, bexp, nused: (b, 0)),
        scratch_shapes=[pltpu.VMEM(wg.shape[2:], BF16), pltpu.VMEM(wu.shape[2:], BF16),
                        pltpu.VMEM(wd.shape[2:], BF16)],
    )
    return pl.pallas_call(
        _experts_kernel,
        grid_spec=grid_spec,
        out_shape=jax.ShapeDtypeStruct((n_blocks * bm, d), F32),
        compiler_params=_params(("arbitrary",), 48),
        name="moe_experts",
    )(bexp, nused, xb, wg, wu, wd)


def _combine_kernel(src_ref, yb_ref, gate_ref, h_ref, g_ref, b_ref, out_ref, ybuf, sem):
    i = pl.program_id(0)
    tm = h_ref.shape[0]

    def issue(r8, carry):
        for u in range(ROW_DMA_UNROLL):
            r = r8 * ROW_DMA_UNROLL + u
            for s in range(2):
                _row_copy(yb_ref, src_ref[2 * (i * tm + r) + s], ybuf.at[s], r, sem).start()
        return carry

    lax.fori_loop(0, tm // ROW_DMA_UNROLL, issue, 0)
    for s in range(2):
        pltpu.make_async_copy(yb_ref.at[pl.ds(0, tm), :], ybuf.at[s], sem).wait()
    gate = gate_ref[...]
    y = gate[:, 0:1] * ybuf[0] + gate[:, 1:2] * ybuf[1]
    out_ref[...] = _layer_norm(DN_ALPHA * h_ref[...] + y, g_ref[...], b_ref[...])


def _combine(dest_flat, yb, gate, h, g, b):
    lp, d = h.shape
    tm = _tile_rows(lp)
    full = lambda a: pl.BlockSpec(a.shape, lambda i, dest: (0,) * a.ndim)
    grid_spec = pltpu.PrefetchScalarGridSpec(
        num_scalar_prefetch=1,
        grid=(lp // tm,),
        in_specs=[pl.BlockSpec(memory_space=pl.ANY), pl.BlockSpec((tm, 2), lambda i, dest: (i, 0)),
                  pl.BlockSpec((tm, d), lambda i, dest: (i, 0)), full(g), full(b)],
        out_specs=pl.BlockSpec((tm, d), lambda i, dest: (i, 0)),
        scratch_shapes=[pltpu.VMEM((2, tm, d), F32), pltpu.SemaphoreType.DMA(())],
    )
    return pl.pallas_call(
        _combine_kernel,
        grid_spec=grid_spec,
        out_shape=jax.ShapeDtypeStruct((lp, d), F32),
        compiler_params=_params(("arbitrary",), 32),
        name="moe_combine",
    )(dest_flat, yb, gate, h, g, b)


def _moe(h, n_real, layer, wg, bg, we, be, w_gate, w_up, w_down, ln_g, ln_b):
    lp, d = h.shape
    bm = EXPERT_ROWS
    wr = jnp.zeros((d, 2 * LANE), F32).at[:, :N_GROUPS].set(wg).at[:, LANE:LANE + N_EXPERTS].set(we)
    br = jnp.zeros((1, 2 * LANE), F32).at[0, :N_GROUPS].set(bg).at[0, LANE:LANE + N_EXPERTS].set(be)
    eid, gate, rank, cnt = _router(h, wr, br, n_real)

    counts = cnt[0, :N_EXPERTS].astype(I32)
    padded = (counts + bm - 1) // bm * bm
    pend = jnp.cumsum(padded)
    pstart = pend - padded
    n_blocks = -(-(2 * n_real + N_EXPERTS * (bm - 1)) // bm)
    cap = n_blocks * bm
    tok = jnp.arange(lp, dtype=I32)[:, None]
    valid = tok < n_real
    row = pstart[eid] + rank
    dest_flat = jnp.where(valid, row, cap + 2 * (tok - n_real) + jnp.arange(2, dtype=I32)[None, :]).reshape(-1)
    src_flat = jnp.where(valid, row, 0).reshape(-1)
    block_start = jnp.arange(n_blocks, dtype=I32) * bm
    bexp = jnp.minimum(jnp.sum((pend[None, :] <= block_start[:, None]).astype(I32), axis=1), N_EXPERTS - 1)
    nused = (pend[-1:] // bm).astype(I32)

    xb = _dispatch(dest_flat, h, cap + 2 * (lp - n_real))
    yb = _experts(bexp, nused, xb, w_gate, w_up, w_down, layer, n_blocks)
    return _combine(src_flat, yb, gate, h, ln_g, ln_b)


def kernel(x, meta_tokens, ab_w_in, ab_conv_w, ab_conv_b, ab_ln_g, ab_ln_b, ab_w_out, c_w_in, c_conv_w, c_conv_b, c_gate_a_w, c_gate_a_b, c_gate_x_w, c_gate_x_b, c_lambda, c_w_out, moe_router_group_w, moe_router_group_b, moe_router_expert_w, moe_router_expert_b, moe_w_gate, moe_w_up, moe_w_down, ln_mix_g, ln_mix_b, ln_ffn_g, ln_ffn_b):
    bsz, seq, d = x.shape
    assert bsz == 1, "kernel is written for batch 1"
    n_real = N_META + seq
    lp = -(-n_real // Q_BLOCK) * Q_BLOCK
    lk = -(-lp // KEY_CHUNK) * KEY_CHUNK
    topk = min(TOPK_MAX, seq // 4)
    row2 = lambda a: a.reshape(1, -1)

    h = jnp.concatenate([meta_tokens.astype(x.dtype), x[0], jnp.zeros((lp - n_real, d), x.dtype)], axis=0)

    half = HEAD_DIM // 2
    inv_freq = (np.float32(ROPE_THETA) ** (np.float32(-2.0) * np.arange(half, dtype=np.float32)
                                           / np.float32(HEAD_DIM))).astype(np.float32)
    ang = (np.arange(lp, dtype=np.float32)[:, None] * inv_freq[None, :]).astype(np.float64)
    cos = jnp.tile(jnp.asarray(np.cos(ang), F32), (1, 4))
    sin_half = jnp.asarray(np.sin(ang), F32)
    sin = jnp.tile(jnp.concatenate([-sin_half, sin_half], axis=1), (1, 2))

    for layer in range(DEPTH):
        j = layer // 2
        if layer % 2 == 0:
            w_in = ab_w_in[j]
            wglu = w_in[:, :2 * CONV_CH].astype(BF16)
            wqkv = w_in[:, 2 * CONV_CH:2 * CONV_CH + 4 * ATT_WIDTH].astype(BF16)
            wsm = jnp.zeros((d, LANE), F32).at[:, :IDX_DIM + IDX_HEADS].set(
                w_in[:, 2 * CONV_CH + 4 * ATT_WIDTH:]).astype(BF16)
            a2, q, k, v, qi, ki, wi = _l0_in(h, wglu, wqkv, wsm, cos, sin, ab_conv_w[j], row2(ab_conv_b[j]),
                                             row2(ab_ln_g[j]), row2(ab_ln_b[j]))
            pad = lambda t: jnp.pad(t, ((0, lk - lp), (0, 0)))
            o = _dsa(q, qi, wi, pad(k), pad(v), pad(ki), topk)
            w_out = ab_w_out[j].astype(BF16)
            h = _l0_out(a2, o, w_out[:CONV_CH], w_out[CONV_CH:], h, row2(ln_mix_g[layer]), row2(ln_mix_b[layer]))
        else:
            h = _l1_mixer(h, c_w_in[j].astype(BF16), c_conv_w[j], row2(c_conv_b[j]),
                          c_gate_a_w[j].astype(BF16), row2(c_gate_a_b[j]),
                          c_gate_x_w[j].astype(BF16), row2(c_gate_x_b[j]), row2(c_lambda[j]),
                          c_w_out[j].astype(BF16), row2(ln_mix_g[layer]), row2(ln_mix_b[layer]))
        h = _moe(h, n_real, layer, moe_router_group_w[layer], moe_router_group_b[layer],
                 moe_router_expert_w[layer], moe_router_expert_b[layer], moe_w_gate, moe_w_up, moe_w_down,
                 row2(ln_ffn_g[layer]), row2(ln_ffn_b[layer]))
    return h[N_META:n_real][None]
```

```python
import functools

import jax
import jax.numpy as jnp
import numpy as np
from jax import lax
from jax.experimental import pallas as pl
from jax.experimental.pallas import tpu as pltpu

F32 = jnp.float32
BF16 = jnp.bfloat16
I32 = jnp.int32

N_META = 16
CONV_CH = 512
CONV_K = 31
ATT_HEADS = 8
HEAD_DIM = 64
ATT_WIDTH = ATT_HEADS * HEAD_DIM
IDX_HEADS = 8
IDX_DIM = 64
TOPK_MAX = 256
ROPE_THETA = 10000.0
RNN_WIDTH = 1280
RNN_BLOCKS = 10
RNN_BLOCK_W = RNN_WIDTH // RNN_BLOCKS
RNN_CONV_K = 4
RG_C = 8.0
N_GROUPS = 4
EXPERTS_PER_GROUP = 8
N_EXPERTS = N_GROUPS * EXPERTS_PER_GROUP
D_EXPERT = 512
LN_EPS = 1e-5
DEPTH = 2
DN_ALPHA = (2 * DEPTH) ** 0.25

LANE = 128
VMEM_BYTES = 64 << 20

Q_BLOCK = 128
KEY_CHUNK = 512
LIST_DEPTH = 12
CONV_HALO = 32
RNN_HALO = 8
EXPERT_ROWS = 256
ROW_DMA_UNROLL = 8
FLT_MAX = 3.4028234663852886e38
MIN_NORMAL_KEY = 1 << 23
NEG_MASK = -2e30
M_INIT = -1e30
LOG2_E = 1.4426950408889634

_NT = (((1,), (1,)), ((), ()))


def _tile_rows(n):
    for t in (512, 384, 256, 128):
        if n % t == 0:
            return t
    raise ValueError(n)


def _mm(a, b):
    return jnp.dot(a, b, preferred_element_type=F32)


def _layer_norm(x, g, b):
    mu = jnp.mean(x, axis=-1, keepdims=True)
    xc = x - mu
    var = jnp.mean(xc * xc, axis=-1, keepdims=True)
    return xc * lax.rsqrt(var + LN_EPS) * g + b


def _rope_group(t, cos, sin_signed, first_half):
    partner = jnp.where(first_half, pltpu.roll(t, LANE - 32, 1), pltpu.roll(t, 32, 1))
    return t * cos + partner * sin_signed


def _params(sem, vmem_mb):
    return pltpu.CompilerParams(dimension_semantics=sem, vmem_limit_bytes=vmem_mb << 20)


def _l0_in_kernel(h_ref, wglu_ref, wqkv_ref, wsm_ref, cos_ref, sin_ref, cw_ref, cb_ref, lg_ref, lb_ref,
                  a2_ref, q_ref, k_ref, v_ref, qi_ref, ki_ref, wi_ref, abuf, shift_buf):
    i = pl.program_id(0)
    tm = h_ref.shape[0]
    hb = h_ref[...].astype(BF16)

    glu = _mm(hb, wglu_ref[...])
    a = glu[:, :CONV_CH] * jax.nn.sigmoid(glu[:, CONV_CH:])

    @pl.when(i == 0)
    def _():
        abuf[0:CONV_HALO, :] = jnp.zeros((CONV_HALO, CONV_CH), F32)

    abuf[pl.ds(CONV_HALO, tm), :] = a
    acc = jnp.broadcast_to(cb_ref[...], (tm, CONV_CH))
    first_off = CONV_HALO - (CONV_K - 1)
    for r in range(8):
        taps = [j for j in range(CONV_K) if (first_off + j) % 8 == r]
        n_rows = tm if r == 0 else tm + 8
        part = None
        for j in taps:
            base = (first_off + j) - r
            term = cw_ref[j:j + 1, :] * abuf[pl.ds(base, n_rows), :]
            part = term if part is None else part + term
        if r == 0:
            acc = acc + part
        else:
            shift_buf[...] = part
            acc = acc + shift_buf[pl.ds(r, tm), :]
    abuf[0:CONV_HALO, :] = abuf[pl.ds(tm, CONV_HALO), :]
    y = _layer_norm(acc, lg_ref[...], lb_ref[...])
    a2_ref[...] = (y * jax.nn.sigmoid(y)).astype(BF16)

    cos = cos_ref[...]
    sin = sin_ref[...]
    lane = lax.broadcasted_iota(I32, (tm, LANE), 1)
    first_half = (lane % HEAD_DIM) < (HEAD_DIM // 2)
    qkv = _mm(hb, wqkv_ref[...])
    for g in range(ATT_WIDTH // LANE):
        sl = slice(g * LANE, (g + 1) * LANE)
        qg = qkv[:, g * LANE:(g + 1) * LANE]
        kg = qkv[:, ATT_WIDTH + g * LANE:ATT_WIDTH + (g + 1) * LANE]
        ig = qkv[:, 3 * ATT_WIDTH + g * LANE:3 * ATT_WIDTH + (g + 1) * LANE]
        q_ref[:, sl] = (_rope_group(qg, cos, sin, first_half) * (LOG2_E * HEAD_DIM ** -0.5)).astype(BF16)
        k_ref[:, sl] = _rope_group(kg, cos, sin, first_half).astype(BF16)
        qi_ref[:, sl] = (_rope_group(ig, cos, sin, first_half) * (IDX_DIM ** -0.5)).astype(BF16)
    v_ref[...] = qkv[:, 2 * ATT_WIDTH:3 * ATT_WIDTH].astype(BF16)

    sm = _mm(hb, wsm_ref[...])
    ki = _rope_group(sm, cos, sin, first_half)
    ki_ref[...] = jnp.where(lane < IDX_DIM, ki, 0.0).astype(BF16)
    wi_ref[...] = sm * (IDX_HEADS ** -0.5)


def _l0_in(h, wglu, wqkv, wsm, cos, sin, cw, cb, lg, lb):
    lp, d = h.shape
    tm = _tile_rows(lp)
    row = lambda w: pl.BlockSpec((tm, w), lambda i: (i, 0))
    full = lambda a: pl.BlockSpec(a.shape, lambda i: (0,) * a.ndim)
    outs = [jax.ShapeDtypeStruct((lp, CONV_CH), BF16)] + [jax.ShapeDtypeStruct((lp, ATT_WIDTH), BF16)] * 4 + [
        jax.ShapeDtypeStruct((lp, LANE), BF16), jax.ShapeDtypeStruct((lp, LANE), F32)]
    return pl.pallas_call(
        _l0_in_kernel,
        grid=(lp // tm,),
        in_specs=[row(d), full(wglu), full(wqkv), full(wsm), row(LANE), row(LANE), full(cw), full(cb), full(lg),
                  full(lb)],
        out_specs=[row(CONV_CH)] + [row(ATT_WIDTH)] * 4 + [row(LANE), row(LANE)],
        out_shape=outs,
        scratch_shapes=[pltpu.VMEM((CONV_HALO + tm, CONV_CH), F32), pltpu.VMEM((tm + 8, CONV_CH), F32)],
        compiler_params=_params(("arbitrary",), 48),
        name="l0_in",
    )(h, wglu, wqkv, wsm, cos, sin, cw, cb, lg, lb)


def _two_stage_chunks(nch, first, second):
    first(0, 0)

    def body(t, carry):
        c = 2 * t
        first(c + 1, 1)
        second(c, 0, False)
        first(c + 2, 0)
        second(c + 1, 1, False)
        return carry

    n_pairs = (nch - 1) // 2
    lax.fori_loop(0, n_pairs, body, 0)
    c0 = 2 * n_pairs

    @pl.when(nch - c0 == 2)
    def _():
        first(c0 + 1, 1)
        second(c0, 0, False)
        second(c0 + 1, 1, True)

    @pl.when(nch - c0 == 1)
    def _():
        second(c0, 0, True)


def _dsa_kernel(q_ref, qi_ref, wi_ref, k_ref, v_ref, ki_ref, u_ref, o_ref,
                s_ref, qi8_ref, qh_ref, wb_ref, m_ref, acc_ref, pbuf, lohi_ref, cand_ref, thr_ref, nlo_ref,
                *, topk):
    i = pl.program_id(0)
    bq = q_ref.shape[0]
    ck = KEY_CHUNK
    ngrp = ck // LANE
    nch = ((i + 1) * bq + ck - 1) // ck
    lane = lax.broadcasted_iota(I32, (bq, LANE), 1)
    low = lane < HEAD_DIM
    chunk_rows = lambda c: pl.ds(pl.multiple_of(c * ck, ck), ck)

    for g in range(ATT_WIDTH // LANE):
        pair = qi_ref[:, g * LANE:(g + 1) * LANE].astype(F32)
        qi8_ref[pl.ds((2 * g) * bq, bq), :] = jnp.where(low, pair, 0.0).astype(BF16)
        qi8_ref[pl.ds((2 * g + 1) * bq, bq), :] = jnp.where(low, pltpu.roll(pair, HEAD_DIM, 1), 0.0).astype(BF16)
        qp = q_ref[:, g * LANE:(g + 1) * LANE].astype(F32)
        qh_ref[g, 0:bq, :] = jnp.where(low, qp, 0.0).astype(BF16)
        qh_ref[g, bq:2 * bq, :] = jnp.where(low, 0.0, qp).astype(BF16)
    wi = wi_ref[...]
    for h in range(IDX_HEADS):
        wb_ref[h] = jnp.broadcast_to(wi[:, IDX_DIM + h:IDX_DIM + h + 1], (bq, LANE))
    lohi_ref[0] = jnp.full((bq, LANE), jnp.inf, F32)
    lohi_ref[1] = jnp.full((bq, LANE), -jnp.inf, F32)

    def to_key(x):
        bits = lax.bitcast_convert_type(x, I32)
        return bits ^ ((bits >> 31) & jnp.int32(0x7FFFFFFF))

    def from_key(k):
        k = jnp.where(k > 0, jnp.where(k < MIN_NORMAL_KEY, MIN_NORMAL_KEY, k),
                      jnp.where(k >= -MIN_NORMAL_KEY, 0, k))
        return lax.bitcast_convert_type(k ^ ((k >> 31) & jnp.int32(0x7FFFFFFF)), F32)

    def index_logits(c, slot):
        pbuf[slot] = lax.dot_general(qi8_ref[...], ki_ref[chunk_rows(c), :], _NT, preferred_element_type=F32)

    cand_ref[...] = jnp.full(cand_ref.shape, -jnp.inf, F32)
    lane8 = lax.broadcasted_iota(I32, (8, LANE), 1)
    row8 = lax.broadcasted_iota(I32, (8, LANE), 0)

    def index_scores(c, slot, last):
        for slab in range(bq // 8):
            rows = slice(slab * 8, (slab + 1) * 8)
            smin = lohi_ref[0, rows, :]
            smax = lohi_ref[1, rows, :]
            lst = [cand_ref[d, rows, :] for d in range(LIST_DEPTH)]
            for j in range(ngrp):
                cols = slice(j * LANE, (j + 1) * LANE)
                sc = jnp.zeros((8, LANE), F32)
                for h in range(IDX_HEADS):
                    hrows = slice(h * bq + slab * 8, h * bq + (slab + 1) * 8)
                    sc = sc + jnp.maximum(pbuf[slot, hrows, cols], 0.0) * wb_ref[h, rows, :]
                if last:
                    valid = (c * ck + j * LANE + lane8) <= (i * bq + slab * 8 + row8)
                    smin = jnp.minimum(smin, jnp.where(valid, sc, jnp.inf))
                    sc = jnp.where(valid, sc, -jnp.inf)
                else:
                    smin = jnp.minimum(smin, sc)
                smax = jnp.maximum(smax, sc)
                s_ref[c, rows, cols] = sc
                x = sc
                for d in range(LIST_DEPTH):
                    top = jnp.maximum(lst[d], x)
                    x = jnp.minimum(lst[d], x)
                    lst[d] = top
            for d in range(LIST_DEPTH):
                cand_ref[d, rows, :] = lst[d]
            lohi_ref[0, rows, :] = smin
            lohi_ref[1, rows, :] = smax

    _two_stage_chunks(nch, index_logits, index_scores)

    def count_ge(cand):
        cand_f = from_key(cand)

        def body(c, cnt):
            blk = s_ref[c]
            for g in range(ngrp):
                cnt = cnt + jnp.where(blk[:, g * LANE:(g + 1) * LANE] >= cand_f, 1.0, 0.0)
            return cnt
        cnt = lax.fori_loop(0, nch, body, jnp.zeros((bq, LANE), F32))
        return jnp.sum(cnt, axis=1, keepdims=True)

    kf = float(topk)
    zeros = jnp.zeros((bq, LANE), F32)
    n_valid = (i * bq + lax.broadcasted_iota(I32, (bq, LANE), 0) + 1).astype(F32)
    lo0 = to_key(jnp.min(lohi_ref[0], axis=1, keepdims=True) + zeros)
    hi0 = to_key(jnp.max(lohi_ref[1], axis=1, keepdims=True) + zeros) + 1

    def open_rows(lo, hi, n_lo):
        return jnp.where(n_lo > kf, jnp.where(hi > lo + 1, 1.0, 0.0), 0.0)

    def bisect(count_fn, steps):
        def cond(st):
            it, _, _, _, any_open = st
            return jnp.logical_and(it < 34, any_open)

        def body(st):
            it, lo, hi, n_lo, _ = st
            for _ in range(steps):
                is_open = open_rows(lo, hi, n_lo) > 0.0
                mid = (lo >> 1) + (hi >> 1) + (lo & hi & 1)
                cand = jnp.where(is_open, mid, lo)
                n = count_fn(cand) + zeros
                ge = n >= kf
                lo, hi, n_lo = (jnp.where(is_open, jnp.where(ge, cand, lo), lo),
                                jnp.where(is_open, jnp.where(ge, hi, cand), hi),
                                jnp.where(is_open, jnp.where(ge, n, n_lo), n_lo))
            return it + steps, lo, hi, n_lo, jnp.max(open_rows(lo, hi, n_lo)) > 0.0

        st = lax.while_loop(cond, body, (jnp.int32(0), lo0, hi0, n_valid,
                                         jnp.max(open_rows(lo0, hi0, n_valid)) > 0.0))
        return st[1], st[3]

    def count_ge_lists(cand):
        cand_f = from_key(cand)
        cnt = jnp.zeros((bq, LANE), F32)
        for d in range(LIST_DEPTH):
            cnt = cnt + jnp.where(cand_ref[d] >= cand_f, 1.0, 0.0)
        return jnp.sum(cnt, axis=1, keepdims=True)

    lo_l, n_l = bisect(count_ge_lists, 2)
    thr_ref[...] = lo_l
    nlo_ref[...] = n_l
    deepest = jnp.where(cand_ref[LIST_DEPTH - 1] >= from_key(lo_l), 1.0, 0.0)
    lists_short = jnp.max(jnp.where(n_valid > kf, deepest, 0.0)) > 0.0

    @pl.when(lists_short)
    def _():
        lo_a, n_a = bisect(count_ge, 1)
        thr_ref[...] = lo_a
        nlo_ref[...] = n_a

    n_lo = nlo_ref[...]
    take_all = n_valid <= kf
    thr_key = thr_ref[...]
    thr = jnp.where(take_all, -FLT_MAX, from_key(thr_key))

    m_ref[...] = jnp.full(m_ref.shape, M_INIT, F32)
    acc_ref[...] = jnp.zeros(acc_ref.shape, F32)
    ones = jnp.ones((ck, LANE), BF16)
    pair_rows = lambda g: slice(g * 2 * bq, (g + 1) * 2 * bq)

    def masked_scores(c, slot, bias):
        bias2 = jnp.concatenate([jnp.concatenate(bias, axis=1)] * 2, axis=0)
        for g in range(ATT_WIDTH // LANE):
            kc = k_ref[chunk_rows(c), g * LANE:(g + 1) * LANE]
            pbuf[slot, pair_rows(g), :] = lax.dot_general(qh_ref[g], kc, _NT, preferred_element_type=F32) + bias2

    def softmax_pv(c, slot, last):
        del last
        for g in range(ATT_WIDTH // LANE):
            sj = [pbuf[slot, pair_rows(g), j * LANE:(j + 1) * LANE] for j in range(ngrp)]
            mx = sj[0]
            for j in range(1, ngrp):
                mx = jnp.maximum(mx, sj[j])
            m_prev = m_ref[g]
            m_new = jnp.maximum(m_prev, jnp.max(mx, axis=1, keepdims=True))
            alpha = jnp.exp2(m_prev - m_new)
            p2 = jnp.concatenate([jnp.exp2(s - m_new) for s in sj], axis=1).astype(BF16)
            v_aug = jnp.concatenate([v_ref[chunk_rows(c), g * LANE:(g + 1) * LANE], ones], axis=1)
            acc_ref[g] = jnp.concatenate([alpha, alpha], axis=1) * acc_ref[g] + _mm(p2, v_aug)
            m_ref[g] = m_new

    has_ties = jnp.max(n_lo) > kf

    @pl.when(jnp.logical_not(has_ties))
    def _():
        def scores(c, slot):
            masked_scores(c, slot, [jnp.where(s_ref[c, :, j * LANE:(j + 1) * LANE] >= thr, 0.0, NEG_MASK)
                                    for j in range(ngrp)])

        _two_stage_chunks(nch, scores, softmax_pv)

    @pl.when(has_ties)
    def _():
        n_above = count_ge(thr_key + 1)
        need = jnp.where(take_all[:, 0:1], 0.0, kf - n_above)
        thr_b = jnp.broadcast_to(thr[:, 0:1], (bq, ck))
        need_b = jnp.broadcast_to(need, (bq, ck))

        def attend(c, ties_seen):
            sc = s_ref[c]
            eq = sc == thr_b
            prefix = _mm(jnp.where(eq, 1.0, 0.0).astype(BF16), u_ref[...]) + ties_seen
            take_tie = jnp.where(eq, jnp.where(prefix <= need_b, 0.0, NEG_MASK), NEG_MASK)
            bias = jnp.where(sc > thr_b, 0.0, take_tie)
            masked_scores(c, 0, [bias[:, j * LANE:(j + 1) * LANE] for j in range(ngrp)])
            softmax_pv(c, 0, False)
            return prefix[:, ck - 1:ck]

        lax.fori_loop(0, nch, attend, jnp.zeros((bq, 1), F32))

    for g in range(ATT_WIDTH // LANE):
        acc = acc_ref[g]
        o2 = acc[:, :LANE] * (1.0 / acc[:, LANE:])
        o_ref[:, g * LANE:(g + 1) * LANE] = jnp.where(low, o2[:bq], o2[bq:]).astype(BF16)


def _dsa(q, qi, wi, k, v, ki, topk):
    lp = q.shape[0]
    lk = k.shape[0]
    bq, ck = Q_BLOCK, KEY_CHUNK
    assert topk <= ck and lk % ck == 0 and lk >= lp
    u = (np.arange(ck)[:, None] <= np.arange(ck)[None, :]).astype(np.float32)
    u = jnp.asarray(u, BF16)
    row = lambda w: pl.BlockSpec((bq, w), lambda i: (i, 0))
    res = lambda a: pl.BlockSpec(a.shape, lambda i: (0, 0), pipeline_mode=pl.Buffered(1))
    return pl.pallas_call(
        functools.partial(_dsa_kernel, topk=topk),
        grid=(lp // bq,),
        in_specs=[row(ATT_WIDTH), row(ATT_WIDTH), row(LANE), res(k), res(v), res(ki), res(u)],
        out_specs=row(ATT_WIDTH),
        out_shape=jax.ShapeDtypeStruct((lp, ATT_WIDTH), BF16),
        scratch_shapes=[
            pltpu.VMEM((lk // ck, bq, ck), F32),
            pltpu.VMEM((IDX_HEADS * bq, LANE), BF16),
            pltpu.VMEM((ATT_HEADS // 2, 2 * bq, LANE), BF16),
            pltpu.VMEM((IDX_HEADS, bq, LANE), F32),
            pltpu.VMEM((ATT_HEADS // 2, 2 * bq, LANE), F32),
            pltpu.VMEM((ATT_HEADS // 2, 2 * bq, 2 * LANE), F32),
            pltpu.VMEM((2, IDX_HEADS * bq, ck), F32),
            pltpu.VMEM((2, bq, LANE), F32),
            pltpu.VMEM((LIST_DEPTH, bq, LANE), F32),
            pltpu.VMEM((bq, LANE), I32),
            pltpu.VMEM((bq, LANE), F32),
        ],
        compiler_params=_params(("arbitrary",), 60),
        name="dsa",
    )(q, qi, wi, k, v, ki, u)


def _l0_out_kernel(a2_ref, o_ref, wa_ref, wo_ref, h_ref, g_ref, b_ref, out_ref):
    m = _mm(a2_ref[...], wa_ref[...]) + _mm(o_ref[...], wo_ref[...])
    out_ref[...] = _layer_norm(DN_ALPHA * h_ref[...] + m, g_ref[...], b_ref[...])


def _l0_out(a2, o, wa, wo, h, g, b):
    lp, d = h.shape
    tm = _tile_rows(lp)
    row = lambda w: pl.BlockSpec((tm, w), lambda i: (i, 0))
    full = lambda a: pl.BlockSpec(a.shape, lambda i: (0,) * a.ndim)
    return pl.pallas_call(
        _l0_out_kernel,
        grid=(lp // tm,),
        in_specs=[row(CONV_CH), row(ATT_WIDTH), full(wa), full(wo), row(d), full(g), full(b)],
        out_specs=row(d),
        out_shape=jax.ShapeDtypeStruct((lp, d), F32),
        compiler_params=_params(("parallel",), 32),
        name="l0_out",
    )(a2, o, wa, wo, h, g, b)


def _l1_kernel(h_ref, win_ref, cw_ref, cb_ref, wa_ref, ba_ref, wx_ref, bx_ref, lam_ref, wout_ref, g_ref, b_ref,
               out_ref, xbuf, hstate):
    i = pl.program_id(0)
    tm = h_ref.shape[0]
    h = h_ref[...]
    z = _mm(h.astype(BF16), win_ref[...])
    gate = z[:, :RNN_WIDTH]

    @pl.when(i == 0)
    def _():
        xbuf[0:RNN_HALO, :] = jnp.zeros((RNN_HALO, RNN_WIDTH), F32)
        hstate[...] = jnp.zeros(hstate.shape, F32)

    xbuf[pl.ds(RNN_HALO, tm), :] = z[:, RNN_WIDTH:]
    xc = jnp.broadcast_to(cb_ref[...], (tm, RNN_WIDTH))
    for j in range(RNN_CONV_K):
        xc = xc + cw_ref[j:j + 1, :] * xbuf[pl.ds(RNN_HALO - (RNN_CONV_K - 1) + j, tm), :]
    xbuf[0:RNN_HALO, :] = xbuf[pl.ds(tm, RNN_HALO), :]

    xcb = xc.astype(BF16)
    ra, ri = [], []
    for n in range(RNN_BLOCKS):
        blk = xcb[:, n * RNN_BLOCK_W:(n + 1) * RNN_BLOCK_W]
        ra.append(_mm(blk, wa_ref[n]))
        ri.append(_mm(blk, wx_ref[n]))
    r = jax.nn.sigmoid(jnp.concatenate(ra, axis=1) + ba_ref[...])
    ig = jax.nn.sigmoid(jnp.concatenate(ri, axis=1) + bx_ref[...])
    nl = -lam_ref[...]
    softplus = jnp.maximum(nl, 0.0) + jnp.log(1.0 + jnp.exp(-jnp.abs(nl)))
    log_a = -RG_C * r * softplus
    a = jnp.exp(log_a)
    gap = 1.0 - a * a
    u = jnp.where(gap > 0.0, gap * lax.rsqrt(gap), 0.0) * (ig * xc)

    in_group = lax.broadcasted_iota(I32, (tm, RNN_WIDTH), 0) % 8
    for d in (1, 2, 4):
        keep = in_group >= d
        a_sh = jnp.where(keep, pltpu.roll(a, d, 0), 1.0)
        u_sh = jnp.where(keep, pltpu.roll(u, d, 0), 0.0)
        u = a * u_sh + u
        a = a * a_sh
    h_prev = hstate[0:1, :]
    groups = []
    for g in range(tm // 8):
        h_g = u[g * 8:(g + 1) * 8, :] + a[g * 8:(g + 1) * 8, :] * h_prev
        groups.append(h_g)
        h_prev = h_g[7:8, :]
    hs = jnp.concatenate(groups, axis=0)
    hstate[0:1, :] = h_prev

    y = (jax.nn.gelu(gate) * hs).astype(BF16)
    m = _mm(y, wout_ref[...])
    out_ref[...] = _layer_norm(DN_ALPHA * h + m, g_ref[...], b_ref[...])


def _l1_mixer(h, win, cw, cb, wa, ba, wx, bx, lam, wout, g, b):
    lp, d = h.shape
    tm = _tile_rows(lp)
    row = lambda w: pl.BlockSpec((tm, w), lambda i: (i, 0))
    full = lambda a: pl.BlockSpec(a.shape, lambda i: (0,) * a.ndim)
    args = (h, win, cw, cb, wa, ba, wx, bx, lam, wout, g, b)
    return pl.pallas_call(
        _l1_kernel,
        grid=(lp // tm,),
        in_specs=[row(d)] + [full(a) for a in args[1:]],
        out_specs=row(d),
        out_shape=jax.ShapeDtypeStruct((lp, d), F32),
        scratch_shapes=[pltpu.VMEM((RNN_HALO + tm, RNN_WIDTH), F32), pltpu.VMEM((8, RNN_WIDTH), F32)],
        compiler_params=_params(("arbitrary",), 56),
        name="l1_mixer",
    )(*args)


def _router_kernel(h_ref, whi_ref, wlo_ref, br_ref, ltri_ref, eid_ref, gate_ref, rank_ref, cnt_ref, carry_ref,
                   *, n_real):
    i = pl.program_id(0)
    tm = h_ref.shape[0]

    @pl.when(i == 0)
    def _():
        carry_ref[...] = jnp.zeros(carry_ref.shape, F32)

    h = h_ref[...]
    h_hi = h.astype(BF16)
    h_lo = (h - h_hi.astype(F32)).astype(BF16)
    logits = _mm(h_hi, whi_ref[...]) + _mm(h_lo, whi_ref[...]) + _mm(h_hi, wlo_ref[...]) + br_ref[...]
    lane = lax.broadcasted_iota(I32, (tm, LANE), 1).astype(F32)
    ninf = -jnp.inf
    big = float(LANE)

    gl = jnp.where(lane < N_GROUPS, logits[:, :LANE], ninf)
    gmax = jnp.max(gl, axis=1, keepdims=True)
    g_p = 1.0 / jnp.sum(jnp.exp(gl - gmax), axis=1, keepdims=True)
    g_idx = jnp.min(jnp.where(gl == gmax, lane, big), axis=1, keepdims=True)

    first = g_idx * EXPERTS_PER_GROUP
    el = logits[:, LANE:]
    m1 = jnp.where(lane >= first, jnp.where(lane < first + EXPERTS_PER_GROUP, el, ninf), ninf)
    t1 = jnp.max(m1, axis=1, keepdims=True)
    i1 = jnp.min(jnp.where(m1 == t1, lane, big), axis=1, keepdims=True)
    m2 = jnp.where(lane == i1, ninf, m1)
    t2 = jnp.max(m2, axis=1, keepdims=True)
    i2 = jnp.min(jnp.where(m2 == t2, lane, big), axis=1, keepdims=True)
    e2 = jnp.exp(t2 - t1)
    den = 1.0 / (1.0 + e2)

    tok = i * tm + lax.broadcasted_iota(I32, (tm, LANE), 0)
    valid = tok < n_real
    oh0 = jnp.where(valid, jnp.where(lane == i1, 1.0, 0.0), 0.0)
    oh1 = jnp.where(valid, jnp.where(lane == i2, 1.0, 0.0), 0.0)
    ohs = oh0 + oh1
    before = _mm(ltri_ref[...], ohs.astype(BF16)) + carry_ref[...]
    carry_ref[...] = carry_ref[...] + jnp.sum(ohs, axis=0, keepdims=True)
    cnt_ref[...] = carry_ref[...]

    eid_ref[:, 0:1] = i1.astype(I32)
    eid_ref[:, 1:2] = i2.astype(I32)
    gate_ref[:, 0:1] = g_p * den
    gate_ref[:, 1:2] = g_p * e2 * den
    rank_ref[:, 0:1] = jnp.sum(oh0 * before, axis=1, keepdims=True).astype(I32)
    rank_ref[:, 1:2] = jnp.sum(oh1 * before, axis=1, keepdims=True).astype(I32)


def _router(h, wr, br, n_real):
    lp, d = h.shape
    tm = _tile_rows(lp)
    w_hi = wr.astype(BF16)
    w_lo = (wr - w_hi.astype(F32)).astype(BF16)
    ltri = jnp.asarray((np.arange(tm)[:, None] > np.arange(tm)[None, :]).astype(np.float32), BF16)
    row = lambda w: pl.BlockSpec((tm, w), lambda i: (i, 0))
    full = lambda a: pl.BlockSpec(a.shape, lambda i: (0,) * a.ndim)
    return pl.pallas_call(
        functools.partial(_router_kernel, n_real=n_real),
        grid=(lp // tm,),
        in_specs=[row(d), full(w_hi), full(w_lo), full(br), full(ltri)],
        out_specs=[row(2), row(2), row(2), pl.BlockSpec((1, LANE), lambda i: (0, 0))],
        out_shape=[jax.ShapeDtypeStruct((lp, 2), I32), jax.ShapeDtypeStruct((lp, 2), F32),
                   jax.ShapeDtypeStruct((lp, 2), I32), jax.ShapeDtypeStruct((1, LANE), F32)],
        scratch_shapes=[pltpu.VMEM((1, LANE), F32)],
        compiler_params=_params(("arbitrary",), 32),
        name="moe_router",
    )(h, w_hi, w_lo, br, ltri)


def _row_copy(src_ref, src_row, dst_ref, dst_row, sem):
    return pltpu.make_async_copy(src_ref.at[pl.ds(src_row, 1), :], dst_ref.at[pl.ds(dst_row, 1), :], sem)


def _dispatch_kernel(dest_ref, h_ref, xb_in_ref, xb_ref, sem):
    del xb_in_ref
    i = pl.program_id(0)
    tm = h_ref.shape[0]

    def issue(r8, carry):
        for u in range(ROW_DMA_UNROLL):
            r = r8 * ROW_DMA_UNROLL + u
            for s in range(2):
                _row_copy(h_ref, r, xb_ref, dest_ref[2 * (i * tm + r) + s], sem).start()
        return carry

    lax.fori_loop(0, tm // ROW_DMA_UNROLL, issue, 0)
    for s in range(2):
        pltpu.make_async_copy(h_ref, xb_ref.at[pl.ds(0, tm), :], sem).wait()


def _dispatch(dest_flat, h, n_rows_out):
    lp, d = h.shape
    tm = _tile_rows(lp)
    xb0 = jnp.zeros((n_rows_out, d), F32)
    grid_spec = pltpu.PrefetchScalarGridSpec(
        num_scalar_prefetch=1,
        grid=(lp // tm,),
        in_specs=[pl.BlockSpec((tm, d), lambda i, dest: (i, 0)), pl.BlockSpec(memory_space=pl.ANY)],
        out_specs=pl.BlockSpec(memory_space=pl.ANY),
        scratch_shapes=[pltpu.SemaphoreType.DMA(())],
    )
    return pl.pallas_call(
        _dispatch_kernel,
        grid_spec=grid_spec,
        out_shape=jax.ShapeDtypeStruct((n_rows_out, d), F32),
        input_output_aliases={2: 0},
        compiler_params=pltpu.CompilerParams(dimension_semantics=("arbitrary",), has_side_effects=True),
        name="moe_dispatch",
    )(dest_flat, h, xb0)


def _experts_kernel(bexp_ref, nused_ref, xb_ref, wg_ref, wu_ref, wd_ref, yb_ref, wg_s, wu_s, wd_s):
    b = pl.program_id(0)
    prev = bexp_ref[jnp.maximum(b - 1, 0)]

    @pl.when((b == 0) | (bexp_ref[b] != prev))
    def _():
        wg_s[...] = wg_ref[...].astype(BF16)
        wu_s[...] = wu_ref[...].astype(BF16)
        wd_s[...] = wd_ref[...].astype(BF16)

    @pl.when(b < nused_ref[0])
    def _():
        x = xb_ref[...].astype(BF16)
        gt = _mm(x, wg_s[...])
        up = _mm(x, wu_s[...])
        mid = (gt * jax.nn.sigmoid(gt) * up).astype(BF16)
        yb_ref[...] = _mm(mid, wd_s[...])

    @pl.when(b >= nused_ref[0])
    def _():
        yb_ref[...] = jnp.zeros(yb_ref.shape, F32)


def _experts(bexp, nused, xb, wg, wu, wd, layer, n_blocks):
    d = xb.shape[1]
    bm = EXPERT_ROWS
    wspec = lambda a: pl.BlockSpec((None, None) + a.shape[2:], lambda b, bexp, *_: (layer, bexp[b], 0, 0))
    grid_spec = pltpu.PrefetchScalarGridSpec(
        num_scalar_prefetch=2,
        grid=(n_blocks,),
        in_specs=[pl.BlockSpec((bm, d), lambda b, *_: (b, 0)), wspec(wg), wspec(wu), wspec(wd)],
        out_specs=pl.BlockSpec((bm, d), lambda b, *_: (b, 0)),
        scratch_shapes=[pltpu.VMEM(wg.shape[2:], BF16), pltpu.VMEM(wu.shape[2:], BF16),
                        pltpu.VMEM(wd.shape[2:], BF16)],
    )
    return pl.pallas_call(
        _experts_kernel,
        grid_spec=grid_spec,
        out_shape=jax.ShapeDtypeStruct((n_blocks * bm, d), F32),
        compiler_params=_params(("arbitrary",), 48),
        name="moe_experts",
    )(bexp, nused, xb, wg, wu, wd)


def _combine_kernel(src_ref, yb_ref, gate_ref, h_ref, g_ref, b_ref, out_ref, ybuf, sem):
    i = pl.program_id(0)
    tm = h_ref.shape[0]

    def issue(r8, carry):
        for u in range(ROW_DMA_UNROLL):
            r = r8 * ROW_DMA_UNROLL + u
            for s in range(2):
                _row_copy(yb_ref, src_ref[2 * (i * tm + r) + s], ybuf.at[s], r, sem).start()
        return carry

    lax.fori_loop(0, tm // ROW_DMA_UNROLL, issue, 0)
    for s in range(2):
        pltpu.make_async_copy(yb_ref.at[pl.ds(0, tm), :], ybuf.at[s], sem).wait()
    gate = gate_ref[...]
    y = gate[:, 0:1] * ybuf[0] + gate[:, 1:2] * ybuf[1]
    out_ref[...] = _layer_norm(DN_ALPHA * h_ref[...] + y, g_ref[...], b_ref[...])


def _combine(dest_flat, yb, gate, h, g, b):
    lp, d = h.shape
    tm = _tile_rows(lp)
    full = lambda a: pl.BlockSpec(a.shape, lambda i, dest: (0,) * a.ndim)
    grid_spec = pltpu.PrefetchScalarGridSpec(
        num_scalar_prefetch=1,
        grid=(lp // tm,),
        in_specs=[pl.BlockSpec(memory_space=pl.ANY), pl.BlockSpec((tm, 2), lambda i, dest: (i, 0)),
                  pl.BlockSpec((tm, d), lambda i, dest: (i, 0)), full(g), full(b)],
        out_specs=pl.BlockSpec((tm, d), lambda i, dest: (i, 0)),
        scratch_shapes=[pltpu.VMEM((2, tm, d), F32), pltpu.SemaphoreType.DMA(())],
    )
    return pl.pallas_call(
        _combine_kernel,
        grid_spec=grid_spec,
        out_shape=jax.ShapeDtypeStruct((lp, d), F32),
        compiler_params=_params(("arbitrary",), 32),
        name="moe_combine",
    )(dest_flat, yb, gate, h, g, b)


def _moe(h, n_real, layer, wg, bg, we, be, w_gate, w_up, w_down, ln_g, ln_b):
    lp, d = h.shape
    bm = EXPERT_ROWS
    wr = jnp.zeros((d, 2 * LANE), F32).at[:, :N_GROUPS].set(wg).at[:, LANE:LANE + N_EXPERTS].set(we)
    br = jnp.zeros((1, 2 * LANE), F32).at[0, :N_GROUPS].set(bg).at[0, LANE:LANE + N_EXPERTS].set(be)
    eid, gate, rank, cnt = _router(h, wr, br, n_real)

    counts = cnt[0, :N_EXPERTS].astype(I32)
    padded = (counts + bm - 1) // bm * bm
    pend = jnp.cumsum(padded)
    pstart = pend - padded
    n_blocks = -(-(2 * n_real + N_EXPERTS * (bm - 1)) // bm)
    cap = n_blocks * bm
    tok = jnp.arange(lp, dtype=I32)[:, None]
    valid = tok < n_real
    experts = jnp.arange(N_EXPERTS, dtype=I32)
    row = jnp.sum(jnp.where(eid[:, :, None] == experts, pstart, 0), axis=-1) + rank
    dest_flat = jnp.where(valid, row, cap + 2 * (tok - n_real) + jnp.arange(2, dtype=I32)[None, :]).reshape(-1)
    src_flat = jnp.where(valid, row, 0).reshape(-1)
    block_start = jnp.arange(n_blocks, dtype=I32) * bm
    bexp = jnp.minimum(jnp.sum((pend[None, :] <= block_start[:, None]).astype(I32), axis=1), N_EXPERTS - 1)
    nused = (pend[-1:] // bm).astype(I32)

    xb = _dispatch(dest_flat, h, cap + 2 * (lp - n_real))
    yb = _experts(bexp, nused, xb, w_gate, w_up, w_down, layer, n_blocks)
    return _combine(src_flat, yb, gate, h, ln_g, ln_b)


def kernel(x, meta_tokens, ab_w_in, ab_conv_w, ab_conv_b, ab_ln_g, ab_ln_b, ab_w_out, c_w_in, c_conv_w, c_conv_b, c_gate_a_w, c_gate_a_b, c_gate_x_w, c_gate_x_b, c_lambda, c_w_out, moe_router_group_w, moe_router_group_b, moe_router_expert_w, moe_router_expert_b, moe_w_gate, moe_w_up, moe_w_down, ln_mix_g, ln_mix_b, ln_ffn_g, ln_ffn_b):
    bsz, seq, d = x.shape
    assert bsz == 1, "kernel is written for batch 1"
    n_real = N_META + seq
    lp = -(-n_real // Q_BLOCK) * Q_BLOCK
    lk = -(-lp // KEY_CHUNK) * KEY_CHUNK
    topk = min(TOPK_MAX, seq // 4)
    row2 = lambda a: a.reshape(1, -1)

    h = jnp.concatenate([meta_tokens.astype(x.dtype), x[0], jnp.zeros((lp - n_real, d), x.dtype)], axis=0)

    half = HEAD_DIM // 2
    inv_freq = (np.float32(ROPE_THETA) ** (np.float32(-2.0) * np.arange(half, dtype=np.float32)
                                           / np.float32(HEAD_DIM))).astype(np.float32)
    ang = (np.arange(lp, dtype=np.float32)[:, None] * inv_freq[None, :]).astype(np.float64)
    cos = jnp.tile(jnp.asarray(np.cos(ang), F32), (1, 4))
    sin_half = jnp.asarray(np.sin(ang), F32)
    sin = jnp.tile(jnp.concatenate([-sin_half, sin_half], axis=1), (1, 2))

    for layer in range(DEPTH):
        j = layer // 2
        if layer % 2 == 0:
            w_in = ab_w_in[j]
            wglu = w_in[:, :2 * CONV_CH].astype(BF16)
            wqkv = w_in[:, 2 * CONV_CH:2 * CONV_CH + 4 * ATT_WIDTH].astype(BF16)
            wsm = jnp.zeros((d, LANE), F32).at[:, :IDX_DIM + IDX_HEADS].set(
                w_in[:, 2 * CONV_CH + 4 * ATT_WIDTH:]).astype(BF16)
            a2, q, k, v, qi, ki, wi = _l0_in(h, wglu, wqkv, wsm, cos, sin, ab_conv_w[j], row2(ab_conv_b[j]),
                                             row2(ab_ln_g[j]), row2(ab_ln_b[j]))
            pad = lambda t: jnp.pad(t, ((0, lk - lp), (0, 0)))
            o = _dsa(q, qi, wi, pad(k), pad(v), pad(ki), topk)
            w_out = ab_w_out[j].astype(BF16)
            h = _l0_out(a2, o, w_out[:CONV_CH], w_out[CONV_CH:], h, row2(ln_mix_g[layer]), row2(ln_mix_b[layer]))
        else:
            h = _l1_mixer(h, c_w_in[j].astype(BF16), c_conv_w[j], row2(c_conv_b[j]),
                          c_gate_a_w[j].astype(BF16), row2(c_gate_a_b[j]),
                          c_gate_x_w[j].astype(BF16), row2(c_gate_x_b[j]), row2(c_lambda[j]),
                          c_w_out[j].astype(BF16), row2(ln_mix_g[layer]), row2(ln_mix_b[layer]))
        h = _moe(h, n_real, layer, moe_router_group_w[layer], moe_router_group_b[layer],
                 moe_router_expert_w[layer], moe_router_expert_b[layer], moe_w_gate, moe_w_up, moe_w_down,
                 row2(ln_ffn_g[layer]), row2(ln_ffn_b[layer]))
    return h[N_META:n_real][None]
```

```python
import functools

import jax
import jax.numpy as jnp
import numpy as np
from jax import lax
from jax.experimental import pallas as pl
from jax.experimental.pallas import tpu as pltpu

F32 = jnp.float32
BF16 = jnp.bfloat16
I32 = jnp.int32

N_META = 16
CONV_CH = 512
CONV_K = 31
ATT_HEADS = 8
HEAD_DIM = 64
ATT_WIDTH = ATT_HEADS * HEAD_DIM
IDX_HEADS = 8
IDX_DIM = 64
TOPK_MAX = 256
ROPE_THETA = 10000.0
RNN_WIDTH = 1280
RNN_BLOCKS = 10
RNN_BLOCK_W = RNN_WIDTH // RNN_BLOCKS
RNN_CONV_K = 4
RG_C = 8.0
N_GROUPS = 4
EXPERTS_PER_GROUP = 8
N_EXPERTS = N_GROUPS * EXPERTS_PER_GROUP
D_EXPERT = 512
LN_EPS = 1e-5
DEPTH = 2
DN_ALPHA = (2 * DEPTH) ** 0.25

LANE = 128
VMEM_BYTES = 64 << 20

Q_BLOCK = 128
KEY_CHUNK = 512
LIST_DEPTH = 12
CONV_HALO = 32
RNN_HALO = 8
EXPERT_ROWS = 256
ROW_DMA_UNROLL = 8
FLT_MAX = 3.4028234663852886e38
MIN_NORMAL_KEY = 1 << 23
NEG_MASK = -2e30
M_INIT = -1e30
LOG2_E = 1.4426950408889634

_NT = (((1,), (1,)), ((), ()))


def _tile_rows(n):
    for t in (512, 384, 256, 128):
        if n % t == 0:
            return t
    raise ValueError(n)


def _mm(a, b):
    return jnp.dot(a, b, preferred_element_type=F32)


def _layer_norm(x, g, b):
    mu = jnp.mean(x, axis=-1, keepdims=True)
    xc = x - mu
    var = jnp.mean(xc * xc, axis=-1, keepdims=True)
    return xc * lax.rsqrt(var + LN_EPS) * g + b


def _pack_bf16_pairs(x):
    n = x.shape[1] // 2
    bits = lax.bitcast_convert_type(x.astype(BF16).astype(F32), I32)
    return bits[:, :n] | lax.shift_right_logical(bits[:, n:], 16)


def _unpack_bf16_pairs(u):
    first = lax.bitcast_convert_type(u & jnp.int32(-65536), F32)
    second = lax.bitcast_convert_type(lax.shift_left(u, 16), F32)
    return jnp.concatenate([first, second], axis=1)


def _rope_group(t, cos, sin_signed, first_half):
    partner = jnp.where(first_half, pltpu.roll(t, LANE - 32, 1), pltpu.roll(t, 32, 1))
    return t * cos + partner * sin_signed


def _params(sem, vmem_mb):
    return pltpu.CompilerParams(dimension_semantics=sem, vmem_limit_bytes=vmem_mb << 20)


def _l0_in_kernel(h_ref, wglu_ref, wqkv_ref, wsm_ref, cos_ref, sin_ref, cw_ref, cb_ref, lg_ref, lb_ref,
                  a2_ref, q_ref, k_ref, v_ref, qi_ref, ki_ref, wi_ref, abuf, shift_buf):
    i = pl.program_id(0)
    tm = h_ref.shape[0]
    hb = h_ref[...].astype(BF16)

    glu = _mm(hb, wglu_ref[...])
    a = glu[:, :CONV_CH] * jax.nn.sigmoid(glu[:, CONV_CH:])

    @pl.when(i == 0)
    def _():
        abuf[0:CONV_HALO, :] = jnp.zeros((CONV_HALO, CONV_CH), F32)

    abuf[pl.ds(CONV_HALO, tm), :] = a
    acc = jnp.broadcast_to(cb_ref[...], (tm, CONV_CH))
    first_off = CONV_HALO - (CONV_K - 1)
    for r in range(8):
        taps = [j for j in range(CONV_K) if (first_off + j) % 8 == r]
        n_rows = tm if r == 0 else tm + 8
        part = None
        for j in taps:
            base = (first_off + j) - r
            term = cw_ref[j:j + 1, :] * abuf[pl.ds(base, n_rows), :]
            part = term if part is None else part + term
        if r == 0:
            acc = acc + part
        else:
            shift_buf[...] = part
            acc = acc + shift_buf[pl.ds(r, tm), :]
    abuf[0:CONV_HALO, :] = abuf[pl.ds(tm, CONV_HALO), :]
    y = _layer_norm(acc, lg_ref[...], lb_ref[...])
    a2_ref[...] = (y * jax.nn.sigmoid(y)).astype(BF16)

    cos = cos_ref[...]
    sin = sin_ref[...]
    lane = lax.broadcasted_iota(I32, (tm, LANE), 1)
    first_half = (lane % HEAD_DIM) < (HEAD_DIM // 2)
    qkv = _mm(hb, wqkv_ref[...])
    for g in range(ATT_WIDTH // LANE):
        sl = slice(g * LANE, (g + 1) * LANE)
        qg = qkv[:, g * LANE:(g + 1) * LANE]
        kg = qkv[:, ATT_WIDTH + g * LANE:ATT_WIDTH + (g + 1) * LANE]
        ig = qkv[:, 3 * ATT_WIDTH + g * LANE:3 * ATT_WIDTH + (g + 1) * LANE]
        q_ref[:, sl] = (_rope_group(qg, cos, sin, first_half) * (LOG2_E * HEAD_DIM ** -0.5)).astype(BF16)
        k_ref[:, sl] = _rope_group(kg, cos, sin, first_half).astype(BF16)
        qi_ref[:, sl] = (_rope_group(ig, cos, sin, first_half) * (IDX_DIM ** -0.5)).astype(BF16)
    v_ref[...] = qkv[:, 2 * ATT_WIDTH:3 * ATT_WIDTH].astype(BF16)

    sm = _mm(hb, wsm_ref[...])
    ki = _rope_group(sm, cos, sin, first_half)
    ki_ref[...] = jnp.where(lane < IDX_DIM, ki, 0.0).astype(BF16)
    wi_ref[...] = sm * (IDX_HEADS ** -0.5)


def _l0_in(h, wglu, wqkv, wsm, cos, sin, cw, cb, lg, lb):
    lp, d = h.shape
    tm = _tile_rows(lp)
    row = lambda w: pl.BlockSpec((tm, w), lambda i: (i, 0))
    full = lambda a: pl.BlockSpec(a.shape, lambda i: (0,) * a.ndim)
    outs = [jax.ShapeDtypeStruct((lp, CONV_CH), BF16)] + [jax.ShapeDtypeStruct((lp, ATT_WIDTH), BF16)] * 4 + [
        jax.ShapeDtypeStruct((lp, LANE), BF16), jax.ShapeDtypeStruct((lp, LANE), F32)]
    return pl.pallas_call(
        _l0_in_kernel,
        grid=(lp // tm,),
        in_specs=[row(d), full(wglu), full(wqkv), full(wsm), row(LANE), row(LANE), full(cw), full(cb), full(lg),
                  full(lb)],
        out_specs=[row(CONV_CH)] + [row(ATT_WIDTH)] * 4 + [row(LANE), row(LANE)],
        out_shape=outs,
        scratch_shapes=[pltpu.VMEM((CONV_HALO + tm, CONV_CH), F32), pltpu.VMEM((tm + 8, CONV_CH), F32)],
        compiler_params=_params(("arbitrary",), 48),
        name="l0_in",
    )(h, wglu, wqkv, wsm, cos, sin, cw, cb, lg, lb)


def _two_stage_chunks(nch, first, second):
    first(0, 0)

    def body(t, carry):
        c = 2 * t
        first(c + 1, 1)
        second(c, 0, False)
        first(c + 2, 0)
        second(c + 1, 1, False)
        return carry

    n_pairs = (nch - 1) // 2
    lax.fori_loop(0, n_pairs, body, 0)
    c0 = 2 * n_pairs

    @pl.when(nch - c0 == 2)
    def _():
        first(c0 + 1, 1)
        second(c0, 0, False)
        second(c0 + 1, 1, True)

    @pl.when(nch - c0 == 1)
    def _():
        second(c0, 0, True)


def _dsa_kernel(q_ref, qi_ref, wi_ref, k_ref, v_ref, ki_ref, u_ref, o_ref,
                s_ref, qi8_ref, qh_ref, wb_ref, m_ref, acc_ref, pbuf, lohi_ref, cand_ref, thr_ref, nlo_ref,
                *, topk):
    i = pl.program_id(0)
    bq = q_ref.shape[0]
    ck = KEY_CHUNK
    ngrp = ck // LANE
    nch = ((i + 1) * bq + ck - 1) // ck
    lane = lax.broadcasted_iota(I32, (bq, LANE), 1)
    low = lane < HEAD_DIM
    chunk_rows = lambda c: pl.ds(pl.multiple_of(c * ck, ck), ck)

    for g in range(ATT_WIDTH // LANE):
        pair = qi_ref[:, g * LANE:(g + 1) * LANE].astype(F32)
        qi8_ref[pl.ds((2 * g) * bq, bq), :] = jnp.where(low, pair, 0.0).astype(BF16)
        qi8_ref[pl.ds((2 * g + 1) * bq, bq), :] = jnp.where(low, pltpu.roll(pair, HEAD_DIM, 1), 0.0).astype(BF16)
        qp = q_ref[:, g * LANE:(g + 1) * LANE].astype(F32)
        qh_ref[g, 0:bq, :] = jnp.where(low, qp, 0.0).astype(BF16)
        qh_ref[g, bq:2 * bq, :] = jnp.where(low, 0.0, qp).astype(BF16)
    wi = wi_ref[...]
    for h in range(IDX_HEADS):
        wb_ref[h] = jnp.broadcast_to(wi[:, IDX_DIM + h:IDX_DIM + h + 1], (bq, LANE))
    lohi_ref[0] = jnp.full((bq, LANE), jnp.inf, F32)
    lohi_ref[1] = jnp.full((bq, LANE), -jnp.inf, F32)

    def to_key(x):
        bits = lax.bitcast_convert_type(x, I32)
        return bits ^ ((bits >> 31) & jnp.int32(0x7FFFFFFF))

    def from_key(k):
        k = jnp.where(k > 0, jnp.where(k < MIN_NORMAL_KEY, MIN_NORMAL_KEY, k),
                      jnp.where(k >= -MIN_NORMAL_KEY, 0, k))
        return lax.bitcast_convert_type(k ^ ((k >> 31) & jnp.int32(0x7FFFFFFF)), F32)

    def index_logits(c, slot):
        pbuf[slot] = lax.dot_general(qi8_ref[...], ki_ref[chunk_rows(c), :], _NT, preferred_element_type=F32)

    cand_ref[...] = jnp.full(cand_ref.shape, -jnp.inf, F32)
    lane8 = lax.broadcasted_iota(I32, (8, LANE), 1)
    row8 = lax.broadcasted_iota(I32, (8, LANE), 0)

    def index_scores(c, slot, last):
        for slab in range(bq // 8):
            rows = slice(slab * 8, (slab + 1) * 8)
            smin = lohi_ref[0, rows, :]
            smax = lohi_ref[1, rows, :]
            lst = [cand_ref[d, rows, :] for d in range(LIST_DEPTH)]
            for j in range(ngrp):
                cols = slice(j * LANE, (j + 1) * LANE)
                sc = jnp.zeros((8, LANE), F32)
                for h in range(IDX_HEADS):
                    hrows = slice(h * bq + slab * 8, h * bq + (slab + 1) * 8)
                    sc = sc + jnp.maximum(pbuf[slot, hrows, cols], 0.0) * wb_ref[h, rows, :]
                if last:
                    valid = (c * ck + j * LANE + lane8) <= (i * bq + slab * 8 + row8)
                    smin = jnp.minimum(smin, jnp.where(valid, sc, jnp.inf))
                    sc = jnp.where(valid, sc, -jnp.inf)
                else:
                    smin = jnp.minimum(smin, sc)
                smax = jnp.maximum(smax, sc)
                s_ref[c, rows, cols] = sc
                x = sc
                for d in range(LIST_DEPTH):
                    top = jnp.maximum(lst[d], x)
                    x = jnp.minimum(lst[d], x)
                    lst[d] = top
            for d in range(LIST_DEPTH):
                cand_ref[d, rows, :] = lst[d]
            lohi_ref[0, rows, :] = smin
            lohi_ref[1, rows, :] = smax

    _two_stage_chunks(nch, index_logits, index_scores)

    def count_ge(cand):
        cand_f = from_key(cand)

        def body(c, cnt):
            blk = s_ref[c]
            for g in range(ngrp):
                cnt = cnt + jnp.where(blk[:, g * LANE:(g + 1) * LANE] >= cand_f, 1.0, 0.0)
            return cnt
        cnt = lax.fori_loop(0, nch, body, jnp.zeros((bq, LANE), F32))
        return jnp.sum(cnt, axis=1, keepdims=True)

    kf = float(topk)
    zeros = jnp.zeros((bq, LANE), F32)
    n_valid = (i * bq + lax.broadcasted_iota(I32, (bq, LANE), 0) + 1).astype(F32)
    lo0 = to_key(jnp.min(lohi_ref[0], axis=1, keepdims=True) + zeros)
    hi0 = to_key(jnp.max(lohi_ref[1], axis=1, keepdims=True) + zeros) + 1

    def open_rows(lo, hi, n_lo):
        return jnp.where(n_lo > kf, jnp.where(hi > lo + 1, 1.0, 0.0), 0.0)

    def bisect(count_fn, steps):
        def cond(st):
            it, _, _, _, any_open = st
            return jnp.logical_and(it < 34, any_open)

        def body(st):
            it, lo, hi, n_lo, _ = st
            for _ in range(steps):
                is_open = open_rows(lo, hi, n_lo) > 0.0
                mid = (lo >> 1) + (hi >> 1) + (lo & hi & 1)
                cand = jnp.where(is_open, mid, lo)
                n = count_fn(cand) + zeros
                ge = n >= kf
                lo, hi, n_lo = (jnp.where(is_open, jnp.where(ge, cand, lo), lo),
                                jnp.where(is_open, jnp.where(ge, hi, cand), hi),
                                jnp.where(is_open, jnp.where(ge, n, n_lo), n_lo))
            return it + steps, lo, hi, n_lo, jnp.max(open_rows(lo, hi, n_lo)) > 0.0

        st = lax.while_loop(cond, body, (jnp.int32(0), lo0, hi0, n_valid,
                                         jnp.max(open_rows(lo0, hi0, n_valid)) > 0.0))
        return st[1], st[3]

    def count_ge_lists(cand):
        cand_f = from_key(cand)
        cnt = jnp.zeros((bq, LANE), F32)
        for d in range(LIST_DEPTH):
            cnt = cnt + jnp.where(cand_ref[d] >= cand_f, 1.0, 0.0)
        return jnp.sum(cnt, axis=1, keepdims=True)

    lo_l, n_l = bisect(count_ge_lists, 2)
    thr_ref[...] = lo_l
    nlo_ref[...] = n_l
    deepest = jnp.where(cand_ref[LIST_DEPTH - 1] >= from_key(lo_l), 1.0, 0.0)
    lists_short = jnp.max(jnp.where(n_valid > kf, deepest, 0.0)) > 0.0

    @pl.when(lists_short)
    def _():
        lo_a, n_a = bisect(count_ge, 1)
        thr_ref[...] = lo_a
        nlo_ref[...] = n_a

    n_lo = nlo_ref[...]
    take_all = n_valid <= kf
    thr_key = thr_ref[...]
    thr = jnp.where(take_all, -FLT_MAX, from_key(thr_key))

    m_ref[...] = jnp.full(m_ref.shape, M_INIT, F32)
    acc_ref[...] = jnp.zeros(acc_ref.shape, F32)
    ones = jnp.ones((ck, LANE), BF16)
    pair_rows = lambda g: slice(g * 2 * bq, (g + 1) * 2 * bq)

    def masked_scores(c, slot, bias):
        bias2 = jnp.concatenate([jnp.concatenate(bias, axis=1)] * 2, axis=0)
        for g in range(ATT_WIDTH // LANE):
            kc = k_ref[chunk_rows(c), g * LANE:(g + 1) * LANE]
            pbuf[slot, pair_rows(g), :] = lax.dot_general(qh_ref[g], kc, _NT, preferred_element_type=F32) + bias2

    def softmax_pv(c, slot, last):
        del last
        for g in range(ATT_WIDTH // LANE):
            sj = [pbuf[slot, pair_rows(g), j * LANE:(j + 1) * LANE] for j in range(ngrp)]
            mx = sj[0]
            for j in range(1, ngrp):
                mx = jnp.maximum(mx, sj[j])
            m_prev = m_ref[g]
            m_new = jnp.maximum(m_prev, jnp.max(mx, axis=1, keepdims=True))
            alpha = jnp.exp2(m_prev - m_new)
            p2 = jnp.concatenate([jnp.exp2(s - m_new) for s in sj], axis=1).astype(BF16)
            v_aug = jnp.concatenate([v_ref[chunk_rows(c), g * LANE:(g + 1) * LANE], ones], axis=1)
            acc_ref[g] = jnp.concatenate([alpha, alpha], axis=1) * acc_ref[g] + _mm(p2, v_aug)
            m_ref[g] = m_new

    has_ties = jnp.max(n_lo) > kf

    @pl.when(jnp.logical_not(has_ties))
    def _():
        def scores(c, slot):
            masked_scores(c, slot, [jnp.where(s_ref[c, :, j * LANE:(j + 1) * LANE] >= thr, 0.0, NEG_MASK)
                                    for j in range(ngrp)])

        _two_stage_chunks(nch, scores, softmax_pv)

    @pl.when(has_ties)
    def _():
        n_above = count_ge(thr_key + 1)
        need = jnp.where(take_all[:, 0:1], 0.0, kf - n_above)
        thr_b = jnp.broadcast_to(thr[:, 0:1], (bq, ck))
        need_b = jnp.broadcast_to(need, (bq, ck))

        def attend(c, ties_seen):
            sc = s_ref[c]
            eq = sc == thr_b
            prefix = _mm(jnp.where(eq, 1.0, 0.0).astype(BF16), u_ref[...]) + ties_seen
            take_tie = jnp.where(eq, jnp.where(prefix <= need_b, 0.0, NEG_MASK), NEG_MASK)
            bias = jnp.where(sc > thr_b, 0.0, take_tie)
            masked_scores(c, 0, [bias[:, j * LANE:(j + 1) * LANE] for j in range(ngrp)])
            softmax_pv(c, 0, False)
            return prefix[:, ck - 1:ck]

        lax.fori_loop(0, nch, attend, jnp.zeros((bq, 1), F32))

    for g in range(ATT_WIDTH // LANE):
        acc = acc_ref[g]
        o2 = acc[:, :LANE] * (1.0 / acc[:, LANE:])
        o_ref[:, g * LANE:(g + 1) * LANE] = jnp.where(low, o2[:bq], o2[bq:]).astype(BF16)


def _dsa(q, qi, wi, k, v, ki, topk):
    lp = q.shape[0]
    lk = k.shape[0]
    bq, ck = Q_BLOCK, KEY_CHUNK
    assert topk <= ck and lk % ck == 0 and lk >= lp
    u = (np.arange(ck)[:, None] <= np.arange(ck)[None, :]).astype(np.float32)
    u = jnp.asarray(u, BF16)
    row = lambda w: pl.BlockSpec((bq, w), lambda i: (i, 0))
    res = lambda a: pl.BlockSpec(a.shape, lambda i: (0, 0), pipeline_mode=pl.Buffered(1))
    return pl.pallas_call(
        functools.partial(_dsa_kernel, topk=topk),
        grid=(lp // bq,),
        in_specs=[row(ATT_WIDTH), row(ATT_WIDTH), row(LANE), res(k), res(v), res(ki), res(u)],
        out_specs=row(ATT_WIDTH),
        out_shape=jax.ShapeDtypeStruct((lp, ATT_WIDTH), BF16),
        scratch_shapes=[
            pltpu.VMEM((lk // ck, bq, ck), F32),
            pltpu.VMEM((IDX_HEADS * bq, LANE), BF16),
            pltpu.VMEM((ATT_HEADS // 2, 2 * bq, LANE), BF16),
            pltpu.VMEM((IDX_HEADS, bq, LANE), F32),
            pltpu.VMEM((ATT_HEADS // 2, 2 * bq, LANE), F32),
            pltpu.VMEM((ATT_HEADS // 2, 2 * bq, 2 * LANE), F32),
            pltpu.VMEM((2, IDX_HEADS * bq, ck), F32),
            pltpu.VMEM((2, bq, LANE), F32),
            pltpu.VMEM((LIST_DEPTH, bq, LANE), F32),
            pltpu.VMEM((bq, LANE), I32),
            pltpu.VMEM((bq, LANE), F32),
        ],
        compiler_params=_params(("arbitrary",), 60),
        name="dsa",
    )(q, qi, wi, k, v, ki, u)


def _l0_out_kernel(a2_ref, o_ref, wa_ref, wo_ref, h_ref, g_ref, b_ref, out_ref):
    m = _mm(a2_ref[...], wa_ref[...]) + _mm(o_ref[...], wo_ref[...])
    out_ref[...] = _layer_norm(DN_ALPHA * h_ref[...] + m, g_ref[...], b_ref[...])


def _l0_out(a2, o, wa, wo, h, g, b):
    lp, d = h.shape
    tm = _tile_rows(lp)
    row = lambda w: pl.BlockSpec((tm, w), lambda i: (i, 0))
    full = lambda a: pl.BlockSpec(a.shape, lambda i: (0,) * a.ndim)
    return pl.pallas_call(
        _l0_out_kernel,
        grid=(lp // tm,),
        in_specs=[row(CONV_CH), row(ATT_WIDTH), full(wa), full(wo), row(d), full(g), full(b)],
        out_specs=row(d),
        out_shape=jax.ShapeDtypeStruct((lp, d), F32),
        compiler_params=_params(("parallel",), 32),
        name="l0_out",
    )(a2, o, wa, wo, h, g, b)


def _l1_kernel(h_ref, win_ref, cw_ref, cb_ref, wa_ref, ba_ref, wx_ref, bx_ref, lam_ref, wout_ref, g_ref, b_ref,
               out_ref, xbuf, hstate):
    i = pl.program_id(0)
    tm = h_ref.shape[0]
    h = h_ref[...]
    z = _mm(h.astype(BF16), win_ref[...])
    gate = z[:, :RNN_WIDTH]

    @pl.when(i == 0)
    def _():
        xbuf[0:RNN_HALO, :] = jnp.zeros((RNN_HALO, RNN_WIDTH), F32)
        hstate[...] = jnp.zeros(hstate.shape, F32)

    xbuf[pl.ds(RNN_HALO, tm), :] = z[:, RNN_WIDTH:]
    xc = jnp.broadcast_to(cb_ref[...], (tm, RNN_WIDTH))
    for j in range(RNN_CONV_K):
        xc = xc + cw_ref[j:j + 1, :] * xbuf[pl.ds(RNN_HALO - (RNN_CONV_K - 1) + j, tm), :]
    xbuf[0:RNN_HALO, :] = xbuf[pl.ds(tm, RNN_HALO), :]

    xcb = xc.astype(BF16)
    ra, ri = [], []
    for n in range(RNN_BLOCKS):
        blk = xcb[:, n * RNN_BLOCK_W:(n + 1) * RNN_BLOCK_W]
        ra.append(_mm(blk, wa_ref[n]))
        ri.append(_mm(blk, wx_ref[n]))
    r = jax.nn.sigmoid(jnp.concatenate(ra, axis=1) + ba_ref[...])
    ig = jax.nn.sigmoid(jnp.concatenate(ri, axis=1) + bx_ref[...])
    nl = -lam_ref[...]
    softplus = jnp.maximum(nl, 0.0) + jnp.log(1.0 + jnp.exp(-jnp.abs(nl)))
    log_a = -RG_C * r * softplus
    a = jnp.exp(log_a)
    gap = 1.0 - a * a
    u = jnp.where(gap > 0.0, gap * lax.rsqrt(gap), 0.0) * (ig * xc)

    in_group = lax.broadcasted_iota(I32, (tm, RNN_WIDTH), 0) % 8
    for d in (1, 2, 4):
        keep = in_group >= d
        a_sh = jnp.where(keep, pltpu.roll(a, d, 0), 1.0)
        u_sh = jnp.where(keep, pltpu.roll(u, d, 0), 0.0)
        u = a * u_sh + u
        a = a * a_sh
    h_prev = hstate[0:1, :]
    groups = []
    for g in range(tm // 8):
        h_g = u[g * 8:(g + 1) * 8, :] + a[g * 8:(g + 1) * 8, :] * h_prev
        groups.append(h_g)
        h_prev = h_g[7:8, :]
    hs = jnp.concatenate(groups, axis=0)
    hstate[0:1, :] = h_prev

    y = (jax.nn.gelu(gate) * hs).astype(BF16)
    m = _mm(y, wout_ref[...])
    out_ref[...] = _layer_norm(DN_ALPHA * h + m, g_ref[...], b_ref[...])


def _l1_mixer(h, win, cw, cb, wa, ba, wx, bx, lam, wout, g, b):
    lp, d = h.shape
    tm = _tile_rows(lp)
    row = lambda w: pl.BlockSpec((tm, w), lambda i: (i, 0))
    full = lambda a: pl.BlockSpec(a.shape, lambda i: (0,) * a.ndim)
    args = (h, win, cw, cb, wa, ba, wx, bx, lam, wout, g, b)
    return pl.pallas_call(
        _l1_kernel,
        grid=(lp // tm,),
        in_specs=[row(d)] + [full(a) for a in args[1:]],
        out_specs=row(d),
        out_shape=jax.ShapeDtypeStruct((lp, d), F32),
        scratch_shapes=[pltpu.VMEM((RNN_HALO + tm, RNN_WIDTH), F32), pltpu.VMEM((8, RNN_WIDTH), F32)],
        compiler_params=_params(("arbitrary",), 56),
        name="l1_mixer",
    )(*args)


def _router_kernel(h_ref, whi_ref, wlo_ref, br_ref, ltri_ref, eid_ref, gate_ref, rank_ref, cnt_ref, carry_ref,
                   *, n_real):
    i = pl.program_id(0)
    tm = h_ref.shape[0]

    @pl.when(i == 0)
    def _():
        carry_ref[...] = jnp.zeros(carry_ref.shape, F32)

    h = h_ref[...]
    h_hi = h.astype(BF16)
    h_lo = (h - h_hi.astype(F32)).astype(BF16)
    logits = _mm(h_hi, whi_ref[...]) + _mm(h_lo, whi_ref[...]) + _mm(h_hi, wlo_ref[...]) + br_ref[...]
    lane = lax.broadcasted_iota(I32, (tm, LANE), 1).astype(F32)
    ninf = -jnp.inf
    big = float(LANE)

    gl = jnp.where(lane < N_GROUPS, logits[:, :LANE], ninf)
    gmax = jnp.max(gl, axis=1, keepdims=True)
    g_p = 1.0 / jnp.sum(jnp.exp(gl - gmax), axis=1, keepdims=True)
    g_idx = jnp.min(jnp.where(gl == gmax, lane, big), axis=1, keepdims=True)

    first = g_idx * EXPERTS_PER_GROUP
    el = logits[:, LANE:]
    m1 = jnp.where(lane >= first, jnp.where(lane < first + EXPERTS_PER_GROUP, el, ninf), ninf)
    t1 = jnp.max(m1, axis=1, keepdims=True)
    i1 = jnp.min(jnp.where(m1 == t1, lane, big), axis=1, keepdims=True)
    m2 = jnp.where(lane == i1, ninf, m1)
    t2 = jnp.max(m2, axis=1, keepdims=True)
    i2 = jnp.min(jnp.where(m2 == t2, lane, big), axis=1, keepdims=True)
    e2 = jnp.exp(t2 - t1)
    den = 1.0 / (1.0 + e2)

    tok = i * tm + lax.broadcasted_iota(I32, (tm, LANE), 0)
    valid = tok < n_real
    oh0 = jnp.where(valid, jnp.where(lane == i1, 1.0, 0.0), 0.0)
    oh1 = jnp.where(valid, jnp.where(lane == i2, 1.0, 0.0), 0.0)
    ohs = oh0 + oh1
    before = _mm(ltri_ref[...], ohs.astype(BF16)) + carry_ref[...]
    carry_ref[...] = carry_ref[...] + jnp.sum(ohs, axis=0, keepdims=True)
    cnt_ref[...] = carry_ref[...]

    eid_ref[:, 0:1] = i1.astype(I32)
    eid_ref[:, 1:2] = i2.astype(I32)
    gate_ref[:, 0:1] = g_p * den
    gate_ref[:, 1:2] = g_p * e2 * den
    rank_ref[:, 0:1] = jnp.sum(oh0 * before, axis=1, keepdims=True).astype(I32)
    rank_ref[:, 1:2] = jnp.sum(oh1 * before, axis=1, keepdims=True).astype(I32)


def _router(h, wr, br, n_real):
    lp, d = h.shape
    tm = _tile_rows(lp)
    w_hi = wr.astype(BF16)
    w_lo = (wr - w_hi.astype(F32)).astype(BF16)
    ltri = jnp.asarray((np.arange(tm)[:, None] > np.arange(tm)[None, :]).astype(np.float32), BF16)
    row = lambda w: pl.BlockSpec((tm, w), lambda i: (i, 0))
    full = lambda a: pl.BlockSpec(a.shape, lambda i: (0,) * a.ndim)
    return pl.pallas_call(
        functools.partial(_router_kernel, n_real=n_real),
        grid=(lp // tm,),
        in_specs=[row(d), full(w_hi), full(w_lo), full(br), full(ltri)],
        out_specs=[row(2), row(2), row(2), pl.BlockSpec((1, LANE), lambda i: (0, 0))],
        out_shape=[jax.ShapeDtypeStruct((lp, 2), I32), jax.ShapeDtypeStruct((lp, 2), F32),
                   jax.ShapeDtypeStruct((lp, 2), I32), jax.ShapeDtypeStruct((1, LANE), F32)],
        scratch_shapes=[pltpu.VMEM((1, LANE), F32)],
        compiler_params=_params(("arbitrary",), 32),
        name="moe_router",
    )(h, w_hi, w_lo, br, ltri)


def _row_copy(src_ref, src_row, dst_ref, dst_row, sem):
    return pltpu.make_async_copy(src_ref.at[pl.ds(src_row, 1), :], dst_ref.at[pl.ds(dst_row, 1), :], sem)


def _dispatch_kernel(dest_ref, h_ref, xb_in_ref, xb_ref, packed, sem):
    del xb_in_ref
    i = pl.program_id(0)
    tm = h_ref.shape[0]
    packed[...] = _pack_bf16_pairs(h_ref[...])

    def issue(r8, carry):
        for u in range(ROW_DMA_UNROLL):
            r = r8 * ROW_DMA_UNROLL + u
            for s in range(2):
                _row_copy(packed, r, xb_ref, dest_ref[2 * (i * tm + r) + s], sem).start()
        return carry

    lax.fori_loop(0, tm // ROW_DMA_UNROLL, issue, 0)
    for s in range(2):
        pltpu.make_async_copy(packed, xb_ref.at[pl.ds(0, tm), :], sem).wait()


def _dispatch(dest_flat, h, n_rows_out):
    lp, d = h.shape
    tm = _tile_rows(lp)
    xb0 = jnp.zeros((n_rows_out, d // 2), I32)
    grid_spec = pltpu.PrefetchScalarGridSpec(
        num_scalar_prefetch=1,
        grid=(lp // tm,),
        in_specs=[pl.BlockSpec((tm, d), lambda i, dest: (i, 0)), pl.BlockSpec(memory_space=pl.ANY)],
        out_specs=pl.BlockSpec(memory_space=pl.ANY),
        scratch_shapes=[pltpu.VMEM((tm, d // 2), I32), pltpu.SemaphoreType.DMA(())],
    )
    return pl.pallas_call(
        _dispatch_kernel,
        grid_spec=grid_spec,
        out_shape=jax.ShapeDtypeStruct((n_rows_out, d // 2), I32),
        input_output_aliases={2: 0},
        compiler_params=pltpu.CompilerParams(dimension_semantics=("arbitrary",), has_side_effects=True),
        name="moe_dispatch",
    )(dest_flat, h, xb0)


def _experts_kernel(bexp_ref, nused_ref, xb_ref, wg_ref, wu_ref, wd_ref, yb_ref, wg_s, wu_s, wd_s):
    b = pl.program_id(0)
    prev = bexp_ref[jnp.maximum(b - 1, 0)]

    @pl.when((b == 0) | (bexp_ref[b] != prev))
    def _():
        wg_s[...] = wg_ref[...].astype(BF16)
        wu_s[...] = wu_ref[...].astype(BF16)
        wd_s[...] = wd_ref[...].astype(BF16)

    @pl.when(b < nused_ref[0])
    def _():
        x = _unpack_bf16_pairs(xb_ref[...]).astype(BF16)
        gt = _mm(x, wg_s[...])
        up = _mm(x, wu_s[...])
        mid = (gt * jax.nn.sigmoid(gt) * up).astype(BF16)
        yb_ref[...] = _pack_bf16_pairs(_mm(mid, wd_s[...]))

    @pl.when(b >= nused_ref[0])
    def _():
        yb_ref[...] = jnp.zeros(yb_ref.shape, I32)


def _experts(bexp, nused, xb, wg, wu, wd, layer, n_blocks):
    d = xb.shape[1]
    bm = EXPERT_ROWS
    wspec = lambda a: pl.BlockSpec((None, None) + a.shape[2:], lambda b, bexp, *_: (layer, bexp[b], 0, 0))
    grid_spec = pltpu.PrefetchScalarGridSpec(
        num_scalar_prefetch=2,
        grid=(n_blocks,),
        in_specs=[pl.BlockSpec((bm, d), lambda b, *_: (b, 0)), wspec(wg), wspec(wu), wspec(wd)],
        out_specs=pl.BlockSpec((bm, d), lambda b, *_: (b, 0)),
        scratch_shapes=[pltpu.VMEM(wg.shape[2:], BF16), pltpu.VMEM(wu.shape[2:], BF16),
                        pltpu.VMEM(wd.shape[2:], BF16)],
    )
    return pl.pallas_call(
        _experts_kernel,
        grid_spec=grid_spec,
        out_shape=jax.ShapeDtypeStruct((n_blocks * bm, d), I32),
        compiler_params=_params(("arbitrary",), 48),
        name="moe_experts",
    )(bexp, nused, xb, wg, wu, wd)


def _combine_kernel(src_ref, yb_ref, gate_ref, h_ref, g_ref, b_ref, out_ref, ybuf, sem):
    i = pl.program_id(0)
    tm = h_ref.shape[0]

    def issue(r8, carry):
        for u in range(ROW_DMA_UNROLL):
            r = r8 * ROW_DMA_UNROLL + u
            for s in range(2):
                _row_copy(yb_ref, src_ref[2 * (i * tm + r) + s], ybuf.at[s], r, sem).start()
        return carry

    lax.fori_loop(0, tm // ROW_DMA_UNROLL, issue, 0)
    for s in range(2):
        pltpu.make_async_copy(yb_ref.at[pl.ds(0, tm), :], ybuf.at[s], sem).wait()
    gate = gate_ref[...]
    y = gate[:, 0:1] * _unpack_bf16_pairs(ybuf[0]) + gate[:, 1:2] * _unpack_bf16_pairs(ybuf[1])
    out_ref[...] = _layer_norm(DN_ALPHA * h_ref[...] + y, g_ref[...], b_ref[...])


def _combine(dest_flat, yb, gate, h, g, b):
    lp, d = h.shape
    tm = _tile_rows(lp)
    full = lambda a: pl.BlockSpec(a.shape, lambda i, dest: (0,) * a.ndim)
    grid_spec = pltpu.PrefetchScalarGridSpec(
        num_scalar_prefetch=1,
        grid=(lp // tm,),
        in_specs=[pl.BlockSpec(memory_space=pl.ANY), pl.BlockSpec((tm, 2), lambda i, dest: (i, 0)),
                  pl.BlockSpec((tm, d), lambda i, dest: (i, 0)), full(g), full(b)],
        out_specs=pl.BlockSpec((tm, d), lambda i, dest: (i, 0)),
        scratch_shapes=[pltpu.VMEM((2, tm, d // 2), I32), pltpu.SemaphoreType.DMA(())],
    )
    return pl.pallas_call(
        _combine_kernel,
        grid_spec=grid_spec,
        out_shape=jax.ShapeDtypeStruct((lp, d), F32),
        compiler_params=_params(("arbitrary",), 32),
        name="moe_combine",
    )(dest_flat, yb, gate, h, g, b)


def _moe(h, n_real, layer, wg, bg, we, be, w_gate, w_up, w_down, ln_g, ln_b):
    lp, d = h.shape
    bm = EXPERT_ROWS
    wr = jnp.zeros((d, 2 * LANE), F32).at[:, :N_GROUPS].set(wg).at[:, LANE:LANE + N_EXPERTS].set(we)
    br = jnp.zeros((1, 2 * LANE), F32).at[0, :N_GROUPS].set(bg).at[0, LANE:LANE + N_EXPERTS].set(be)
    eid, gate, rank, cnt = _router(h, wr, br, n_real)

    counts = cnt[0, :N_EXPERTS].astype(I32)
    padded = (counts + bm - 1) // bm * bm
    pend = jnp.cumsum(padded)
    pstart = pend - padded
    n_blocks = -(-(2 * n_real + N_EXPERTS * (bm - 1)) // bm)
    cap = n_blocks * bm
    tok = jnp.arange(lp, dtype=I32)[:, None]
    valid = tok < n_real
    experts = jnp.arange(N_EXPERTS, dtype=I32)
    row = jnp.sum(jnp.where(eid[:, :, None] == experts, pstart, 0), axis=-1) + rank
    dest_flat = jnp.where(valid, row, cap + 2 * (tok - n_real) + jnp.arange(2, dtype=I32)[None, :]).reshape(-1)
    src_flat = jnp.where(valid, row, 0).reshape(-1)
    block_start = jnp.arange(n_blocks, dtype=I32) * bm
    bexp = jnp.minimum(jnp.sum((pend[None, :] <= block_start[:, None]).astype(I32), axis=1), N_EXPERTS - 1)
    nused = (pend[-1:] // bm).astype(I32)

    xb = _dispatch(dest_flat, h, cap + 2 * (lp - n_real))
    yb = _experts(bexp, nused, xb, w_gate, w_up, w_down, layer, n_blocks)
    return _combine(src_flat, yb, gate, h, ln_g, ln_b)


def kernel(x, meta_tokens, ab_w_in, ab_conv_w, ab_conv_b, ab_ln_g, ab_ln_b, ab_w_out, c_w_in, c_conv_w, c_conv_b, c_gate_a_w, c_gate_a_b, c_gate_x_w, c_gate_x_b, c_lambda, c_w_out, moe_router_group_w, moe_router_group_b, moe_router_expert_w, moe_router_expert_b, moe_w_gate, moe_w_up, moe_w_down, ln_mix_g, ln_mix_b, ln_ffn_g, ln_ffn_b):
    bsz, seq, d = x.shape
    assert bsz == 1, "kernel is written for batch 1"
    n_real = N_META + seq
    lp = -(-n_real // Q_BLOCK) * Q_BLOCK
    lk = -(-lp // KEY_CHUNK) * KEY_CHUNK
    topk = min(TOPK_MAX, seq // 4)
    row2 = lambda a: a.reshape(1, -1)

    h = jnp.concatenate([meta_tokens.astype(x.dtype), x[0], jnp.zeros((lp - n_real, d), x.dtype)], axis=0)

    half = HEAD_DIM // 2
    inv_freq = (np.float32(ROPE_THETA) ** (np.float32(-2.0) * np.arange(half, dtype=np.float32)
                                           / np.float32(HEAD_DIM))).astype(np.float32)
    ang = (np.arange(lp, dtype=np.float32)[:, None] * inv_freq[None, :]).astype(np.float64)
    cos = jnp.tile(jnp.asarray(np.cos(ang), F32), (1, 4))
    sin_half = jnp.asarray(np.sin(ang), F32)
    sin = jnp.tile(jnp.concatenate([-sin_half, sin_half], axis=1), (1, 2))

    for layer in range(DEPTH):
        j = layer // 2
        if layer % 2 == 0:
            w_in = ab_w_in[j]
            wglu = w_in[:, :2 * CONV_CH].astype(BF16)
            wqkv = w_in[:, 2 * CONV_CH:2 * CONV_CH + 4 * ATT_WIDTH].astype(BF16)
            wsm = jnp.zeros((d, LANE), F32).at[:, :IDX_DIM + IDX_HEADS].set(
                w_in[:, 2 * CONV_CH + 4 * ATT_WIDTH:]).astype(BF16)
            a2, q, k, v, qi, ki, wi = _l0_in(h, wglu, wqkv, wsm, cos, sin, ab_conv_w[j], row2(ab_conv_b[j]),
                                             row2(ab_ln_g[j]), row2(ab_ln_b[j]))
            pad = lambda t: jnp.pad(t, ((0, lk - lp), (0, 0)))
            o = _dsa(q, qi, wi, pad(k), pad(v), pad(ki), topk)
            w_out = ab_w_out[j].astype(BF16)
            h = _l0_out(a2, o, w_out[:CONV_CH], w_out[CONV_CH:], h, row2(ln_mix_g[layer]), row2(ln_mix_b[layer]))
        else:
            h = _l1_mixer(h, c_w_in[j].astype(BF16), c_conv_w[j], row2(c_conv_b[j]),
                          c_gate_a_w[j].astype(BF16), row2(c_gate_a_b[j]),
                          c_gate_x_w[j].astype(BF16), row2(c_gate_x_b[j]), row2(c_lambda[j]),
                          c_w_out[j].astype(BF16), row2(ln_mix_g[layer]), row2(ln_mix_b[layer]))
        h = _moe(h, n_real, layer, moe_router_group_w[layer], moe_router_group_b[layer],
                 moe_router_expert_w[layer], moe_router_expert_b[layer], moe_w_gate, moe_w_up, moe_w_down,
                 row2(ln_ffn_g[layer]), row2(ln_ffn_b[layer]))
    return h[N_META:n_real][None]
```

```python
import functools

import jax
import jax.numpy as jnp
import numpy as np
from jax import lax
from jax.experimental import pallas as pl
from jax.experimental.pallas import tpu as pltpu

F32 = jnp.float32
BF16 = jnp.bfloat16
I32 = jnp.int32

N_META = 16
CONV_CH = 512
CONV_K = 31
ATT_HEADS = 8
HEAD_DIM = 64
ATT_WIDTH = ATT_HEADS * HEAD_DIM
IDX_HEADS = 8
IDX_DIM = 64
TOPK_MAX = 256
ROPE_THETA = 10000.0
RNN_WIDTH = 1280
RNN_BLOCKS = 10
RNN_BLOCK_W = RNN_WIDTH // RNN_BLOCKS
RNN_CONV_K = 4
RG_C = 8.0
N_GROUPS = 4
EXPERTS_PER_GROUP = 8
N_EXPERTS = N_GROUPS * EXPERTS_PER_GROUP
D_EXPERT = 512
LN_EPS = 1e-5
DEPTH = 2
DN_ALPHA = (2 * DEPTH) ** 0.25

LANE = 128
VMEM_BYTES = 64 << 20

Q_BLOCK = 128
KEY_CHUNK = 512
LIST_DEPTH = 12
CONV_HALO = 32
RNN_HALO = 8
EXPERT_ROWS = 256
ROW_DMA_UNROLL = 8
FLT_MAX = 3.4028234663852886e38
MIN_NORMAL_KEY = 1 << 23
NEG_MASK = -2e30
M_INIT = -1e30
LOG2_E = 1.4426950408889634

_NT = (((1,), (1,)), ((), ()))


def _tile_rows(n):
    for t in (512, 384, 256, 128):
        if n % t == 0:
            return t
    raise ValueError(n)


def _mm(a, b):
    return jnp.dot(a, b, preferred_element_type=F32)


def _layer_norm(x, g, b):
    mu = jnp.mean(x, axis=-1, keepdims=True)
    xc = x - mu
    var = jnp.mean(xc * xc, axis=-1, keepdims=True)
    return xc * lax.rsqrt(var + LN_EPS) * g + b


def _pack_bf16_pairs(x):
    n = x.shape[1] // 2
    bits = lax.bitcast_convert_type(x.astype(BF16).astype(F32), I32)
    return bits[:, :n] | lax.shift_right_logical(bits[:, n:], 16)


def _unpack_bf16_pairs(u):
    first = lax.bitcast_convert_type(u & jnp.int32(-65536), F32)
    second = lax.bitcast_convert_type(lax.shift_left(u, 16), F32)
    return jnp.concatenate([first, second], axis=1)


def _rope_group(t, cos, sin_signed, first_half):
    partner = jnp.where(first_half, pltpu.roll(t, LANE - 32, 1), pltpu.roll(t, 32, 1))
    return t * cos + partner * sin_signed


def _params(sem, vmem_mb):
    return pltpu.CompilerParams(dimension_semantics=sem, vmem_limit_bytes=vmem_mb << 20)


def _l0_in_kernel(h_ref, wglu_ref, wqkv_ref, wsm_ref, cos_ref, sin_ref, cw_ref, cb_ref, lg_ref, lb_ref,
                  a2_ref, q_ref, k_ref, v_ref, qi_ref, ki_ref, wi_ref, abuf, shift_buf):
    i = pl.program_id(0)
    tm = h_ref.shape[0]
    hb = h_ref[...].astype(BF16)

    glu = _mm(hb, wglu_ref[...])
    a = glu[:, :CONV_CH] * jax.nn.sigmoid(glu[:, CONV_CH:])

    @pl.when(i == 0)
    def _():
        abuf[0:CONV_HALO, :] = jnp.zeros((CONV_HALO, CONV_CH), F32)

    abuf[pl.ds(CONV_HALO, tm), :] = a
    acc = jnp.broadcast_to(cb_ref[...], (tm, CONV_CH))
    first_off = CONV_HALO - (CONV_K - 1)
    for r in range(8):
        taps = [j for j in range(CONV_K) if (first_off + j) % 8 == r]
        n_rows = tm if r == 0 else tm + 8
        part = None
        for j in taps:
            base = (first_off + j) - r
            term = cw_ref[j:j + 1, :] * abuf[pl.ds(base, n_rows), :]
            part = term if part is None else part + term
        if r == 0:
            acc = acc + part
        else:
            shift_buf[...] = part
            acc = acc + shift_buf[pl.ds(r, tm), :]
    abuf[0:CONV_HALO, :] = abuf[pl.ds(tm, CONV_HALO), :]
    y = _layer_norm(acc, lg_ref[...], lb_ref[...])
    a2_ref[...] = (y * jax.nn.sigmoid(y)).astype(BF16)

    cos = cos_ref[...]
    sin = sin_ref[...]
    lane = lax.broadcasted_iota(I32, (tm, LANE), 1)
    first_half = (lane % HEAD_DIM) < (HEAD_DIM // 2)
    qkv = _mm(hb, wqkv_ref[...])
    for g in range(ATT_WIDTH // LANE):
        sl = slice(g * LANE, (g + 1) * LANE)
        qg = qkv[:, g * LANE:(g + 1) * LANE]
        kg = qkv[:, ATT_WIDTH + g * LANE:ATT_WIDTH + (g + 1) * LANE]
        ig = qkv[:, 3 * ATT_WIDTH + g * LANE:3 * ATT_WIDTH + (g + 1) * LANE]
        q_ref[:, sl] = (_rope_group(qg, cos, sin, first_half) * (LOG2_E * HEAD_DIM ** -0.5)).astype(BF16)
        k_ref[:, sl] = _rope_group(kg, cos, sin, first_half).astype(BF16)
        qi_ref[:, sl] = (_rope_group(ig, cos, sin, first_half) * (IDX_DIM ** -0.5)).astype(BF16)
    v_ref[...] = qkv[:, 2 * ATT_WIDTH:3 * ATT_WIDTH].astype(BF16)

    sm = _mm(hb, wsm_ref[...])
    ki = _rope_group(sm, cos, sin, first_half)
    ki_ref[...] = jnp.where(lane < IDX_DIM, ki, 0.0).astype(BF16)
    wi_ref[...] = sm * (IDX_HEADS ** -0.5)


def _l0_in(h, wglu, wqkv, wsm, cos, sin, cw, cb, lg, lb):
    lp, d = h.shape
    tm = _tile_rows(lp)
    row = lambda w: pl.BlockSpec((tm, w), lambda i: (i, 0))
    full = lambda a: pl.BlockSpec(a.shape, lambda i: (0,) * a.ndim)
    outs = [jax.ShapeDtypeStruct((lp, CONV_CH), BF16)] + [jax.ShapeDtypeStruct((lp, ATT_WIDTH), BF16)] * 4 + [
        jax.ShapeDtypeStruct((lp, LANE), BF16), jax.ShapeDtypeStruct((lp, LANE), F32)]
    return pl.pallas_call(
        _l0_in_kernel,
        grid=(lp // tm,),
        in_specs=[row(d), full(wglu), full(wqkv), full(wsm), row(LANE), row(LANE), full(cw), full(cb), full(lg),
                  full(lb)],
        out_specs=[row(CONV_CH)] + [row(ATT_WIDTH)] * 4 + [row(LANE), row(LANE)],
        out_shape=outs,
        scratch_shapes=[pltpu.VMEM((CONV_HALO + tm, CONV_CH), F32), pltpu.VMEM((tm + 8, CONV_CH), F32)],
        compiler_params=_params(("arbitrary",), 48),
        name="l0_in",
    )(h, wglu, wqkv, wsm, cos, sin, cw, cb, lg, lb)


def _two_stage_chunks(nch, first, second):
    first(0, 0)

    def body(t, carry):
        c = 2 * t
        first(c + 1, 1)
        second(c, 0, False)
        first(c + 2, 0)
        second(c + 1, 1, False)
        return carry

    n_pairs = (nch - 1) // 2
    lax.fori_loop(0, n_pairs, body, 0)
    c0 = 2 * n_pairs

    @pl.when(nch - c0 == 2)
    def _():
        first(c0 + 1, 1)
        second(c0, 0, False)
        second(c0 + 1, 1, True)

    @pl.when(nch - c0 == 1)
    def _():
        second(c0, 0, True)


def _dsa_kernel(q_ref, qi_ref, wi_ref, k_ref, v_ref, ki_ref, u_ref, o_ref,
                s_ref, qi8_ref, qh_ref, wb_ref, m_ref, acc_ref, pbuf, lohi_ref, cand_ref, thr_ref, nlo_ref,
                *, topk):
    i = pl.program_id(0)
    bq = q_ref.shape[0]
    ck = KEY_CHUNK
    ngrp = ck // LANE
    nch = ((i + 1) * bq + ck - 1) // ck
    lane = lax.broadcasted_iota(I32, (bq, LANE), 1)
    low = lane < HEAD_DIM
    chunk_rows = lambda c: pl.ds(pl.multiple_of(c * ck, ck), ck)

    for g in range(ATT_WIDTH // LANE):
        pair = qi_ref[:, g * LANE:(g + 1) * LANE].astype(F32)
        qi8_ref[pl.ds((2 * g) * bq, bq), :] = jnp.where(low, pair, 0.0).astype(BF16)
        qi8_ref[pl.ds((2 * g + 1) * bq, bq), :] = jnp.where(low, pltpu.roll(pair, HEAD_DIM, 1), 0.0).astype(BF16)
        qp = q_ref[:, g * LANE:(g + 1) * LANE].astype(F32)
        qh_ref[g, 0:bq, :] = jnp.where(low, qp, 0.0).astype(BF16)
        qh_ref[g, bq:2 * bq, :] = jnp.where(low, 0.0, qp).astype(BF16)
    wi = wi_ref[...]
    for h in range(IDX_HEADS):
        wb_ref[h] = jnp.broadcast_to(wi[:, IDX_DIM + h:IDX_DIM + h + 1], (bq, LANE))
    lohi_ref[0] = jnp.full((bq, LANE), jnp.inf, F32)
    lohi_ref[1] = jnp.full((bq, LANE), -jnp.inf, F32)

    def to_key(x):
        bits = lax.bitcast_convert_type(x, I32)
        return bits ^ ((bits >> 31) & jnp.int32(0x7FFFFFFF))

    def from_key(k):
        k = jnp.where(k > 0, jnp.where(k < MIN_NORMAL_KEY, MIN_NORMAL_KEY, k),
                      jnp.where(k >= -MIN_NORMAL_KEY, 0, k))
        return lax.bitcast_convert_type(k ^ ((k >> 31) & jnp.int32(0x7FFFFFFF)), F32)

    def index_logits(c, slot):
        pbuf[slot] = lax.dot_general(qi8_ref[...], ki_ref[chunk_rows(c), :], _NT, preferred_element_type=F32)

    cand_ref[...] = jnp.full(cand_ref.shape, -jnp.inf, F32)
    lane8 = lax.broadcasted_iota(I32, (8, LANE), 1)
    row8 = lax.broadcasted_iota(I32, (8, LANE), 0)

    def index_scores(c, slot, last):
        for slab in range(bq // 8):
            rows = slice(slab * 8, (slab + 1) * 8)
            smin = lohi_ref[0, rows, :]
            smax = lohi_ref[1, rows, :]
            lst = [cand_ref[d, rows, :] for d in range(LIST_DEPTH)]
            for j in range(ngrp):
                cols = slice(j * LANE, (j + 1) * LANE)
                sc = jnp.zeros((8, LANE), F32)
                for h in range(IDX_HEADS):
                    hrows = slice(h * bq + slab * 8, h * bq + (slab + 1) * 8)
                    sc = sc + jnp.maximum(pbuf[slot, hrows, cols], 0.0) * wb_ref[h, rows, :]
                if last:
                    valid = (c * ck + j * LANE + lane8) <= (i * bq + slab * 8 + row8)
                    smin = jnp.minimum(smin, jnp.where(valid, sc, jnp.inf))
                    sc = jnp.where(valid, sc, -jnp.inf)
                else:
                    smin = jnp.minimum(smin, sc)
                smax = jnp.maximum(smax, sc)
                s_ref[c, rows, cols] = sc
                x = sc
                for d in range(LIST_DEPTH):
                    top = jnp.maximum(lst[d], x)
                    x = jnp.minimum(lst[d], x)
                    lst[d] = top
            for d in range(LIST_DEPTH):
                cand_ref[d, rows, :] = lst[d]
            lohi_ref[0, rows, :] = smin
            lohi_ref[1, rows, :] = smax

    _two_stage_chunks(nch, index_logits, index_scores)

    def count_ge(cand):
        cand_f = from_key(cand)

        def body(c, cnt):
            blk = s_ref[c]
            for g in range(ngrp):
                cnt = cnt + jnp.where(blk[:, g * LANE:(g + 1) * LANE] >= cand_f, 1.0, 0.0)
            return cnt
        cnt = lax.fori_loop(0, nch, body, jnp.zeros((bq, LANE), F32))
        return jnp.sum(cnt, axis=1, keepdims=True)

    kf = float(topk)
    zeros = jnp.zeros((bq, LANE), F32)
    n_valid = (i * bq + lax.broadcasted_iota(I32, (bq, LANE), 0) + 1).astype(F32)
    lo0 = to_key(jnp.min(lohi_ref[0], axis=1, keepdims=True) + zeros)
    hi0 = to_key(jnp.max(lohi_ref[1], axis=1, keepdims=True) + zeros) + 1

    def open_rows(lo, hi, n_lo):
        return jnp.where(n_lo > kf, jnp.where(hi > lo + 1, 1.0, 0.0), 0.0)

    def bisect(count_fn, steps, lo_init, n_init):
        def cond(st):
            it, _, _, _, any_open = st
            return jnp.logical_and(it < 34, any_open)

        def body(st):
            it, lo, hi, n_lo, _ = st
            for _ in range(steps):
                is_open = open_rows(lo, hi, n_lo) > 0.0
                mid = (lo >> 1) + (hi >> 1) + (lo & hi & 1)
                cand = jnp.where(is_open, mid, lo)
                n = count_fn(cand) + zeros
                ge = n >= kf
                lo, hi, n_lo = (jnp.where(is_open, jnp.where(ge, cand, lo), lo),
                                jnp.where(is_open, jnp.where(ge, hi, cand), hi),
                                jnp.where(is_open, jnp.where(ge, n, n_lo), n_lo))
            return it + steps, lo, hi, n_lo, jnp.max(open_rows(lo, hi, n_lo)) > 0.0

        st = lax.while_loop(cond, body, (jnp.int32(0), lo_init, hi0, n_init,
                                         jnp.max(open_rows(lo_init, hi0, n_init)) > 0.0))
        return st[1], st[3]

    def count_ge_lists(cand):
        cand_f = from_key(cand)
        cnt = jnp.zeros((bq, LANE), F32)
        for d in range(LIST_DEPTH):
            cnt = cnt + jnp.where(cand_ref[d] >= cand_f, 1.0, 0.0)
        return jnp.sum(cnt, axis=1, keepdims=True)

    runner_up = jnp.min(cand_ref[1], axis=1, keepdims=True) + zeros
    have_two = runner_up > -jnp.inf
    lo_start = jnp.where(have_two, to_key(runner_up), lo0)
    n_start = jnp.where(have_two, count_ge_lists(lo_start) + zeros, n_valid)
    lo_l, n_l = bisect(count_ge_lists, 2, lo_start, n_start)
    thr_ref[...] = lo_l
    nlo_ref[...] = n_l
    deepest = jnp.where(cand_ref[LIST_DEPTH - 1] >= from_key(lo_l), 1.0, 0.0)
    lists_short = jnp.max(jnp.where(n_valid > kf, deepest, 0.0)) > 0.0

    @pl.when(lists_short)
    def _():
        lo_a, n_a = bisect(count_ge, 1, lo0, n_valid)
        thr_ref[...] = lo_a
        nlo_ref[...] = n_a

    n_lo = nlo_ref[...]
    take_all = n_valid <= kf
    thr_key = thr_ref[...]
    thr = jnp.where(take_all, -FLT_MAX, from_key(thr_key))

    m_ref[...] = jnp.full(m_ref.shape, M_INIT, F32)
    acc_ref[...] = jnp.zeros(acc_ref.shape, F32)
    ones = jnp.ones((ck, LANE), BF16)
    pair_rows = lambda g: slice(g * 2 * bq, (g + 1) * 2 * bq)

    def masked_scores(c, slot, bias):
        bias2 = jnp.concatenate([jnp.concatenate(bias, axis=1)] * 2, axis=0)
        for g in range(ATT_WIDTH // LANE):
            kc = k_ref[chunk_rows(c), g * LANE:(g + 1) * LANE]
            pbuf[slot, pair_rows(g), :] = lax.dot_general(qh_ref[g], kc, _NT, preferred_element_type=F32) + bias2

    def softmax_pv(c, slot, last):
        del last
        for g in range(ATT_WIDTH // LANE):
            sj = [pbuf[slot, pair_rows(g), j * LANE:(j + 1) * LANE] for j in range(ngrp)]
            mx = sj[0]
            for j in range(1, ngrp):
                mx = jnp.maximum(mx, sj[j])
            m_prev = m_ref[g]
            m_new = jnp.maximum(m_prev, jnp.max(mx, axis=1, keepdims=True))
            alpha = jnp.exp2(m_prev - m_new)
            p2 = jnp.concatenate([jnp.exp2(s - m_new) for s in sj], axis=1).astype(BF16)
            v_aug = jnp.concatenate([v_ref[chunk_rows(c), g * LANE:(g + 1) * LANE], ones], axis=1)
            acc_ref[g] = jnp.concatenate([alpha, alpha], axis=1) * acc_ref[g] + _mm(p2, v_aug)
            m_ref[g] = m_new

    has_ties = jnp.max(n_lo) > kf

    @pl.when(jnp.logical_not(has_ties))
    def _():
        def scores(c, slot):
            masked_scores(c, slot, [jnp.where(s_ref[c, :, j * LANE:(j + 1) * LANE] >= thr, 0.0, NEG_MASK)
                                    for j in range(ngrp)])

        _two_stage_chunks(nch, scores, softmax_pv)

    @pl.when(has_ties)
    def _():
        n_above = count_ge(thr_key + 1)
        need = jnp.where(take_all[:, 0:1], 0.0, kf - n_above)
        thr_b = jnp.broadcast_to(thr[:, 0:1], (bq, ck))
        need_b = jnp.broadcast_to(need, (bq, ck))

        def attend(c, ties_seen):
            sc = s_ref[c]
            eq = sc == thr_b
            prefix = _mm(jnp.where(eq, 1.0, 0.0).astype(BF16), u_ref[...]) + ties_seen
            take_tie = jnp.where(eq, jnp.where(prefix <= need_b, 0.0, NEG_MASK), NEG_MASK)
            bias = jnp.where(sc > thr_b, 0.0, take_tie)
            masked_scores(c, 0, [bias[:, j * LANE:(j + 1) * LANE] for j in range(ngrp)])
            softmax_pv(c, 0, False)
            return prefix[:, ck - 1:ck]

        lax.fori_loop(0, nch, attend, jnp.zeros((bq, 1), F32))

    for g in range(ATT_WIDTH // LANE):
        acc = acc_ref[g]
        o2 = acc[:, :LANE] * (1.0 / acc[:, LANE:])
        o_ref[:, g * LANE:(g + 1) * LANE] = jnp.where(low, o2[:bq], o2[bq:]).astype(BF16)


def _dsa(q, qi, wi, k, v, ki, topk):
    lp = q.shape[0]
    lk = k.shape[0]
    bq, ck = Q_BLOCK, KEY_CHUNK
    assert topk <= min(ck, 2 * LANE) and lk % ck == 0 and lk >= lp
    u = (np.arange(ck)[:, None] <= np.arange(ck)[None, :]).astype(np.float32)
    u = jnp.asarray(u, BF16)
    row = lambda w: pl.BlockSpec((bq, w), lambda i: (i, 0))
    res = lambda a: pl.BlockSpec(a.shape, lambda i: (0, 0), pipeline_mode=pl.Buffered(1))
    return pl.pallas_call(
        functools.partial(_dsa_kernel, topk=topk),
        grid=(lp // bq,),
        in_specs=[row(ATT_WIDTH), row(ATT_WIDTH), row(LANE), res(k), res(v), res(ki), res(u)],
        out_specs=row(ATT_WIDTH),
        out_shape=jax.ShapeDtypeStruct((lp, ATT_WIDTH), BF16),
        scratch_shapes=[
            pltpu.VMEM((lk // ck, bq, ck), F32),
            pltpu.VMEM((IDX_HEADS * bq, LANE), BF16),
            pltpu.VMEM((ATT_HEADS // 2, 2 * bq, LANE), BF16),
            pltpu.VMEM((IDX_HEADS, bq, LANE), F32),
            pltpu.VMEM((ATT_HEADS // 2, 2 * bq, LANE), F32),
            pltpu.VMEM((ATT_HEADS // 2, 2 * bq, 2 * LANE), F32),
            pltpu.VMEM((2, IDX_HEADS * bq, ck), F32),
            pltpu.VMEM((2, bq, LANE), F32),
            pltpu.VMEM((LIST_DEPTH, bq, LANE), F32),
            pltpu.VMEM((bq, LANE), I32),
            pltpu.VMEM((bq, LANE), F32),
        ],
        compiler_params=_params(("arbitrary",), 60),
        name="dsa",
    )(q, qi, wi, k, v, ki, u)


def _l0_out_kernel(a2_ref, o_ref, wa_ref, wo_ref, h_ref, g_ref, b_ref, out_ref):
    m = _mm(a2_ref[...], wa_ref[...]) + _mm(o_ref[...], wo_ref[...])
    out_ref[...] = _layer_norm(DN_ALPHA * h_ref[...] + m, g_ref[...], b_ref[...])


def _l0_out(a2, o, wa, wo, h, g, b):
    lp, d = h.shape
    tm = _tile_rows(lp)
    row = lambda w: pl.BlockSpec((tm, w), lambda i: (i, 0))
    full = lambda a: pl.BlockSpec(a.shape, lambda i: (0,) * a.ndim)
    return pl.pallas_call(
        _l0_out_kernel,
        grid=(lp // tm,),
        in_specs=[row(CONV_CH), row(ATT_WIDTH), full(wa), full(wo), row(d), full(g), full(b)],
        out_specs=row(d),
        out_shape=jax.ShapeDtypeStruct((lp, d), F32),
        compiler_params=_params(("parallel",), 32),
        name="l0_out",
    )(a2, o, wa, wo, h, g, b)


def _l1_kernel(h_ref, win_ref, cw_ref, cb_ref, wa_ref, ba_ref, wx_ref, bx_ref, lam_ref, wout_ref, g_ref, b_ref,
               out_ref, xbuf, hstate):
    i = pl.program_id(0)
    tm = h_ref.shape[0]
    h = h_ref[...]
    z = _mm(h.astype(BF16), win_ref[...])
    gate = z[:, :RNN_WIDTH]

    @pl.when(i == 0)
    def _():
        xbuf[0:RNN_HALO, :] = jnp.zeros((RNN_HALO, RNN_WIDTH), F32)
        hstate[...] = jnp.zeros(hstate.shape, F32)

    xbuf[pl.ds(RNN_HALO, tm), :] = z[:, RNN_WIDTH:]
    xc = jnp.broadcast_to(cb_ref[...], (tm, RNN_WIDTH))
    for j in range(RNN_CONV_K):
        xc = xc + cw_ref[j:j + 1, :] * xbuf[pl.ds(RNN_HALO - (RNN_CONV_K - 1) + j, tm), :]
    xbuf[0:RNN_HALO, :] = xbuf[pl.ds(tm, RNN_HALO), :]

    xcb = xc.astype(BF16)
    ra, ri = [], []
    for n in range(RNN_BLOCKS):
        blk = xcb[:, n * RNN_BLOCK_W:(n + 1) * RNN_BLOCK_W]
        ra.append(_mm(blk, wa_ref[n]))
        ri.append(_mm(blk, wx_ref[n]))
    r = jax.nn.sigmoid(jnp.concatenate(ra, axis=1) + ba_ref[...])
    ig = jax.nn.sigmoid(jnp.concatenate(ri, axis=1) + bx_ref[...])
    nl = -lam_ref[...]
    softplus = jnp.maximum(nl, 0.0) + jnp.log(1.0 + jnp.exp(-jnp.abs(nl)))
    log_a = -RG_C * r * softplus
    a = jnp.exp(log_a)
    gap = 1.0 - a * a
    u = jnp.where(gap > 0.0, gap * lax.rsqrt(gap), 0.0) * (ig * xc)

    in_group = lax.broadcasted_iota(I32, (tm, RNN_WIDTH), 0) % 8
    for d in (1, 2, 4):
        keep = in_group >= d
        a_sh = jnp.where(keep, pltpu.roll(a, d, 0), 1.0)
        u_sh = jnp.where(keep, pltpu.roll(u, d, 0), 0.0)
        u = a * u_sh + u
        a = a * a_sh
    h_prev = hstate[0:1, :]
    groups = []
    for g in range(tm // 8):
        h_g = u[g * 8:(g + 1) * 8, :] + a[g * 8:(g + 1) * 8, :] * h_prev
        groups.append(h_g)
        h_prev = h_g[7:8, :]
    hs = jnp.concatenate(groups, axis=0)
    hstate[0:1, :] = h_prev

    y = (jax.nn.gelu(gate) * hs).astype(BF16)
    m = _mm(y, wout_ref[...])
    out_ref[...] = _layer_norm(DN_ALPHA * h + m, g_ref[...], b_ref[...])


def _l1_mixer(h, win, cw, cb, wa, ba, wx, bx, lam, wout, g, b):
    lp, d = h.shape
    tm = _tile_rows(lp)
    row = lambda w: pl.BlockSpec((tm, w), lambda i: (i, 0))
    full = lambda a: pl.BlockSpec(a.shape, lambda i: (0,) * a.ndim)
    args = (h, win, cw, cb, wa, ba, wx, bx, lam, wout, g, b)
    return pl.pallas_call(
        _l1_kernel,
        grid=(lp // tm,),
        in_specs=[row(d)] + [full(a) for a in args[1:]],
        out_specs=row(d),
        out_shape=jax.ShapeDtypeStruct((lp, d), F32),
        scratch_shapes=[pltpu.VMEM((RNN_HALO + tm, RNN_WIDTH), F32), pltpu.VMEM((8, RNN_WIDTH), F32)],
        compiler_params=_params(("arbitrary",), 56),
        name="l1_mixer",
    )(*args)


def _router_kernel(h_ref, whi_ref, wlo_ref, br_ref, ltri_ref, eid_ref, gate_ref, rank_ref, cnt_ref, carry_ref,
                   *, n_real):
    i = pl.program_id(0)
    tm = h_ref.shape[0]

    @pl.when(i == 0)
    def _():
        carry_ref[...] = jnp.zeros(carry_ref.shape, F32)

    h = h_ref[...]
    h_hi = h.astype(BF16)
    h_lo = (h - h_hi.astype(F32)).astype(BF16)
    logits = _mm(h_hi, whi_ref[...]) + _mm(h_lo, whi_ref[...]) + _mm(h_hi, wlo_ref[...]) + br_ref[...]
    lane = lax.broadcasted_iota(I32, (tm, LANE), 1).astype(F32)
    ninf = -jnp.inf
    big = float(LANE)

    gl = jnp.where(lane < N_GROUPS, logits[:, :LANE], ninf)
    gmax = jnp.max(gl, axis=1, keepdims=True)
    g_p = 1.0 / jnp.sum(jnp.exp(gl - gmax), axis=1, keepdims=True)
    g_idx = jnp.min(jnp.where(gl == gmax, lane, big), axis=1, keepdims=True)

    first = g_idx * EXPERTS_PER_GROUP
    el = logits[:, LANE:]
    m1 = jnp.where(lane >= first, jnp.where(lane < first + EXPERTS_PER_GROUP, el, ninf), ninf)
    t1 = jnp.max(m1, axis=1, keepdims=True)
    i1 = jnp.min(jnp.where(m1 == t1, lane, big), axis=1, keepdims=True)
    m2 = jnp.where(lane == i1, ninf, m1)
    t2 = jnp.max(m2, axis=1, keepdims=True)
    i2 = jnp.min(jnp.where(m2 == t2, lane, big), axis=1, keepdims=True)
    e2 = jnp.exp(t2 - t1)
    den = 1.0 / (1.0 + e2)

    tok = i * tm + lax.broadcasted_iota(I32, (tm, LANE), 0)
    valid = tok < n_real
    oh0 = jnp.where(valid, jnp.where(lane == i1, 1.0, 0.0), 0.0)
    oh1 = jnp.where(valid, jnp.where(lane == i2, 1.0, 0.0), 0.0)
    ohs = oh0 + oh1
    before = _mm(ltri_ref[...], ohs.astype(BF16)) + carry_ref[...]
    carry_ref[...] = carry_ref[...] + jnp.sum(ohs, axis=0, keepdims=True)
    cnt_ref[...] = carry_ref[...]

    eid_ref[:, 0:1] = i1.astype(I32)
    eid_ref[:, 1:2] = i2.astype(I32)
    gate_ref[:, 0:1] = g_p * den
    gate_ref[:, 1:2] = g_p * e2 * den
    rank_ref[:, 0:1] = jnp.sum(oh0 * before, axis=1, keepdims=True).astype(I32)
    rank_ref[:, 1:2] = jnp.sum(oh1 * before, axis=1, keepdims=True).astype(I32)


def _router(h, wr, br, n_real):
    lp, d = h.shape
    tm = _tile_rows(lp)
    w_hi = wr.astype(BF16)
    w_lo = (wr - w_hi.astype(F32)).astype(BF16)
    ltri = jnp.asarray((np.arange(tm)[:, None] > np.arange(tm)[None, :]).astype(np.float32), BF16)
    row = lambda w: pl.BlockSpec((tm, w), lambda i: (i, 0))
    full = lambda a: pl.BlockSpec(a.shape, lambda i: (0,) * a.ndim)
    return pl.pallas_call(
        functools.partial(_router_kernel, n_real=n_real),
        grid=(lp // tm,),
        in_specs=[row(d), full(w_hi), full(w_lo), full(br), full(ltri)],
        out_specs=[row(2), row(2), row(2), pl.BlockSpec((1, LANE), lambda i: (0, 0))],
        out_shape=[jax.ShapeDtypeStruct((lp, 2), I32), jax.ShapeDtypeStruct((lp, 2), F32),
                   jax.ShapeDtypeStruct((lp, 2), I32), jax.ShapeDtypeStruct((1, LANE), F32)],
        scratch_shapes=[pltpu.VMEM((1, LANE), F32)],
        compiler_params=_params(("arbitrary",), 32),
        name="moe_router",
    )(h, w_hi, w_lo, br, ltri)


def _row_copy(src_ref, src_row, dst_ref, dst_row, sem):
    return pltpu.make_async_copy(src_ref.at[pl.ds(src_row, 1), :], dst_ref.at[pl.ds(dst_row, 1), :], sem)


def _dispatch_kernel(dest_ref, h_ref, xb_in_ref, xb_ref, packed, sem):
    del xb_in_ref
    i = pl.program_id(0)
    tm = h_ref.shape[0]
    packed[...] = _pack_bf16_pairs(h_ref[...])

    def issue(r8, carry):
        for u in range(ROW_DMA_UNROLL):
            r = r8 * ROW_DMA_UNROLL + u
            for s in range(2):
                _row_copy(packed, r, xb_ref, dest_ref[2 * (i * tm + r) + s], sem).start()
        return carry

    lax.fori_loop(0, tm // ROW_DMA_UNROLL, issue, 0)
    for s in range(2):
        pltpu.make_async_copy(packed, xb_ref.at[pl.ds(0, tm), :], sem).wait()


def _dispatch(dest_flat, h, n_rows_out):
    lp, d = h.shape
    tm = _tile_rows(lp)
    xb0 = jnp.zeros((n_rows_out, d // 2), I32)
    grid_spec = pltpu.PrefetchScalarGridSpec(
        num_scalar_prefetch=1,
        grid=(lp // tm,),
        in_specs=[pl.BlockSpec((tm, d), lambda i, dest: (i, 0)), pl.BlockSpec(memory_space=pl.ANY)],
        out_specs=pl.BlockSpec(memory_space=pl.ANY),
        scratch_shapes=[pltpu.VMEM((tm, d // 2), I32), pltpu.SemaphoreType.DMA(())],
    )
    return pl.pallas_call(
        _dispatch_kernel,
        grid_spec=grid_spec,
        out_shape=jax.ShapeDtypeStruct((n_rows_out, d // 2), I32),
        input_output_aliases={2: 0},
        compiler_params=pltpu.CompilerParams(dimension_semantics=("arbitrary",), has_side_effects=True),
        name="moe_dispatch",
    )(dest_flat, h, xb0)


def _experts_kernel(bexp_ref, nused_ref, xb_ref, wg_ref, wu_ref, wd_ref, yb_ref, wg_s, wu_s, wd_s):
    b = pl.program_id(0)
    prev = bexp_ref[jnp.maximum(b - 1, 0)]

    @pl.when((b == 0) | (bexp_ref[b] != prev))
    def _():
        wg_s[...] = wg_ref[...].astype(BF16)
        wu_s[...] = wu_ref[...].astype(BF16)
        wd_s[...] = wd_ref[...].astype(BF16)

    @pl.when(b < nused_ref[0])
    def _():
        x = _unpack_bf16_pairs(xb_ref[...]).astype(BF16)
        gt = _mm(x, wg_s[...])
        up = _mm(x, wu_s[...])
        mid = (gt * jax.nn.sigmoid(gt) * up).astype(BF16)
        yb_ref[...] = _pack_bf16_pairs(_mm(mid, wd_s[...]))

    @pl.when(b >= nused_ref[0])
    def _():
        yb_ref[...] = jnp.zeros(yb_ref.shape, I32)


def _experts(bexp, nused, xb, wg, wu, wd, layer, n_blocks):
    d = xb.shape[1]
    bm = EXPERT_ROWS
    wspec = lambda a: pl.BlockSpec((None, None) + a.shape[2:], lambda b, bexp, *_: (layer, bexp[b], 0, 0))
    grid_spec = pltpu.PrefetchScalarGridSpec(
        num_scalar_prefetch=2,
        grid=(n_blocks,),
        in_specs=[pl.BlockSpec((bm, d), lambda b, *_: (b, 0)), wspec(wg), wspec(wu), wspec(wd)],
        out_specs=pl.BlockSpec((bm, d), lambda b, *_: (b, 0)),
        scratch_shapes=[pltpu.VMEM(wg.shape[2:], BF16), pltpu.VMEM(wu.shape[2:], BF16),
                        pltpu.VMEM(wd.shape[2:], BF16)],
    )
    return pl.pallas_call(
        _experts_kernel,
        grid_spec=grid_spec,
        out_shape=jax.ShapeDtypeStruct((n_blocks * bm, d), I32),
        compiler_params=_params(("arbitrary",), 48),
        name="moe_experts",
    )(bexp, nused, xb, wg, wu, wd)


def _combine_kernel(src_ref, yb_ref, gate_ref, h_ref, g_ref, b_ref, out_ref, ybuf, sem):
    i = pl.program_id(0)
    tm = h_ref.shape[0]

    def gather_tile(tile, half):
        def issue(r8, carry):
            for u in range(ROW_DMA_UNROLL):
                r = r8 * ROW_DMA_UNROLL + u
                for s in range(2):
                    _row_copy(yb_ref, src_ref[2 * (tile * tm + r) + s], ybuf.at[half, s], r, sem.at[half]).start()
            return carry

        lax.fori_loop(0, tm // ROW_DMA_UNROLL, issue, 0)

    @pl.when(i == 0)
    def _():
        gather_tile(0, 0)

    @pl.when(i + 1 < pl.num_programs(0))
    def _():
        gather_tile(i + 1, (i + 1) % 2)

    half = i % 2
    for s in range(2):
        pltpu.make_async_copy(yb_ref.at[pl.ds(0, tm), :], ybuf.at[half, s], sem.at[half]).wait()
    gate = gate_ref[...]
    y = gate[:, 0:1] * _unpack_bf16_pairs(ybuf[half, 0]) + gate[:, 1:2] * _unpack_bf16_pairs(ybuf[half, 1])
    out_ref[...] = _layer_norm(DN_ALPHA * h_ref[...] + y, g_ref[...], b_ref[...])


def _combine(dest_flat, yb, gate, h, g, b):
    lp, d = h.shape
    tm = _tile_rows(lp)
    full = lambda a: pl.BlockSpec(a.shape, lambda i, dest: (0,) * a.ndim)
    grid_spec = pltpu.PrefetchScalarGridSpec(
        num_scalar_prefetch=1,
        grid=(lp // tm,),
        in_specs=[pl.BlockSpec(memory_space=pl.ANY), pl.BlockSpec((tm, 2), lambda i, dest: (i, 0)),
                  pl.BlockSpec((tm, d), lambda i, dest: (i, 0)), full(g), full(b)],
        out_specs=pl.BlockSpec((tm, d), lambda i, dest: (i, 0)),
        scratch_shapes=[pltpu.VMEM((2, 2, tm, d // 2), I32), pltpu.SemaphoreType.DMA((2,))],
    )
    return pl.pallas_call(
        _combine_kernel,
        grid_spec=grid_spec,
        out_shape=jax.ShapeDtypeStruct((lp, d), F32),
        compiler_params=_params(("arbitrary",), 32),
        name="moe_combine",
    )(dest_flat, yb, gate, h, g, b)


def _moe(h, n_real, layer, wg, bg, we, be, w_gate, w_up, w_down, ln_g, ln_b):
    lp, d = h.shape
    bm = EXPERT_ROWS
    wr = jnp.zeros((d, 2 * LANE), F32).at[:, :N_GROUPS].set(wg).at[:, LANE:LANE + N_EXPERTS].set(we)
    br = jnp.zeros((1, 2 * LANE), F32).at[0, :N_GROUPS].set(bg).at[0, LANE:LANE + N_EXPERTS].set(be)
    eid, gate, rank, cnt = _router(h, wr, br, n_real)

    counts = cnt[0, :N_EXPERTS].astype(I32)
    padded = (counts + bm - 1) // bm * bm
    pend = jnp.cumsum(padded)
    pstart = pend - padded
    n_blocks = -(-(2 * n_real + N_EXPERTS * (bm - 1)) // bm)
    cap = n_blocks * bm
    tok = jnp.arange(lp, dtype=I32)[:, None]
    valid = tok < n_real
    experts = jnp.arange(N_EXPERTS, dtype=I32)
    row = jnp.sum(jnp.where(eid[:, :, None] == experts, pstart, 0), axis=-1) + rank
    dest_flat = jnp.where(valid, row, cap + 2 * (tok - n_real) + jnp.arange(2, dtype=I32)[None, :]).reshape(-1)
    src_flat = jnp.where(valid, row, 0).reshape(-1)
    block_start = jnp.arange(n_blocks, dtype=I32) * bm
    bexp = jnp.minimum(jnp.sum((pend[None, :] <= block_start[:, None]).astype(I32), axis=1), N_EXPERTS - 1)
    nused = (pend[-1:] // bm).astype(I32)

    xb = _dispatch(dest_flat, h, cap + 2 * (lp - n_real))
    yb = _experts(bexp, nused, xb, w_gate, w_up, w_down, layer, n_blocks)
    return _combine(src_flat, yb, gate, h, ln_g, ln_b)


def kernel(x, meta_tokens, ab_w_in, ab_conv_w, ab_conv_b, ab_ln_g, ab_ln_b, ab_w_out, c_w_in, c_conv_w, c_conv_b, c_gate_a_w, c_gate_a_b, c_gate_x_w, c_gate_x_b, c_lambda, c_w_out, moe_router_group_w, moe_router_group_b, moe_router_expert_w, moe_router_expert_b, moe_w_gate, moe_w_up, moe_w_down, ln_mix_g, ln_mix_b, ln_ffn_g, ln_ffn_b):
    bsz, seq, d = x.shape
    assert bsz == 1, "kernel is written for batch 1"
    n_real = N_META + seq
    lp = -(-n_real // Q_BLOCK) * Q_BLOCK
    lk = -(-lp // KEY_CHUNK) * KEY_CHUNK
    topk = min(TOPK_MAX, seq // 4)
    row2 = lambda a: a.reshape(1, -1)

    h = jnp.concatenate([meta_tokens.astype(x.dtype), x[0], jnp.zeros((lp - n_real, d), x.dtype)], axis=0)

    half = HEAD_DIM // 2
    inv_freq = (np.float32(ROPE_THETA) ** (np.float32(-2.0) * np.arange(half, dtype=np.float32)
                                           / np.float32(HEAD_DIM))).astype(np.float32)
    ang = (np.arange(lp, dtype=np.float32)[:, None] * inv_freq[None, :]).astype(np.float64)
    cos = jnp.tile(jnp.asarray(np.cos(ang), F32), (1, 4))
    sin_half = jnp.asarray(np.sin(ang), F32)
    sin = jnp.tile(jnp.concatenate([-sin_half, sin_half], axis=1), (1, 2))

    for layer in range(DEPTH):
        j = layer // 2
        if layer % 2 == 0:
            w_in = ab_w_in[j]
            wglu = w_in[:, :2 * CONV_CH].astype(BF16)
            wqkv = w_in[:, 2 * CONV_CH:2 * CONV_CH + 4 * ATT_WIDTH].astype(BF16)
            wsm = jnp.zeros((d, LANE), F32).at[:, :IDX_DIM + IDX_HEADS].set(
                w_in[:, 2 * CONV_CH + 4 * ATT_WIDTH:]).astype(BF16)
            a2, q, k, v, qi, ki, wi = _l0_in(h, wglu, wqkv, wsm, cos, sin, ab_conv_w[j], row2(ab_conv_b[j]),
                                             row2(ab_ln_g[j]), row2(ab_ln_b[j]))
            pad = lambda t: jnp.pad(t, ((0, lk - lp), (0, 0)))
            o = _dsa(q, qi, wi, pad(k), pad(v), pad(ki), topk)
            w_out = ab_w_out[j].astype(BF16)
            h = _l0_out(a2, o, w_out[:CONV_CH], w_out[CONV_CH:], h, row2(ln_mix_g[layer]), row2(ln_mix_b[layer]))
        else:
            h = _l1_mixer(h, c_w_in[j].astype(BF16), c_conv_w[j], row2(c_conv_b[j]),
                          c_gate_a_w[j].astype(BF16), row2(c_gate_a_b[j]),
                          c_gate_x_w[j].astype(BF16), row2(c_gate_x_b[j]), row2(c_lambda[j]),
                          c_w_out[j].astype(BF16), row2(ln_mix_g[layer]), row2(ln_mix_b[layer]))
        h = _moe(h, n_real, layer, moe_router_group_w[layer], moe_router_group_b[layer],
                 moe_router_expert_w[layer], moe_router_expert_b[layer], moe_w_gate, moe_w_up, moe_w_down,
                 row2(ln_ffn_g[layer]), row2(ln_ffn_b[layer]))
    return h[N_META:n_real][None]
```

```python
import functools

import jax
import jax.numpy as jnp
import numpy as np
from jax import lax
from jax.experimental import pallas as pl
from jax.experimental.pallas import tpu as pltpu

F32 = jnp.float32
BF16 = jnp.bfloat16
I32 = jnp.int32

N_META = 16
CONV_CH = 512
CONV_K = 31
ATT_HEADS = 8
HEAD_DIM = 64
ATT_WIDTH = ATT_HEADS * HEAD_DIM
IDX_HEADS = 8
IDX_DIM = 64
TOPK_MAX = 256
ROPE_THETA = 10000.0
RNN_WIDTH = 1280
RNN_BLOCKS = 10
RNN_BLOCK_W = RNN_WIDTH // RNN_BLOCKS
RNN_CONV_K = 4
RG_C = 8.0
N_GROUPS = 4
EXPERTS_PER_GROUP = 8
N_EXPERTS = N_GROUPS * EXPERTS_PER_GROUP
D_EXPERT = 512
LN_EPS = 1e-5
DEPTH = 2
DN_ALPHA = (2 * DEPTH) ** 0.25

LANE = 128
VMEM_BYTES = 64 << 20

Q_BLOCK = 128
KEY_CHUNK = 512
LIST_DEPTH = 12
CONV_HALO = 32
RNN_HALO = 8
EXPERT_ROWS = 256
ROW_DMA_UNROLL = 8
FLT_MAX = 3.4028234663852886e38
MIN_NORMAL_KEY = 1 << 23
NEG_MASK = -2e30
M_INIT = -1e30
LOG2_E = 1.4426950408889634

_NT = (((1,), (1,)), ((), ()))


def _tile_rows(n):
    for t in (512, 384, 256, 128):
        if n % t == 0:
            return t
    raise ValueError(n)


def _mm(a, b):
    return jnp.dot(a, b, preferred_element_type=F32)


def _layer_norm(x, g, b):
    mu = jnp.mean(x, axis=-1, keepdims=True)
    xc = x - mu
    var = jnp.mean(xc * xc, axis=-1, keepdims=True)
    return xc * lax.rsqrt(var + LN_EPS) * g + b


def _pack_bf16_pairs(x):
    n = x.shape[1] // 2
    bits = lax.bitcast_convert_type(x.astype(BF16).astype(F32), I32)
    return bits[:, :n] | lax.shift_right_logical(bits[:, n:], 16)


def _unpack_bf16_pairs(u):
    first = lax.bitcast_convert_type(u & jnp.int32(-65536), F32)
    second = lax.bitcast_convert_type(lax.shift_left(u, 16), F32)
    return jnp.concatenate([first, second], axis=1)


def _rope_group(t, cos, sin_signed, first_half):
    partner = jnp.where(first_half, pltpu.roll(t, LANE - 32, 1), pltpu.roll(t, 32, 1))
    return t * cos + partner * sin_signed


def _params(sem, vmem_mb):
    assert (vmem_mb << 20) < VMEM_BYTES
    return pltpu.CompilerParams(dimension_semantics=sem, vmem_limit_bytes=vmem_mb << 20)


def _l0_in_kernel(h_ref, wglu_ref, wqkv_ref, wsm_ref, cos_ref, sin_ref, cw_ref, cb_ref, lg_ref, lb_ref,
                  a2_ref, q_ref, k_ref, v_ref, qi_ref, ki_ref, wi_ref, abuf, shift_buf):
    i = pl.program_id(0)
    tm = h_ref.shape[0]
    hb = h_ref[...].astype(BF16)

    glu = _mm(hb, wglu_ref[...])
    a = glu[:, :CONV_CH] * jax.nn.sigmoid(glu[:, CONV_CH:])

    @pl.when(i == 0)
    def _():
        abuf[0:CONV_HALO, :] = jnp.zeros((CONV_HALO, CONV_CH), F32)

    abuf[pl.ds(CONV_HALO, tm), :] = a
    acc = jnp.broadcast_to(cb_ref[...], (tm, CONV_CH))
    first_off = CONV_HALO - (CONV_K - 1)
    for r in range(8):
        taps = [j for j in range(CONV_K) if (first_off + j) % 8 == r]
        n_rows = tm if r == 0 else tm + 8
        part = None
        for j in taps:
            base = (first_off + j) - r
            term = cw_ref[j:j + 1, :] * abuf[pl.ds(base, n_rows), :]
            part = term if part is None else part + term
        if r == 0:
            acc = acc + part
        else:
            shift_buf[...] = part
            acc = acc + shift_buf[pl.ds(r, tm), :]
    abuf[0:CONV_HALO, :] = abuf[pl.ds(tm, CONV_HALO), :]
    y = _layer_norm(acc, lg_ref[...], lb_ref[...])
    a2_ref[...] = (y * jax.nn.sigmoid(y)).astype(BF16)

    cos = cos_ref[...]
    sin = sin_ref[...]
    lane = lax.broadcasted_iota(I32, (tm, LANE), 1)
    first_half = (lane % HEAD_DIM) < (HEAD_DIM // 2)
    qkv = _mm(hb, wqkv_ref[...])
    for g in range(ATT_WIDTH // LANE):
        sl = slice(g * LANE, (g + 1) * LANE)
        qg = qkv[:, g * LANE:(g + 1) * LANE]
        kg = qkv[:, ATT_WIDTH + g * LANE:ATT_WIDTH + (g + 1) * LANE]
        ig = qkv[:, 3 * ATT_WIDTH + g * LANE:3 * ATT_WIDTH + (g + 1) * LANE]
        q_ref[:, sl] = (_rope_group(qg, cos, sin, first_half) * (LOG2_E * HEAD_DIM ** -0.5)).astype(BF16)
        k_ref[:, sl] = _rope_group(kg, cos, sin, first_half).astype(BF16)
        qi_ref[:, sl] = (_rope_group(ig, cos, sin, first_half) * (IDX_DIM ** -0.5)).astype(BF16)
    v_ref[...] = qkv[:, 2 * ATT_WIDTH:3 * ATT_WIDTH].astype(BF16)

    sm = _mm(hb, wsm_ref[...])
    ki = _rope_group(sm, cos, sin, first_half)
    ki_ref[...] = jnp.where(lane < IDX_DIM, ki, 0.0).astype(BF16)
    wi_ref[...] = sm * (IDX_HEADS ** -0.5)


def _l0_in(h, wglu, wqkv, wsm, cos, sin, cw, cb, lg, lb):
    lp, d = h.shape
    tm = _tile_rows(lp)
    row = lambda w: pl.BlockSpec((tm, w), lambda i: (i, 0))
    full = lambda a: pl.BlockSpec(a.shape, lambda i: (0,) * a.ndim)
    outs = [jax.ShapeDtypeStruct((lp, CONV_CH), BF16)] + [jax.ShapeDtypeStruct((lp, ATT_WIDTH), BF16)] * 4 + [
        jax.ShapeDtypeStruct((lp, LANE), BF16), jax.ShapeDtypeStruct((lp, LANE), F32)]
    return pl.pallas_call(
        _l0_in_kernel,
        grid=(lp // tm,),
        in_specs=[row(d), full(wglu), full(wqkv), full(wsm), row(LANE), row(LANE), full(cw), full(cb), full(lg),
                  full(lb)],
        out_specs=[row(CONV_CH)] + [row(ATT_WIDTH)] * 4 + [row(LANE), row(LANE)],
        out_shape=outs,
        scratch_shapes=[pltpu.VMEM((CONV_HALO + tm, CONV_CH), F32), pltpu.VMEM((tm + 8, CONV_CH), F32)],
        compiler_params=_params(("arbitrary",), 48),
        name="l0_in",
    )(h, wglu, wqkv, wsm, cos, sin, cw, cb, lg, lb)


def _two_stage_chunks(nch, first, second):
    first(0, 0)

    def body(t, carry):
        c = 2 * t
        first(c + 1, 1)
        second(c, 0, False)
        first(c + 2, 0)
        second(c + 1, 1, False)
        return carry

    n_pairs = (nch - 1) // 2
    lax.fori_loop(0, n_pairs, body, 0)
    c0 = 2 * n_pairs

    @pl.when(nch - c0 == 2)
    def _():
        first(c0 + 1, 1)
        second(c0, 0, False)
        second(c0 + 1, 1, True)

    @pl.when(nch - c0 == 1)
    def _():
        second(c0, 0, True)


def _dsa_kernel(q_ref, qi_ref, wi_ref, k_ref, v_ref, ki_ref, u_ref, o_ref,
                s_ref, qi8_ref, qh_ref, wb_ref, m_ref, acc_ref, pbuf, lohi_ref, cand_ref, thr_ref, nlo_ref,
                *, topk):
    i = pl.program_id(0)
    bq = q_ref.shape[0]
    ck = KEY_CHUNK
    ngrp = ck // LANE
    nch = ((i + 1) * bq + ck - 1) // ck
    lane = lax.broadcasted_iota(I32, (bq, LANE), 1)
    low = lane < HEAD_DIM
    chunk_rows = lambda c: pl.ds(pl.multiple_of(c * ck, ck), ck)

    for g in range(ATT_WIDTH // LANE):
        pair = qi_ref[:, g * LANE:(g + 1) * LANE].astype(F32)
        qi8_ref[pl.ds((2 * g) * bq, bq), :] = jnp.where(low, pair, 0.0).astype(BF16)
        qi8_ref[pl.ds((2 * g + 1) * bq, bq), :] = jnp.where(low, pltpu.roll(pair, HEAD_DIM, 1), 0.0).astype(BF16)
        qp = q_ref[:, g * LANE:(g + 1) * LANE].astype(F32)
        qh_ref[g, 0:bq, :] = jnp.where(low, qp, 0.0).astype(BF16)
        qh_ref[g, bq:2 * bq, :] = jnp.where(low, 0.0, qp).astype(BF16)
    wi = wi_ref[...]
    for h in range(IDX_HEADS):
        wb_ref[h] = jnp.broadcast_to(wi[:, IDX_DIM + h:IDX_DIM + h + 1], (bq, LANE))
    lohi_ref[0] = jnp.full((bq, LANE), jnp.inf, F32)
    lohi_ref[1] = jnp.full((bq, LANE), -jnp.inf, F32)

    def to_key(x):
        bits = lax.bitcast_convert_type(x, I32)
        return bits ^ ((bits >> 31) & jnp.int32(0x7FFFFFFF))

    def from_key(k):
        k = jnp.where(k > 0, jnp.where(k < MIN_NORMAL_KEY, MIN_NORMAL_KEY, k),
                      jnp.where(k >= -MIN_NORMAL_KEY, 0, k))
        return lax.bitcast_convert_type(k ^ ((k >> 31) & jnp.int32(0x7FFFFFFF)), F32)

    def index_logits(c, slot):
        pbuf[slot] = lax.dot_general(qi8_ref[...], ki_ref[chunk_rows(c), :], _NT, preferred_element_type=F32)

    cand_ref[...] = jnp.full(cand_ref.shape, -jnp.inf, F32)
    lane8 = lax.broadcasted_iota(I32, (8, LANE), 1)
    row8 = lax.broadcasted_iota(I32, (8, LANE), 0)

    def index_scores(c, slot, last):
        for slab in range(bq // 8):
            rows = slice(slab * 8, (slab + 1) * 8)
            smin = lohi_ref[0, rows, :]
            smax = lohi_ref[1, rows, :]
            lst = [cand_ref[d, rows, :] for d in range(LIST_DEPTH)]
            for j in range(ngrp):
                cols = slice(j * LANE, (j + 1) * LANE)
                sc = jnp.zeros((8, LANE), F32)
                for h in range(IDX_HEADS):
                    hrows = slice(h * bq + slab * 8, h * bq + (slab + 1) * 8)
                    sc = sc + jnp.maximum(pbuf[slot, hrows, cols], 0.0) * wb_ref[h, rows, :]
                if last:
                    valid = (c * ck + j * LANE + lane8) <= (i * bq + slab * 8 + row8)
                    smin = jnp.minimum(smin, jnp.where(valid, sc, jnp.inf))
                    sc = jnp.where(valid, sc, -jnp.inf)
                else:
                    smin = jnp.minimum(smin, sc)
                smax = jnp.maximum(smax, sc)
                s_ref[c, rows, cols] = sc
                x = sc
                for d in range(LIST_DEPTH):
                    top = jnp.maximum(lst[d], x)
                    x = jnp.minimum(lst[d], x)
                    lst[d] = top
            for d in range(LIST_DEPTH):
                cand_ref[d, rows, :] = lst[d]
            lohi_ref[0, rows, :] = smin
            lohi_ref[1, rows, :] = smax

    _two_stage_chunks(nch, index_logits, index_scores)

    def count_ge(cand):
        cand_f = from_key(cand)

        def body(c, cnt):
            blk = s_ref[c]
            for g in range(ngrp):
                cnt = cnt + jnp.where(blk[:, g * LANE:(g + 1) * LANE] >= cand_f, 1.0, 0.0)
            return cnt
        cnt = lax.fori_loop(0, nch, body, jnp.zeros((bq, LANE), F32))
        return jnp.sum(cnt, axis=1, keepdims=True)

    kf = float(topk)
    zeros = jnp.zeros((bq, LANE), F32)
    n_valid = (i * bq + lax.broadcasted_iota(I32, (bq, LANE), 0) + 1).astype(F32)
    lo0 = to_key(jnp.min(lohi_ref[0], axis=1, keepdims=True) + zeros)
    hi0 = to_key(jnp.max(lohi_ref[1], axis=1, keepdims=True) + zeros) + 1

    def open_rows(lo, hi, n_lo):
        return jnp.where(n_lo > kf, jnp.where(hi > lo + 1, 1.0, 0.0), 0.0)

    def bisect(count_fn, steps, lo_init, n_init):
        def cond(st):
            it, _, _, _, any_open = st
            return jnp.logical_and(it < 34, any_open)

        def body(st):
            it, lo, hi, n_lo, _ = st
            for _ in range(steps):
                is_open = open_rows(lo, hi, n_lo) > 0.0
                mid = (lo >> 1) + (hi >> 1) + (lo & hi & 1)
                cand = jnp.where(is_open, mid, lo)
                n = count_fn(cand) + zeros
                ge = n >= kf
                lo, hi, n_lo = (jnp.where(is_open, jnp.where(ge, cand, lo), lo),
                                jnp.where(is_open, jnp.where(ge, hi, cand), hi),
                                jnp.where(is_open, jnp.where(ge, n, n_lo), n_lo))
            return it + steps, lo, hi, n_lo, jnp.max(open_rows(lo, hi, n_lo)) > 0.0

        st = lax.while_loop(cond, body, (jnp.int32(0), lo_init, hi0, n_init,
                                         jnp.max(open_rows(lo_init, hi0, n_init)) > 0.0))
        return st[1], st[3]

    def count_ge_lists(cand):
        cand_f = from_key(cand)
        cnt = jnp.zeros((bq, LANE), F32)
        for d in range(LIST_DEPTH):
            cnt = cnt + jnp.where(cand_ref[d] >= cand_f, 1.0, 0.0)
        return jnp.sum(cnt, axis=1, keepdims=True)

    runner_up = jnp.min(cand_ref[1], axis=1, keepdims=True) + zeros
    have_two = runner_up > -jnp.inf
    lo_start = jnp.where(have_two, to_key(runner_up), lo0)
    n_start = jnp.where(have_two, count_ge_lists(lo_start) + zeros, n_valid)
    lo_l, n_l = bisect(count_ge_lists, 2, lo_start, n_start)
    thr_ref[...] = lo_l
    nlo_ref[...] = n_l
    deepest = jnp.where(cand_ref[LIST_DEPTH - 1] >= from_key(lo_l), 1.0, 0.0)
    lists_short = jnp.max(jnp.where(n_valid > kf, deepest, 0.0)) > 0.0

    @pl.when(lists_short)
    def _():
        lo_a, n_a = bisect(count_ge, 1, lo0, n_valid)
        thr_ref[...] = lo_a
        nlo_ref[...] = n_a

    n_lo = nlo_ref[...]
    take_all = n_valid <= kf
    thr_key = thr_ref[...]
    thr = jnp.where(take_all, -FLT_MAX, from_key(thr_key))

    m_ref[...] = jnp.full(m_ref.shape, M_INIT, F32)
    acc_ref[...] = jnp.zeros(acc_ref.shape, F32)
    ones = jnp.ones((ck, LANE), BF16)
    pair_rows = lambda g: slice(g * 2 * bq, (g + 1) * 2 * bq)

    def masked_scores(c, slot, bias):
        bias2 = jnp.concatenate([jnp.concatenate(bias, axis=1)] * 2, axis=0)
        for g in range(ATT_WIDTH // LANE):
            kc = k_ref[chunk_rows(c), g * LANE:(g + 1) * LANE]
            pbuf[slot, pair_rows(g), :] = lax.dot_general(qh_ref[g], kc, _NT, preferred_element_type=F32) + bias2

    def softmax_pv(c, slot, last):
        del last
        for g in range(ATT_WIDTH // LANE):
            sj = [pbuf[slot, pair_rows(g), j * LANE:(j + 1) * LANE] for j in range(ngrp)]
            mx = sj[0]
            for j in range(1, ngrp):
                mx = jnp.maximum(mx, sj[j])
            m_prev = m_ref[g]
            m_new = jnp.maximum(m_prev, jnp.max(mx, axis=1, keepdims=True))
            alpha = jnp.exp2(m_prev - m_new)
            p2 = jnp.concatenate([jnp.exp2(s - m_new) for s in sj], axis=1).astype(BF16)
            v_aug = jnp.concatenate([v_ref[chunk_rows(c), g * LANE:(g + 1) * LANE], ones], axis=1)
            acc_ref[g] = jnp.concatenate([alpha, alpha], axis=1) * acc_ref[g] + _mm(p2, v_aug)
            m_ref[g] = m_new

    has_ties = jnp.max(n_lo) > kf

    @pl.when(jnp.logical_not(has_ties))
    def _():
        def scores(c, slot):
            masked_scores(c, slot, [jnp.where(s_ref[c, :, j * LANE:(j + 1) * LANE] >= thr, 0.0, NEG_MASK)
                                    for j in range(ngrp)])

        _two_stage_chunks(nch, scores, softmax_pv)

    @pl.when(has_ties)
    def _():
        n_above = count_ge(thr_key + 1)
        need = jnp.where(take_all[:, 0:1], 0.0, kf - n_above)
        thr_b = jnp.broadcast_to(thr[:, 0:1], (bq, ck))
        need_b = jnp.broadcast_to(need, (bq, ck))

        def attend(c, ties_seen):
            sc = s_ref[c]
            eq = sc == thr_b
            prefix = _mm(jnp.where(eq, 1.0, 0.0).astype(BF16), u_ref[...]) + ties_seen
            take_tie = jnp.where(eq, jnp.where(prefix <= need_b, 0.0, NEG_MASK), NEG_MASK)
            bias = jnp.where(sc > thr_b, 0.0, take_tie)
            masked_scores(c, 0, [bias[:, j * LANE:(j + 1) * LANE] for j in range(ngrp)])
            softmax_pv(c, 0, False)
            return prefix[:, ck - 1:ck]

        lax.fori_loop(0, nch, attend, jnp.zeros((bq, 1), F32))

    for g in range(ATT_WIDTH // LANE):
        acc = acc_ref[g]
        o2 = acc[:, :LANE] * (1.0 / acc[:, LANE:])
        o_ref[:, g * LANE:(g + 1) * LANE] = jnp.where(low, o2[:bq], o2[bq:]).astype(BF16)


def _dsa(q, qi, wi, k, v, ki, topk):
    lp = q.shape[0]
    lk = k.shape[0]
    bq, ck = Q_BLOCK, KEY_CHUNK
    assert topk <= min(ck, 2 * LANE) and lk % ck == 0 and lk >= lp
    u = (np.arange(ck)[:, None] <= np.arange(ck)[None, :]).astype(np.float32)
    u = jnp.asarray(u, BF16)
    row = lambda w: pl.BlockSpec((bq, w), lambda i: (i, 0))
    res = lambda a: pl.BlockSpec(a.shape, lambda i: (0, 0), pipeline_mode=pl.Buffered(1))
    return pl.pallas_call(
        functools.partial(_dsa_kernel, topk=topk),
        grid=(lp // bq,),
        in_specs=[row(ATT_WIDTH), row(ATT_WIDTH), row(LANE), res(k), res(v), res(ki), res(u)],
        out_specs=row(ATT_WIDTH),
        out_shape=jax.ShapeDtypeStruct((lp, ATT_WIDTH), BF16),
        scratch_shapes=[
            pltpu.VMEM((lk // ck, bq, ck), F32),
            pltpu.VMEM((IDX_HEADS * bq, LANE), BF16),
            pltpu.VMEM((ATT_HEADS // 2, 2 * bq, LANE), BF16),
            pltpu.VMEM((IDX_HEADS, bq, LANE), F32),
            pltpu.VMEM((ATT_HEADS // 2, 2 * bq, LANE), F32),
            pltpu.VMEM((ATT_HEADS // 2, 2 * bq, 2 * LANE), F32),
            pltpu.VMEM((2, IDX_HEADS * bq, ck), F32),
            pltpu.VMEM((2, bq, LANE), F32),
            pltpu.VMEM((LIST_DEPTH, bq, LANE), F32),
            pltpu.VMEM((bq, LANE), I32),
            pltpu.VMEM((bq, LANE), F32),
        ],
        compiler_params=_params(("arbitrary",), 60),
        name="dsa",
    )(q, qi, wi, k, v, ki, u)


def _l0_out_kernel(a2_ref, o_ref, wa_ref, wo_ref, h_ref, g_ref, b_ref, out_ref):
    m = _mm(a2_ref[...], wa_ref[...]) + _mm(o_ref[...], wo_ref[...])
    out_ref[...] = _layer_norm(DN_ALPHA * h_ref[...] + m, g_ref[...], b_ref[...])


def _l0_out(a2, o, wa, wo, h, g, b):
    lp, d = h.shape
    tm = _tile_rows(lp)
    row = lambda w: pl.BlockSpec((tm, w), lambda i: (i, 0))
    full = lambda a: pl.BlockSpec(a.shape, lambda i: (0,) * a.ndim)
    return pl.pallas_call(
        _l0_out_kernel,
        grid=(lp // tm,),
        in_specs=[row(CONV_CH), row(ATT_WIDTH), full(wa), full(wo), row(d), full(g), full(b)],
        out_specs=row(d),
        out_shape=jax.ShapeDtypeStruct((lp, d), F32),
        compiler_params=_params(("parallel",), 32),
        name="l0_out",
    )(a2, o, wa, wo, h, g, b)


def _l1_kernel(h_ref, win_ref, cw_ref, cb_ref, wa_ref, ba_ref, wx_ref, bx_ref, lam_ref, wout_ref, g_ref, b_ref,
               out_ref, xbuf, hstate):
    i = pl.program_id(0)
    tm = h_ref.shape[0]
    h = h_ref[...]
    z = _mm(h.astype(BF16), win_ref[...])
    gate = z[:, :RNN_WIDTH]

    @pl.when(i == 0)
    def _():
        xbuf[0:RNN_HALO, :] = jnp.zeros((RNN_HALO, RNN_WIDTH), F32)
        hstate[...] = jnp.zeros(hstate.shape, F32)

    xbuf[pl.ds(RNN_HALO, tm), :] = z[:, RNN_WIDTH:]
    xc = jnp.broadcast_to(cb_ref[...], (tm, RNN_WIDTH))
    for j in range(RNN_CONV_K):
        xc = xc + cw_ref[j:j + 1, :] * xbuf[pl.ds(RNN_HALO - (RNN_CONV_K - 1) + j, tm), :]
    xbuf[0:RNN_HALO, :] = xbuf[pl.ds(tm, RNN_HALO), :]

    xcb = xc.astype(BF16)
    ra, ri = [], []
    for n in range(RNN_BLOCKS):
        blk = xcb[:, n * RNN_BLOCK_W:(n + 1) * RNN_BLOCK_W]
        ra.append(_mm(blk, wa_ref[n]))
        ri.append(_mm(blk, wx_ref[n]))
    r = jax.nn.sigmoid(jnp.concatenate(ra, axis=1) + ba_ref[...])
    ig = jax.nn.sigmoid(jnp.concatenate(ri, axis=1) + bx_ref[...])
    nl = -lam_ref[...]
    softplus = jnp.maximum(nl, 0.0) + jnp.log(1.0 + jnp.exp(-jnp.abs(nl)))
    log_a = -RG_C * r * softplus
    a = jnp.exp(log_a)
    gap = 1.0 - a * a
    u = jnp.where(gap > 0.0, gap * lax.rsqrt(gap), 0.0) * (ig * xc)

    in_group = lax.broadcasted_iota(I32, (tm, RNN_WIDTH), 0) % 8
    for d in (1, 2, 4):
        keep = in_group >= d
        a_sh = jnp.where(keep, pltpu.roll(a, d, 0), 1.0)
        u_sh = jnp.where(keep, pltpu.roll(u, d, 0), 0.0)
        u = a * u_sh + u
        a = a * a_sh
    h_prev = hstate[0:1, :]
    groups = []
    for g in range(tm // 8):
        h_g = u[g * 8:(g + 1) * 8, :] + a[g * 8:(g + 1) * 8, :] * h_prev
        groups.append(h_g)
        h_prev = h_g[7:8, :]
    hs = jnp.concatenate(groups, axis=0)
    hstate[0:1, :] = h_prev

    y = (jax.nn.gelu(gate) * hs).astype(BF16)
    m = _mm(y, wout_ref[...])
    out_ref[...] = _layer_norm(DN_ALPHA * h + m, g_ref[...], b_ref[...])


def _l1_mixer(h, win, cw, cb, wa, ba, wx, bx, lam, wout, g, b):
    lp, d = h.shape
    tm = _tile_rows(lp)
    row = lambda w: pl.BlockSpec((tm, w), lambda i: (i, 0))
    full = lambda a: pl.BlockSpec(a.shape, lambda i: (0,) * a.ndim)
    args = (h, win, cw, cb, wa, ba, wx, bx, lam, wout, g, b)
    return pl.pallas_call(
        _l1_kernel,
        grid=(lp // tm,),
        in_specs=[row(d)] + [full(a) for a in args[1:]],
        out_specs=row(d),
        out_shape=jax.ShapeDtypeStruct((lp, d), F32),
        scratch_shapes=[pltpu.VMEM((RNN_HALO + tm, RNN_WIDTH), F32), pltpu.VMEM((8, RNN_WIDTH), F32)],
        compiler_params=_params(("arbitrary",), 56),
        name="l1_mixer",
    )(*args)


def _router_kernel(h_ref, whi_ref, wlo_ref, br_ref, ltri_ref, eid_ref, gate_ref, rank_ref, cnt_ref, carry_ref,
                   *, n_real):
    i = pl.program_id(0)
    tm = h_ref.shape[0]

    @pl.when(i == 0)
    def _():
        carry_ref[...] = jnp.zeros(carry_ref.shape, F32)

    h = h_ref[...]
    h_hi = h.astype(BF16)
    h_lo = (h - h_hi.astype(F32)).astype(BF16)
    logits = _mm(h_hi, whi_ref[...]) + _mm(h_lo, whi_ref[...]) + _mm(h_hi, wlo_ref[...]) + br_ref[...]
    lane = lax.broadcasted_iota(I32, (tm, LANE), 1).astype(F32)
    ninf = -jnp.inf
    big = float(LANE)

    gl = jnp.where(lane < N_GROUPS, logits[:, :LANE], ninf)
    gmax = jnp.max(gl, axis=1, keepdims=True)
    g_p = 1.0 / jnp.sum(jnp.exp(gl - gmax), axis=1, keepdims=True)
    g_idx = jnp.min(jnp.where(gl == gmax, lane, big), axis=1, keepdims=True)

    first = g_idx * EXPERTS_PER_GROUP
    el = logits[:, LANE:]
    m1 = jnp.where(lane >= first, jnp.where(lane < first + EXPERTS_PER_GROUP, el, ninf), ninf)
    t1 = jnp.max(m1, axis=1, keepdims=True)
    i1 = jnp.min(jnp.where(m1 == t1, lane, big), axis=1, keepdims=True)
    m2 = jnp.where(lane == i1, ninf, m1)
    t2 = jnp.max(m2, axis=1, keepdims=True)
    i2 = jnp.min(jnp.where(m2 == t2, lane, big), axis=1, keepdims=True)
    e2 = jnp.exp(t2 - t1)
    den = 1.0 / (1.0 + e2)

    tok = i * tm + lax.broadcasted_iota(I32, (tm, LANE), 0)
    valid = tok < n_real
    oh0 = jnp.where(valid, jnp.where(lane == i1, 1.0, 0.0), 0.0)
    oh1 = jnp.where(valid, jnp.where(lane == i2, 1.0, 0.0), 0.0)
    ohs = oh0 + oh1
    before = _mm(ltri_ref[...], ohs.astype(BF16)) + carry_ref[...]
    carry_ref[...] = carry_ref[...] + jnp.sum(ohs, axis=0, keepdims=True)
    cnt_ref[...] = carry_ref[...]

    eid_ref[:, 0:1] = i1.astype(I32)
    eid_ref[:, 1:2] = i2.astype(I32)
    gate_ref[:, 0:1] = g_p * den
    gate_ref[:, 1:2] = g_p * e2 * den
    rank_ref[:, 0:1] = jnp.sum(oh0 * before, axis=1, keepdims=True).astype(I32)
    rank_ref[:, 1:2] = jnp.sum(oh1 * before, axis=1, keepdims=True).astype(I32)


def _router(h, wr, br, n_real):
    lp, d = h.shape
    tm = _tile_rows(lp)
    w_hi = wr.astype(BF16)
    w_lo = (wr - w_hi.astype(F32)).astype(BF16)
    ltri = jnp.asarray((np.arange(tm)[:, None] > np.arange(tm)[None, :]).astype(np.float32), BF16)
    row = lambda w: pl.BlockSpec((tm, w), lambda i: (i, 0))
    full = lambda a: pl.BlockSpec(a.shape, lambda i: (0,) * a.ndim)
    return pl.pallas_call(
        functools.partial(_router_kernel, n_real=n_real),
        grid=(lp // tm,),
        in_specs=[row(d), full(w_hi), full(w_lo), full(br), full(ltri)],
        out_specs=[row(2), row(2), row(2), pl.BlockSpec((1, LANE), lambda i: (0, 0))],
        out_shape=[jax.ShapeDtypeStruct((lp, 2), I32), jax.ShapeDtypeStruct((lp, 2), F32),
                   jax.ShapeDtypeStruct((lp, 2), I32), jax.ShapeDtypeStruct((1, LANE), F32)],
        scratch_shapes=[pltpu.VMEM((1, LANE), F32)],
        compiler_params=_params(("arbitrary",), 32),
        name="moe_router",
    )(h, w_hi, w_lo, br, ltri)


def _row_copy(src_ref, src_row, dst_ref, dst_row, sem):
    return pltpu.make_async_copy(src_ref.at[pl.ds(src_row, 1), :], dst_ref.at[pl.ds(dst_row, 1), :], sem)


def _dispatch_kernel(dest_ref, h_ref, xb_in_ref, xb_ref, packed, sem):
    del xb_in_ref
    i = pl.program_id(0)
    tm = h_ref.shape[0]
    half = i % 2
    packed[half] = _pack_bf16_pairs(h_ref[...])

    def issue(r8, carry):
        for u in range(ROW_DMA_UNROLL):
            r = r8 * ROW_DMA_UNROLL + u
            for s in range(2):
                _row_copy(packed.at[half], r, xb_ref, dest_ref[2 * (i * tm + r) + s], sem.at[half]).start()
        return carry

    lax.fori_loop(0, tm // ROW_DMA_UNROLL, issue, 0)

    def drain(which):
        for s in range(2):
            pltpu.make_async_copy(packed.at[which], xb_ref.at[pl.ds(0, tm), :], sem.at[which]).wait()

    @pl.when(i > 0)
    def _():
        drain(1 - half)

    @pl.when(i == pl.num_programs(0) - 1)
    def _():
        drain(half)


def _dispatch(dest_flat, h, n_rows_out):
    lp, d = h.shape
    tm = _tile_rows(lp)
    xb0 = jnp.zeros((n_rows_out, d // 2), I32)
    grid_spec = pltpu.PrefetchScalarGridSpec(
        num_scalar_prefetch=1,
        grid=(lp // tm,),
        in_specs=[pl.BlockSpec((tm, d), lambda i, dest: (i, 0)), pl.BlockSpec(memory_space=pl.ANY)],
        out_specs=pl.BlockSpec(memory_space=pl.ANY),
        scratch_shapes=[pltpu.VMEM((2, tm, d // 2), I32), pltpu.SemaphoreType.DMA((2,))],
    )
    return pl.pallas_call(
        _dispatch_kernel,
        grid_spec=grid_spec,
        out_shape=jax.ShapeDtypeStruct((n_rows_out, d // 2), I32),
        input_output_aliases={2: 0},
        compiler_params=pltpu.CompilerParams(dimension_semantics=("arbitrary",), has_side_effects=True),
        name="moe_dispatch",
    )(dest_flat, h, xb0)


def _experts_kernel(bexp_ref, nused_ref, xb_ref, wg_ref, wu_ref, wd_ref, yb_ref, wg_s, wu_s, wd_s):
    b = pl.program_id(0)
    prev = bexp_ref[jnp.maximum(b - 1, 0)]

    @pl.when((b == 0) | (bexp_ref[b] != prev))
    def _():
        wg_s[...] = wg_ref[...].astype(BF16)
        wu_s[...] = wu_ref[...].astype(BF16)
        wd_s[...] = wd_ref[...].astype(BF16)

    @pl.when(b < nused_ref[0])
    def _():
        x = _unpack_bf16_pairs(xb_ref[...]).astype(BF16)
        gt = _mm(x, wg_s[...])
        up = _mm(x, wu_s[...])
        mid = (gt * jax.nn.sigmoid(gt) * up).astype(BF16)
        yb_ref[...] = _pack_bf16_pairs(_mm(mid, wd_s[...]))

    @pl.when(b >= nused_ref[0])
    def _():
        yb_ref[...] = jnp.zeros(yb_ref.shape, I32)


def _experts(bexp, nused, xb, wg, wu, wd, layer, n_blocks):
    d = xb.shape[1]
    bm = EXPERT_ROWS
    wspec = lambda a: pl.BlockSpec((None, None) + a.shape[2:], lambda b, bexp, *_: (layer, bexp[b], 0, 0))
    grid_spec = pltpu.PrefetchScalarGridSpec(
        num_scalar_prefetch=2,
        grid=(n_blocks,),
        in_specs=[pl.BlockSpec((bm, d), lambda b, *_: (b, 0)), wspec(wg), wspec(wu), wspec(wd)],
        out_specs=pl.BlockSpec((bm, d), lambda b, *_: (b, 0)),
        scratch_shapes=[pltpu.VMEM(wg.shape[2:], BF16), pltpu.VMEM(wu.shape[2:], BF16),
                        pltpu.VMEM(wd.shape[2:], BF16)],
    )
    return pl.pallas_call(
        _experts_kernel,
        grid_spec=grid_spec,
        out_shape=jax.ShapeDtypeStruct((n_blocks * bm, d), I32),
        compiler_params=_params(("arbitrary",), 48),
        name="moe_experts",
    )(bexp, nused, xb, wg, wu, wd)


def _combine_kernel(src_ref, yb_ref, gate_ref, h_ref, g_ref, b_ref, out_ref, ybuf, sem):
    i = pl.program_id(0)
    tm = h_ref.shape[0]

    def gather_tile(tile, half):
        def issue(r8, carry):
            for u in range(ROW_DMA_UNROLL):
                r = r8 * ROW_DMA_UNROLL + u
                for s in range(2):
                    _row_copy(yb_ref, src_ref[2 * (tile * tm + r) + s], ybuf.at[half, s], r, sem.at[half]).start()
            return carry

        lax.fori_loop(0, tm // ROW_DMA_UNROLL, issue, 0)

    @pl.when(i == 0)
    def _():
        gather_tile(0, 0)

    @pl.when(i + 1 < pl.num_programs(0))
    def _():
        gather_tile(i + 1, (i + 1) % 2)

    half = i % 2
    for s in range(2):
        pltpu.make_async_copy(yb_ref.at[pl.ds(0, tm), :], ybuf.at[half, s], sem.at[half]).wait()
    gate = gate_ref[...]
    y = gate[:, 0:1] * _unpack_bf16_pairs(ybuf[half, 0]) + gate[:, 1:2] * _unpack_bf16_pairs(ybuf[half, 1])
    out_ref[...] = _layer_norm(DN_ALPHA * h_ref[...] + y, g_ref[...], b_ref[...])


def _combine(dest_flat, yb, gate, h, g, b):
    lp, d = h.shape
    tm = _tile_rows(lp)
    full = lambda a: pl.BlockSpec(a.shape, lambda i, dest: (0,) * a.ndim)
    grid_spec = pltpu.PrefetchScalarGridSpec(
        num_scalar_prefetch=1,
        grid=(lp // tm,),
        in_specs=[pl.BlockSpec(memory_space=pl.ANY), pl.BlockSpec((tm, 2), lambda i, dest: (i, 0)),
                  pl.BlockSpec((tm, d), lambda i, dest: (i, 0)), full(g), full(b)],
        out_specs=pl.BlockSpec((tm, d), lambda i, dest: (i, 0)),
        scratch_shapes=[pltpu.VMEM((2, 2, tm, d // 2), I32), pltpu.SemaphoreType.DMA((2,))],
    )
    return pl.pallas_call(
        _combine_kernel,
        grid_spec=grid_spec,
        out_shape=jax.ShapeDtypeStruct((lp, d), F32),
        compiler_params=_params(("arbitrary",), 32),
        name="moe_combine",
    )(dest_flat, yb, gate, h, g, b)


def _moe(h, n_real, layer, wg, bg, we, be, w_gate, w_up, w_down, ln_g, ln_b):
    lp, d = h.shape
    bm = EXPERT_ROWS
    wr = jnp.zeros((d, 2 * LANE), F32).at[:, :N_GROUPS].set(wg).at[:, LANE:LANE + N_EXPERTS].set(we)
    br = jnp.zeros((1, 2 * LANE), F32).at[0, :N_GROUPS].set(bg).at[0, LANE:LANE + N_EXPERTS].set(be)
    eid, gate, rank, cnt = _router(h, wr, br, n_real)

    counts = cnt[0, :N_EXPERTS].astype(I32)
    padded = (counts + bm - 1) // bm * bm
    pend = jnp.cumsum(padded)
    pstart = pend - padded
    n_blocks = -(-(2 * n_real + N_EXPERTS * (bm - 1)) // bm)
    cap = n_blocks * bm
    tok = jnp.arange(lp, dtype=I32)[:, None]
    valid = tok < n_real
    experts = jnp.arange(N_EXPERTS, dtype=I32)
    row = jnp.sum(jnp.where(eid[:, :, None] == experts, pstart, 0), axis=-1) + rank
    dest_flat = jnp.where(valid, row, cap + 2 * (tok - n_real) + jnp.arange(2, dtype=I32)[None, :]).reshape(-1)
    src_flat = jnp.where(valid, row, 0).reshape(-1)
    block_start = jnp.arange(n_blocks, dtype=I32) * bm
    bexp = jnp.minimum(jnp.sum((pend[None, :] <= block_start[:, None]).astype(I32), axis=1), N_EXPERTS - 1)
    nused = (pend[-1:] // bm).astype(I32)

    xb = _dispatch(dest_flat, h, cap + 2 * (lp - n_real))
    yb = _experts(bexp, nused, xb, w_gate, w_up, w_down, layer, n_blocks)
    return _combine(src_flat, yb, gate, h, ln_g, ln_b)


def kernel(x, meta_tokens, ab_w_in, ab_conv_w, ab_conv_b, ab_ln_g, ab_ln_b, ab_w_out, c_w_in, c_conv_w, c_conv_b, c_gate_a_w, c_gate_a_b, c_gate_x_w, c_gate_x_b, c_lambda, c_w_out, moe_router_group_w, moe_router_group_b, moe_router_expert_w, moe_router_expert_b, moe_w_gate, moe_w_up, moe_w_down, ln_mix_g, ln_mix_b, ln_ffn_g, ln_ffn_b):
    bsz, seq, d = x.shape
    assert bsz == 1, "kernel is written for batch 1"
    n_real = N_META + seq
    lp = -(-n_real // Q_BLOCK) * Q_BLOCK
    lk = -(-lp // KEY_CHUNK) * KEY_CHUNK
    topk = min(TOPK_MAX, seq // 4)
    row2 = lambda a: a.reshape(1, -1)

    h = jnp.concatenate([meta_tokens.astype(x.dtype), x[0], jnp.zeros((lp - n_real, d), x.dtype)], axis=0)

    half = HEAD_DIM // 2
    inv_freq = (np.float32(ROPE_THETA) ** (np.float32(-2.0) * np.arange(half, dtype=np.float32)
                                           / np.float32(HEAD_DIM))).astype(np.float32)
    ang = (np.arange(lp, dtype=np.float32)[:, None] * inv_freq[None, :]).astype(np.float64)
    cos = jnp.tile(jnp.asarray(np.cos(ang), F32), (1, 4))
    sin_half = jnp.asarray(np.sin(ang), F32)
    sin = jnp.tile(jnp.concatenate([-sin_half, sin_half], axis=1), (1, 2))

    for layer in range(DEPTH):
        j = layer // 2
        if layer % 2 == 0:
            w_in = ab_w_in[j]
            wglu = w_in[:, :2 * CONV_CH].astype(BF16)
            wqkv = w_in[:, 2 * CONV_CH:2 * CONV_CH + 4 * ATT_WIDTH].astype(BF16)
            wsm = jnp.zeros((d, LANE), F32).at[:, :IDX_DIM + IDX_HEADS].set(
                w_in[:, 2 * CONV_CH + 4 * ATT_WIDTH:]).astype(BF16)
            a2, q, k, v, qi, ki, wi = _l0_in(h, wglu, wqkv, wsm, cos, sin, ab_conv_w[j], row2(ab_conv_b[j]),
                                             row2(ab_ln_g[j]), row2(ab_ln_b[j]))
            pad = lambda t: jnp.pad(t, ((0, lk - lp), (0, 0)))
            o = _dsa(q, qi, wi, pad(k), pad(v), pad(ki), topk)
            w_out = ab_w_out[j].astype(BF16)
            h = _l0_out(a2, o, w_out[:CONV_CH], w_out[CONV_CH:], h, row2(ln_mix_g[layer]), row2(ln_mix_b[layer]))
        else:
            h = _l1_mixer(h, c_w_in[j].astype(BF16), c_conv_w[j], row2(c_conv_b[j]),
                          c_gate_a_w[j].astype(BF16), row2(c_gate_a_b[j]),
                          c_gate_x_w[j].astype(BF16), row2(c_gate_x_b[j]), row2(c_lambda[j]),
                          c_w_out[j].astype(BF16), row2(ln_mix_g[layer]), row2(ln_mix_b[layer]))
        h = _moe(h, n_real, layer, moe_router_group_w[layer], moe_router_group_b[layer],
                 moe_router_expert_w[layer], moe_router_expert_b[layer], moe_w_gate, moe_w_up, moe_w_down,
                 row2(ln_ffn_g[layer]), row2(ln_ffn_b[layer]))
    return h[N_META:n_real][None]
```

```python
import functools

import jax
import jax.numpy as jnp
import numpy as np
from jax import lax
from jax.experimental import pallas as pl
from jax.experimental.pallas import tpu as pltpu

F32 = jnp.float32
BF16 = jnp.bfloat16
I32 = jnp.int32

N_META = 16
CONV_CH = 512
CONV_K = 31
ATT_HEADS = 8
HEAD_DIM = 64
ATT_WIDTH = ATT_HEADS * HEAD_DIM
IDX_HEADS = 8
IDX_DIM = 64
TOPK_MAX = 256
ROPE_THETA = 10000.0
RNN_WIDTH = 1280
RNN_BLOCKS = 10
RNN_BLOCK_W = RNN_WIDTH // RNN_BLOCKS
RNN_CONV_K = 4
RG_C = 8.0
N_GROUPS = 4
EXPERTS_PER_GROUP = 8
N_EXPERTS = N_GROUPS * EXPERTS_PER_GROUP
D_EXPERT = 512
LN_EPS = 1e-5
DEPTH = 2
DN_ALPHA = (2 * DEPTH) ** 0.25

LANE = 128
VMEM_BYTES = 64 << 20

Q_BLOCK = 128
KEY_CHUNK = 512
LIST_DEPTH = 12
CONV_HALO = 32
RNN_HALO = 8
EXPERT_ROWS = 256
ROW_DMA_UNROLL = 8
FLT_MAX = 3.4028234663852886e38
MIN_NORMAL_KEY = 1 << 23
NEG_MASK = -2e30
M_INIT = -1e30
LOG2_E = 1.4426950408889634

_NT = (((1,), (1,)), ((), ()))


def _tile_rows(n):
    for t in (512, 384, 256, 128):
        if n % t == 0:
            return t
    raise ValueError(n)


def _mm(a, b):
    return jnp.dot(a, b, preferred_element_type=F32)


def _layer_norm(x, g, b):
    mu = jnp.mean(x, axis=-1, keepdims=True)
    xc = x - mu
    var = jnp.mean(xc * xc, axis=-1, keepdims=True)
    return xc * lax.rsqrt(var + LN_EPS) * g + b


def _pack_bf16_pairs(x):
    n = x.shape[1] // 2
    bits = lax.bitcast_convert_type(x.astype(BF16).astype(F32), I32)
    return bits[:, :n] | lax.shift_right_logical(bits[:, n:], 16)


def _unpack_bf16_pairs(u):
    first = lax.bitcast_convert_type(u & jnp.int32(-65536), F32)
    second = lax.bitcast_convert_type(lax.shift_left(u, 16), F32)
    return jnp.concatenate([first, second], axis=1)


def _rope_group(t, cos, sin_signed, first_half):
    partner = jnp.where(first_half, pltpu.roll(t, LANE - 32, 1), pltpu.roll(t, 32, 1))
    return t * cos + partner * sin_signed


def _params(sem, vmem_mb):
    assert (vmem_mb << 20) < VMEM_BYTES
    return pltpu.CompilerParams(dimension_semantics=sem, vmem_limit_bytes=vmem_mb << 20)


def _l0_in_kernel(h_ref, wglu_ref, wqkv_ref, wsm_ref, cos_ref, sin_ref, cw_ref, cb_ref, lg_ref, lb_ref,
                  a2_ref, q_ref, k_ref, v_ref, qi_ref, ki_ref, wi_ref, abuf, shift_buf):
    i = pl.program_id(0)
    tm = h_ref.shape[0]
    hb = h_ref[...].astype(BF16)

    glu = _mm(hb, wglu_ref[...])
    a = glu[:, :CONV_CH] * jax.nn.sigmoid(glu[:, CONV_CH:])

    @pl.when(i == 0)
    def _():
        abuf[0:CONV_HALO, :] = jnp.zeros((CONV_HALO, CONV_CH), F32)

    abuf[pl.ds(CONV_HALO, tm), :] = a
    acc = jnp.broadcast_to(cb_ref[...], (tm, CONV_CH))
    first_off = CONV_HALO - (CONV_K - 1)
    for r in range(8):
        taps = [j for j in range(CONV_K) if (first_off + j) % 8 == r]
        n_rows = tm if r == 0 else tm + 8
        part = None
        for j in taps:
            base = (first_off + j) - r
            term = cw_ref[j:j + 1, :] * abuf[pl.ds(base, n_rows), :]
            part = term if part is None else part + term
        if r == 0:
            acc = acc + part
        else:
            shift_buf[...] = part
            acc = acc + shift_buf[pl.ds(r, tm), :]
    abuf[0:CONV_HALO, :] = abuf[pl.ds(tm, CONV_HALO), :]
    y = _layer_norm(acc, lg_ref[...], lb_ref[...])
    a2_ref[...] = (y * jax.nn.sigmoid(y)).astype(BF16)

    cos = cos_ref[...]
    sin = sin_ref[...]
    lane = lax.broadcasted_iota(I32, (tm, LANE), 1)
    first_half = (lane % HEAD_DIM) < (HEAD_DIM // 2)
    qkv = _mm(hb, wqkv_ref[...])
    for g in range(ATT_WIDTH // LANE):
        sl = slice(g * LANE, (g + 1) * LANE)
        qg = qkv[:, g * LANE:(g + 1) * LANE]
        kg = qkv[:, ATT_WIDTH + g * LANE:ATT_WIDTH + (g + 1) * LANE]
        ig = qkv[:, 3 * ATT_WIDTH + g * LANE:3 * ATT_WIDTH + (g + 1) * LANE]
        q_ref[:, sl] = (_rope_group(qg, cos, sin, first_half) * (LOG2_E * HEAD_DIM ** -0.5)).astype(BF16)
        k_ref[:, sl] = _rope_group(kg, cos, sin, first_half).astype(BF16)
        qi_ref[:, sl] = (_rope_group(ig, cos, sin, first_half) * (IDX_DIM ** -0.5)).astype(BF16)
    v_ref[...] = qkv[:, 2 * ATT_WIDTH:3 * ATT_WIDTH].astype(BF16)

    sm = _mm(hb, wsm_ref[...])
    ki = _rope_group(sm, cos, sin, first_half)
    ki_ref[...] = jnp.where(lane < IDX_DIM, ki, 0.0).astype(BF16)
    wi_ref[...] = sm * (IDX_HEADS ** -0.5)


def _l0_in(h, wglu, wqkv, wsm, cos, sin, cw, cb, lg, lb):
    lp, d = h.shape
    tm = _tile_rows(lp)
    row = lambda w: pl.BlockSpec((tm, w), lambda i: (i, 0))
    full = lambda a: pl.BlockSpec(a.shape, lambda i: (0,) * a.ndim)
    outs = [jax.ShapeDtypeStruct((lp, CONV_CH), BF16)] + [jax.ShapeDtypeStruct((lp, ATT_WIDTH), BF16)] * 4 + [
        jax.ShapeDtypeStruct((lp, LANE), BF16), jax.ShapeDtypeStruct((lp, LANE), F32)]
    return pl.pallas_call(
        _l0_in_kernel,
        grid=(lp // tm,),
        in_specs=[row(d), full(wglu), full(wqkv), full(wsm), row(LANE), row(LANE), full(cw), full(cb), full(lg),
                  full(lb)],
        out_specs=[row(CONV_CH)] + [row(ATT_WIDTH)] * 4 + [row(LANE), row(LANE)],
        out_shape=outs,
        scratch_shapes=[pltpu.VMEM((CONV_HALO + tm, CONV_CH), F32), pltpu.VMEM((tm + 8, CONV_CH), F32)],
        compiler_params=_params(("arbitrary",), 48),
        name="l0_in",
    )(h, wglu, wqkv, wsm, cos, sin, cw, cb, lg, lb)


def _two_stage_chunks(nch, first, second):
    first(0, 0)

    def body(t, carry):
        c = 2 * t
        first(c + 1, 1)
        second(c, 0, False)
        first(c + 2, 0)
        second(c + 1, 1, False)
        return carry

    n_pairs = (nch - 1) // 2
    lax.fori_loop(0, n_pairs, body, 0)
    c0 = 2 * n_pairs

    @pl.when(nch - c0 == 2)
    def _():
        first(c0 + 1, 1)
        second(c0, 0, False)
        second(c0 + 1, 1, True)

    @pl.when(nch - c0 == 1)
    def _():
        second(c0, 0, True)


def _dsa_kernel(q_ref, qi_ref, wi_ref, k_ref, v_ref, ki_ref, u_ref, o_ref,
                s_ref, qi8_ref, qh_ref, wb_ref, m_ref, acc_ref, pbuf, cand_ref, thr_ref, nlo_ref,
                *, topk):
    i = pl.program_id(0)
    bq = q_ref.shape[0]
    ck = KEY_CHUNK
    ngrp = ck // LANE
    nch = ((i + 1) * bq + ck - 1) // ck
    lane = lax.broadcasted_iota(I32, (bq, LANE), 1)
    low = lane < HEAD_DIM
    chunk_rows = lambda c: pl.ds(pl.multiple_of(c * ck, ck), ck)

    for g in range(ATT_WIDTH // LANE):
        pair = qi_ref[:, g * LANE:(g + 1) * LANE].astype(F32)
        qi8_ref[pl.ds((2 * g) * bq, bq), :] = jnp.where(low, pair, 0.0).astype(BF16)
        qi8_ref[pl.ds((2 * g + 1) * bq, bq), :] = jnp.where(low, pltpu.roll(pair, HEAD_DIM, 1), 0.0).astype(BF16)
        qp = q_ref[:, g * LANE:(g + 1) * LANE].astype(F32)
        qh_ref[g, 0:bq, :] = jnp.where(low, qp, 0.0).astype(BF16)
        qh_ref[g, bq:2 * bq, :] = jnp.where(low, 0.0, qp).astype(BF16)
    wi = wi_ref[...]
    for h in range(IDX_HEADS):
        wb_ref[h] = jnp.broadcast_to(wi[:, IDX_DIM + h:IDX_DIM + h + 1], (bq, LANE))

    def to_key(x):
        bits = lax.bitcast_convert_type(x, I32)
        return bits ^ ((bits >> 31) & jnp.int32(0x7FFFFFFF))

    def from_key(k):
        k = jnp.where(k > 0, jnp.where(k < MIN_NORMAL_KEY, MIN_NORMAL_KEY, k),
                      jnp.where(k >= -MIN_NORMAL_KEY, 0, k))
        return lax.bitcast_convert_type(k ^ ((k >> 31) & jnp.int32(0x7FFFFFFF)), F32)

    def index_logits(c, slot):
        pbuf[slot] = lax.dot_general(qi8_ref[...], ki_ref[chunk_rows(c), :], _NT, preferred_element_type=F32)

    cand_ref[...] = jnp.full(cand_ref.shape, -jnp.inf, F32)
    lane8 = lax.broadcasted_iota(I32, (8, LANE), 1)
    row8 = lax.broadcasted_iota(I32, (8, LANE), 0)

    def index_scores(c, slot, last):
        for slab in range(bq // 8):
            rows = slice(slab * 8, (slab + 1) * 8)
            lst = [cand_ref[d, rows, :] for d in range(LIST_DEPTH)]
            for j in range(ngrp):
                cols = slice(j * LANE, (j + 1) * LANE)
                sc = jnp.zeros((8, LANE), F32)
                for h in range(IDX_HEADS):
                    hrows = slice(h * bq + slab * 8, h * bq + (slab + 1) * 8)
                    sc = sc + jnp.maximum(pbuf[slot, hrows, cols], 0.0) * wb_ref[h, rows, :]
                if last:
                    valid = (c * ck + j * LANE + lane8) <= (i * bq + slab * 8 + row8)
                    sc = jnp.where(valid, sc, -jnp.inf)
                s_ref[c, rows, cols] = sc
                x = sc
                for d in range(LIST_DEPTH):
                    top = jnp.maximum(lst[d], x)
                    x = jnp.minimum(lst[d], x)
                    lst[d] = top
            for d in range(LIST_DEPTH):
                cand_ref[d, rows, :] = lst[d]

    _two_stage_chunks(nch, index_logits, index_scores)

    def count_ge(cand):
        cand_f = from_key(cand)

        def body(c, cnt):
            blk = s_ref[c]
            for g in range(ngrp):
                cnt = cnt + jnp.where(blk[:, g * LANE:(g + 1) * LANE] >= cand_f, 1.0, 0.0)
            return cnt
        cnt = lax.fori_loop(0, nch, body, jnp.zeros((bq, LANE), F32))
        return jnp.sum(cnt, axis=1, keepdims=True)

    kf = float(topk)
    zeros = jnp.zeros((bq, LANE), F32)
    n_valid = (i * bq + lax.broadcasted_iota(I32, (bq, LANE), 0) + 1).astype(F32)
    lo0 = to_key(jnp.full((bq, LANE), -FLT_MAX, F32))
    hi0 = to_key(jnp.max(cand_ref[0], axis=1, keepdims=True) + zeros) + 1

    def open_rows(lo, hi, n_lo):
        return jnp.where(n_lo > kf, jnp.where(hi > lo + 1, 1.0, 0.0), 0.0)

    def bisect(count_fn, steps, lo_init, n_init):
        def cond(st):
            it, _, _, _, any_open = st
            return jnp.logical_and(it < 34, any_open)

        def body(st):
            it, lo, hi, n_lo, _ = st
            for _ in range(steps):
                is_open = open_rows(lo, hi, n_lo) > 0.0
                mid = (lo >> 1) + (hi >> 1) + (lo & hi & 1)
                cand = jnp.where(is_open, mid, lo)
                n = count_fn(cand) + zeros
                ge = n >= kf
                lo, hi, n_lo = (jnp.where(is_open, jnp.where(ge, cand, lo), lo),
                                jnp.where(is_open, jnp.where(ge, hi, cand), hi),
                                jnp.where(is_open, jnp.where(ge, n, n_lo), n_lo))
            return it + steps, lo, hi, n_lo, jnp.max(open_rows(lo, hi, n_lo)) > 0.0

        st = lax.while_loop(cond, body, (jnp.int32(0), lo_init, hi0, n_init,
                                         jnp.max(open_rows(lo_init, hi0, n_init)) > 0.0))
        return st[1], st[3]

    def count_ge_lists(cand):
        cand_f = from_key(cand)
        cnt = jnp.zeros((bq, LANE), F32)
        for d in range(LIST_DEPTH):
            cnt = cnt + jnp.where(cand_ref[d] >= cand_f, 1.0, 0.0)
        return jnp.sum(cnt, axis=1, keepdims=True)

    runner_up = jnp.min(cand_ref[1], axis=1, keepdims=True) + zeros
    have_two = runner_up > -jnp.inf
    lo_start = jnp.where(have_two, to_key(runner_up), lo0)
    n_start = jnp.where(have_two, count_ge_lists(lo_start) + zeros, n_valid)
    lo_l, n_l = bisect(count_ge_lists, 2, lo_start, n_start)
    thr_ref[...] = lo_l
    nlo_ref[...] = n_l
    deepest = jnp.where(cand_ref[LIST_DEPTH - 1] >= from_key(lo_l), 1.0, 0.0)
    lists_short = jnp.max(jnp.where(n_valid > kf, deepest, 0.0)) > 0.0

    @pl.when(lists_short)
    def _():
        lo_a, n_a = bisect(count_ge, 1, lo0, n_valid)
        thr_ref[...] = lo_a
        nlo_ref[...] = n_a

    n_lo = nlo_ref[...]
    take_all = n_valid <= kf
    thr_key = thr_ref[...]
    thr = jnp.where(take_all, -FLT_MAX, from_key(thr_key))

    m_ref[...] = jnp.full(m_ref.shape, M_INIT, F32)
    acc_ref[...] = jnp.zeros(acc_ref.shape, F32)
    ones = jnp.ones((ck, LANE), BF16)
    pair_rows = lambda g: slice(g * 2 * bq, (g + 1) * 2 * bq)

    def masked_scores(c, slot, bias):
        bias2 = jnp.concatenate([jnp.concatenate(bias, axis=1)] * 2, axis=0)
        for g in range(ATT_WIDTH // LANE):
            kc = k_ref[chunk_rows(c), g * LANE:(g + 1) * LANE]
            pbuf[slot, pair_rows(g), :] = lax.dot_general(qh_ref[g], kc, _NT, preferred_element_type=F32) + bias2

    def softmax_pv(c, slot, last):
        del last
        for g in range(ATT_WIDTH // LANE):
            sj = [pbuf[slot, pair_rows(g), j * LANE:(j + 1) * LANE] for j in range(ngrp)]
            mx = sj[0]
            for j in range(1, ngrp):
                mx = jnp.maximum(mx, sj[j])
            m_prev = m_ref[g]
            m_new = jnp.maximum(m_prev, jnp.max(mx, axis=1, keepdims=True))
            alpha = jnp.exp2(m_prev - m_new)
            p2 = jnp.concatenate([jnp.exp2(s - m_new) for s in sj], axis=1).astype(BF16)
            v_aug = jnp.concatenate([v_ref[chunk_rows(c), g * LANE:(g + 1) * LANE], ones], axis=1)
            acc_ref[g] = jnp.concatenate([alpha, alpha], axis=1) * acc_ref[g] + _mm(p2, v_aug)
            m_ref[g] = m_new

    has_ties = jnp.max(n_lo) > kf

    @pl.when(jnp.logical_not(has_ties))
    def _():
        def scores(c, slot):
            masked_scores(c, slot, [jnp.where(s_ref[c, :, j * LANE:(j + 1) * LANE] >= thr, 0.0, NEG_MASK)
                                    for j in range(ngrp)])

        _two_stage_chunks(nch, scores, softmax_pv)

    @pl.when(has_ties)
    def _():
        n_above = count_ge(thr_key + 1)
        need = jnp.where(take_all[:, 0:1], 0.0, kf - n_above)
        thr_b = jnp.broadcast_to(thr[:, 0:1], (bq, ck))
        need_b = jnp.broadcast_to(need, (bq, ck))

        def attend(c, ties_seen):
            sc = s_ref[c]
            eq = sc == thr_b
            prefix = _mm(jnp.where(eq, 1.0, 0.0).astype(BF16), u_ref[...]) + ties_seen
            take_tie = jnp.where(eq, jnp.where(prefix <= need_b, 0.0, NEG_MASK), NEG_MASK)
            bias = jnp.where(sc > thr_b, 0.0, take_tie)
            masked_scores(c, 0, [bias[:, j * LANE:(j + 1) * LANE] for j in range(ngrp)])
            softmax_pv(c, 0, False)
            return prefix[:, ck - 1:ck]

        lax.fori_loop(0, nch, attend, jnp.zeros((bq, 1), F32))

    for g in range(ATT_WIDTH // LANE):
        acc = acc_ref[g]
        o2 = acc[:, :LANE] * (1.0 / acc[:, LANE:])
        o_ref[:, g * LANE:(g + 1) * LANE] = jnp.where(low, o2[:bq], o2[bq:]).astype(BF16)


def _dsa(q, qi, wi, k, v, ki, topk):
    lp = q.shape[0]
    lk = k.shape[0]
    bq, ck = Q_BLOCK, KEY_CHUNK
    assert topk <= min(ck, 2 * LANE) and lk % ck == 0 and lk >= lp
    u = (np.arange(ck)[:, None] <= np.arange(ck)[None, :]).astype(np.float32)
    u = jnp.asarray(u, BF16)
    row = lambda w: pl.BlockSpec((bq, w), lambda i: (i, 0))
    res = lambda a: pl.BlockSpec(a.shape, lambda i: (0, 0), pipeline_mode=pl.Buffered(1))
    return pl.pallas_call(
        functools.partial(_dsa_kernel, topk=topk),
        grid=(lp // bq,),
        in_specs=[row(ATT_WIDTH), row(ATT_WIDTH), row(LANE), res(k), res(v), res(ki), res(u)],
        out_specs=row(ATT_WIDTH),
        out_shape=jax.ShapeDtypeStruct((lp, ATT_WIDTH), BF16),
        scratch_shapes=[
            pltpu.VMEM((lk // ck, bq, ck), F32),
            pltpu.VMEM((IDX_HEADS * bq, LANE), BF16),
            pltpu.VMEM((ATT_HEADS // 2, 2 * bq, LANE), BF16),
            pltpu.VMEM((IDX_HEADS, bq, LANE), F32),
            pltpu.VMEM((ATT_HEADS // 2, 2 * bq, LANE), F32),
            pltpu.VMEM((ATT_HEADS // 2, 2 * bq, 2 * LANE), F32),
            pltpu.VMEM((2, IDX_HEADS * bq, ck), F32),
            pltpu.VMEM((LIST_DEPTH, bq, LANE), F32),
            pltpu.VMEM((bq, LANE), I32),
            pltpu.VMEM((bq, LANE), F32),
        ],
        compiler_params=_params(("arbitrary",), 60),
        name="dsa",
    )(q, qi, wi, k, v, ki, u)


def _l0_out_kernel(a2_ref, o_ref, wa_ref, wo_ref, h_ref, g_ref, b_ref, out_ref):
    m = _mm(a2_ref[...], wa_ref[...]) + _mm(o_ref[...], wo_ref[...])
    out_ref[...] = _layer_norm(DN_ALPHA * h_ref[...] + m, g_ref[...], b_ref[...])


def _l0_out(a2, o, wa, wo, h, g, b):
    lp, d = h.shape
    tm = _tile_rows(lp)
    row = lambda w: pl.BlockSpec((tm, w), lambda i: (i, 0))
    full = lambda a: pl.BlockSpec(a.shape, lambda i: (0,) * a.ndim)
    return pl.pallas_call(
        _l0_out_kernel,
        grid=(lp // tm,),
        in_specs=[row(CONV_CH), row(ATT_WIDTH), full(wa), full(wo), row(d), full(g), full(b)],
        out_specs=row(d),
        out_shape=jax.ShapeDtypeStruct((lp, d), F32),
        compiler_params=_params(("parallel",), 32),
        name="l0_out",
    )(a2, o, wa, wo, h, g, b)


def _l1_kernel(h_ref, win_ref, cw_ref, cb_ref, wa_ref, ba_ref, wx_ref, bx_ref, lam_ref, wout_ref, g_ref, b_ref,
               out_ref, xbuf, hstate):
    i = pl.program_id(0)
    tm = h_ref.shape[0]
    h = h_ref[...]
    z = _mm(h.astype(BF16), win_ref[...])
    gate = z[:, :RNN_WIDTH]

    @pl.when(i == 0)
    def _():
        xbuf[0:RNN_HALO, :] = jnp.zeros((RNN_HALO, RNN_WIDTH), F32)
        hstate[...] = jnp.zeros(hstate.shape, F32)

    xbuf[pl.ds(RNN_HALO, tm), :] = z[:, RNN_WIDTH:]
    xc = jnp.broadcast_to(cb_ref[...], (tm, RNN_WIDTH))
    for j in range(RNN_CONV_K):
        xc = xc + cw_ref[j:j + 1, :] * xbuf[pl.ds(RNN_HALO - (RNN_CONV_K - 1) + j, tm), :]
    xbuf[0:RNN_HALO, :] = xbuf[pl.ds(tm, RNN_HALO), :]

    xcb = xc.astype(BF16)
    ra, ri = [], []
    for n in range(RNN_BLOCKS):
        blk = xcb[:, n * RNN_BLOCK_W:(n + 1) * RNN_BLOCK_W]
        ra.append(_mm(blk, wa_ref[n]))
        ri.append(_mm(blk, wx_ref[n]))
    r = jax.nn.sigmoid(jnp.concatenate(ra, axis=1) + ba_ref[...])
    ig = jax.nn.sigmoid(jnp.concatenate(ri, axis=1) + bx_ref[...])
    nl = -lam_ref[...]
    softplus = jnp.maximum(nl, 0.0) + jnp.log(1.0 + jnp.exp(-jnp.abs(nl)))
    log_a = -RG_C * r * softplus
    a = jnp.exp(log_a)
    gap = 1.0 - a * a
    u = jnp.where(gap > 0.0, gap * lax.rsqrt(gap), 0.0) * (ig * xc)

    in_group = lax.broadcasted_iota(I32, (tm, RNN_WIDTH), 0) % 8
    for d in (1, 2, 4):
        keep = in_group >= d
        a_sh = jnp.where(keep, pltpu.roll(a, d, 0), 1.0)
        u_sh = jnp.where(keep, pltpu.roll(u, d, 0), 0.0)
        u = a * u_sh + u
        a = a * a_sh
    h_prev = hstate[0:1, :]
    groups = []
    for g in range(tm // 8):
        h_g = u[g * 8:(g + 1) * 8, :] + a[g * 8:(g + 1) * 8, :] * h_prev
        groups.append(h_g)
        h_prev = h_g[7:8, :]
    hs = jnp.concatenate(groups, axis=0)
    hstate[0:1, :] = h_prev

    y = (jax.nn.gelu(gate) * hs).astype(BF16)
    m = _mm(y, wout_ref[...])
    out_ref[...] = _layer_norm(DN_ALPHA * h + m, g_ref[...], b_ref[...])


def _l1_mixer(h, win, cw, cb, wa, ba, wx, bx, lam, wout, g, b):
    lp, d = h.shape
    tm = _tile_rows(lp)
    row = lambda w: pl.BlockSpec((tm, w), lambda i: (i, 0))
    full = lambda a: pl.BlockSpec(a.shape, lambda i: (0,) * a.ndim)
    args = (h, win, cw, cb, wa, ba, wx, bx, lam, wout, g, b)
    return pl.pallas_call(
        _l1_kernel,
        grid=(lp // tm,),
        in_specs=[row(d)] + [full(a) for a in args[1:]],
        out_specs=row(d),
        out_shape=jax.ShapeDtypeStruct((lp, d), F32),
        scratch_shapes=[pltpu.VMEM((RNN_HALO + tm, RNN_WIDTH), F32), pltpu.VMEM((8, RNN_WIDTH), F32)],
        compiler_params=_params(("arbitrary",), 56),
        name="l1_mixer",
    )(*args)


def _router_kernel(h_ref, whi_ref, wlo_ref, br_ref, ltri_ref, eid_ref, gate_ref, rank_ref, cnt_ref, carry_ref,
                   *, n_real):
    i = pl.program_id(0)
    tm = h_ref.shape[0]

    @pl.when(i == 0)
    def _():
        carry_ref[...] = jnp.zeros(carry_ref.shape, F32)

    h = h_ref[...]
    h_hi = h.astype(BF16)
    h_lo = (h - h_hi.astype(F32)).astype(BF16)
    logits = _mm(h_hi, whi_ref[...]) + _mm(h_lo, whi_ref[...]) + _mm(h_hi, wlo_ref[...]) + br_ref[...]
    lane = lax.broadcasted_iota(I32, (tm, LANE), 1).astype(F32)
    ninf = -jnp.inf
    big = float(LANE)

    gl = jnp.where(lane < N_GROUPS, logits[:, :LANE], ninf)
    gmax = jnp.max(gl, axis=1, keepdims=True)
    g_p = 1.0 / jnp.sum(jnp.exp(gl - gmax), axis=1, keepdims=True)
    g_idx = jnp.min(jnp.where(gl == gmax, lane, big), axis=1, keepdims=True)

    first = g_idx * EXPERTS_PER_GROUP
    el = logits[:, LANE:]
    m1 = jnp.where(lane >= first, jnp.where(lane < first + EXPERTS_PER_GROUP, el, ninf), ninf)
    t1 = jnp.max(m1, axis=1, keepdims=True)
    i1 = jnp.min(jnp.where(m1 == t1, lane, big), axis=1, keepdims=True)
    m2 = jnp.where(lane == i1, ninf, m1)
    t2 = jnp.max(m2, axis=1, keepdims=True)
    i2 = jnp.min(jnp.where(m2 == t2, lane, big), axis=1, keepdims=True)
    e2 = jnp.exp(t2 - t1)
    den = 1.0 / (1.0 + e2)

    tok = i * tm + lax.broadcasted_iota(I32, (tm, LANE), 0)
    valid = tok < n_real
    oh0 = jnp.where(valid, jnp.where(lane == i1, 1.0, 0.0), 0.0)
    oh1 = jnp.where(valid, jnp.where(lane == i2, 1.0, 0.0), 0.0)
    ohs = oh0 + oh1
    before = _mm(ltri_ref[...], ohs.astype(BF16)) + carry_ref[...]
    carry_ref[...] = carry_ref[...] + jnp.sum(ohs, axis=0, keepdims=True)
    cnt_ref[...] = carry_ref[...]

    eid_ref[:, 0:1] = i1.astype(I32)
    eid_ref[:, 1:2] = i2.astype(I32)
    gate_ref[:, 0:1] = g_p * den
    gate_ref[:, 1:2] = g_p * e2 * den
    rank_ref[:, 0:1] = jnp.sum(oh0 * before, axis=1, keepdims=True).astype(I32)
    rank_ref[:, 1:2] = jnp.sum(oh1 * before, axis=1, keepdims=True).astype(I32)


def _router(h, wr, br, n_real):
    lp, d = h.shape
    tm = _tile_rows(lp)
    w_hi = wr.astype(BF16)
    w_lo = (wr - w_hi.astype(F32)).astype(BF16)
    ltri = jnp.asarray((np.arange(tm)[:, None] > np.arange(tm)[None, :]).astype(np.float32), BF16)
    row = lambda w: pl.BlockSpec((tm, w), lambda i: (i, 0))
    full = lambda a: pl.BlockSpec(a.shape, lambda i: (0,) * a.ndim)
    return pl.pallas_call(
        functools.partial(_router_kernel, n_real=n_real),
        grid=(lp // tm,),
        in_specs=[row(d), full(w_hi), full(w_lo), full(br), full(ltri)],
        out_specs=[row(2), row(2), row(2), pl.BlockSpec((1, LANE), lambda i: (0, 0))],
        out_shape=[jax.ShapeDtypeStruct((lp, 2), I32), jax.ShapeDtypeStruct((lp, 2), F32),
                   jax.ShapeDtypeStruct((lp, 2), I32), jax.ShapeDtypeStruct((1, LANE), F32)],
        scratch_shapes=[pltpu.VMEM((1, LANE), F32)],
        compiler_params=_params(("arbitrary",), 32),
        name="moe_router",
    )(h, w_hi, w_lo, br, ltri)


def _row_copy(src_ref, src_row, dst_ref, dst_row, sem):
    return pltpu.make_async_copy(src_ref.at[pl.ds(src_row, 1), :], dst_ref.at[pl.ds(dst_row, 1), :], sem)


def _dispatch_kernel(dest_ref, h_ref, xb_in_ref, xb_ref, packed, sem):
    del xb_in_ref
    i = pl.program_id(0)
    tm = h_ref.shape[0]
    half = i % 2
    packed[half] = _pack_bf16_pairs(h_ref[...])

    def issue(r8, carry):
        for u in range(ROW_DMA_UNROLL):
            r = r8 * ROW_DMA_UNROLL + u
            for s in range(2):
                _row_copy(packed.at[half], r, xb_ref, dest_ref[2 * (i * tm + r) + s], sem.at[half]).start()
        return carry

    lax.fori_loop(0, tm // ROW_DMA_UNROLL, issue, 0)

    def drain(which):
        for s in range(2):
            pltpu.make_async_copy(packed.at[which], xb_ref.at[pl.ds(0, tm), :], sem.at[which]).wait()

    @pl.when(i > 0)
    def _():
        drain(1 - half)

    @pl.when(i == pl.num_programs(0) - 1)
    def _():
        drain(half)


def _dispatch(dest_flat, h, n_rows_out):
    lp, d = h.shape
    tm = _tile_rows(lp)
    xb0 = jnp.zeros((n_rows_out, d // 2), I32)
    grid_spec = pltpu.PrefetchScalarGridSpec(
        num_scalar_prefetch=1,
        grid=(lp // tm,),
        in_specs=[pl.BlockSpec((tm, d), lambda i, dest: (i, 0)), pl.BlockSpec(memory_space=pl.ANY)],
        out_specs=pl.BlockSpec(memory_space=pl.ANY),
        scratch_shapes=[pltpu.VMEM((2, tm, d // 2), I32), pltpu.SemaphoreType.DMA((2,))],
    )
    return pl.pallas_call(
        _dispatch_kernel,
        grid_spec=grid_spec,
        out_shape=jax.ShapeDtypeStruct((n_rows_out, d // 2), I32),
        input_output_aliases={2: 0},
        compiler_params=pltpu.CompilerParams(dimension_semantics=("arbitrary",), has_side_effects=True),
        name="moe_dispatch",
    )(dest_flat, h, xb0)


def _experts_kernel(bexp_ref, nused_ref, xb_ref, wg_ref, wu_ref, wd_ref, yb_ref, wg_s, wu_s, wd_s):
    b = pl.program_id(0)
    prev = bexp_ref[jnp.maximum(b - 1, 0)]

    @pl.when((b == 0) | (bexp_ref[b] != prev))
    def _():
        wg_s[...] = wg_ref[...].astype(BF16)
        wu_s[...] = wu_ref[...].astype(BF16)
        wd_s[...] = wd_ref[...].astype(BF16)

    @pl.when(b < nused_ref[0])
    def _():
        x = _unpack_bf16_pairs(xb_ref[...]).astype(BF16)
        gt = _mm(x, wg_s[...])
        up = _mm(x, wu_s[...])
        mid = (gt * jax.nn.sigmoid(gt) * up).astype(BF16)
        yb_ref[...] = _pack_bf16_pairs(_mm(mid, wd_s[...]))

    @pl.when(b >= nused_ref[0])
    def _():
        yb_ref[...] = jnp.zeros(yb_ref.shape, I32)


def _experts(bexp, nused, xb, wg, wu, wd, layer, n_blocks):
    d = xb.shape[1]
    bm = EXPERT_ROWS
    wspec = lambda a: pl.BlockSpec((None, None) + a.shape[2:], lambda b, bexp, *_: (layer, bexp[b], 0, 0))
    grid_spec = pltpu.PrefetchScalarGridSpec(
        num_scalar_prefetch=2,
        grid=(n_blocks,),
        in_specs=[pl.BlockSpec((bm, d), lambda b, *_: (b, 0)), wspec(wg), wspec(wu), wspec(wd)],
        out_specs=pl.BlockSpec((bm, d), lambda b, *_: (b, 0)),
        scratch_shapes=[pltpu.VMEM(wg.shape[2:], BF16), pltpu.VMEM(wu.shape[2:], BF16),
                        pltpu.VMEM(wd.shape[2:], BF16)],
    )
    return pl.pallas_call(
        _experts_kernel,
        grid_spec=grid_spec,
        out_shape=jax.ShapeDtypeStruct((n_blocks * bm, d), I32),
        compiler_params=_params(("arbitrary",), 48),
        name="moe_experts",
    )(bexp, nused, xb, wg, wu, wd)


def _combine_kernel(src_ref, yb_ref, gate_ref, h_ref, g_ref, b_ref, out_ref, ybuf, sem):
    i = pl.program_id(0)
    tm = h_ref.shape[0]

    def gather_tile(tile, half):
        def issue(r8, carry):
            for u in range(ROW_DMA_UNROLL):
                r = r8 * ROW_DMA_UNROLL + u
                for s in range(2):
                    _row_copy(yb_ref, src_ref[2 * (tile * tm + r) + s], ybuf.at[half, s], r, sem.at[half]).start()
            return carry

        lax.fori_loop(0, tm // ROW_DMA_UNROLL, issue, 0)

    @pl.when(i == 0)
    def _():
        gather_tile(0, 0)

    @pl.when(i + 1 < pl.num_programs(0))
    def _():
        gather_tile(i + 1, (i + 1) % 2)

    half = i % 2
    for s in range(2):
        pltpu.make_async_copy(yb_ref.at[pl.ds(0, tm), :], ybuf.at[half, s], sem.at[half]).wait()
    gate = gate_ref[...]
    y = gate[:, 0:1] * _unpack_bf16_pairs(ybuf[half, 0]) + gate[:, 1:2] * _unpack_bf16_pairs(ybuf[half, 1])
    out_ref[...] = _layer_norm(DN_ALPHA * h_ref[...] + y, g_ref[...], b_ref[...])


def _combine(dest_flat, yb, gate, h, g, b):
    lp, d = h.shape
    tm = _tile_rows(lp)
    full = lambda a: pl.BlockSpec(a.shape, lambda i, dest: (0,) * a.ndim)
    grid_spec = pltpu.PrefetchScalarGridSpec(
        num_scalar_prefetch=1,
        grid=(lp // tm,),
        in_specs=[pl.BlockSpec(memory_space=pl.ANY), pl.BlockSpec((tm, 2), lambda i, dest: (i, 0)),
                  pl.BlockSpec((tm, d), lambda i, dest: (i, 0)), full(g), full(b)],
        out_specs=pl.BlockSpec((tm, d), lambda i, dest: (i, 0)),
        scratch_shapes=[pltpu.VMEM((2, 2, tm, d // 2), I32), pltpu.SemaphoreType.DMA((2,))],
    )
    return pl.pallas_call(
        _combine_kernel,
        grid_spec=grid_spec,
        out_shape=jax.ShapeDtypeStruct((lp, d), F32),
        compiler_params=_params(("arbitrary",), 32),
        name="moe_combine",
    )(dest_flat, yb, gate, h, g, b)


def _moe(h, n_real, layer, wg, bg, we, be, w_gate, w_up, w_down, ln_g, ln_b):
    lp, d = h.shape
    bm = EXPERT_ROWS
    wr = jnp.zeros((d, 2 * LANE), F32).at[:, :N_GROUPS].set(wg).at[:, LANE:LANE + N_EXPERTS].set(we)
    br = jnp.zeros((1, 2 * LANE), F32).at[0, :N_GROUPS].set(bg).at[0, LANE:LANE + N_EXPERTS].set(be)
    eid, gate, rank, cnt = _router(h, wr, br, n_real)

    counts = cnt[0, :N_EXPERTS].astype(I32)
    padded = (counts + bm - 1) // bm * bm
    pend = jnp.cumsum(padded)
    pstart = pend - padded
    n_blocks = -(-(2 * n_real + N_EXPERTS * (bm - 1)) // bm)
    cap = n_blocks * bm
    tok = jnp.arange(lp, dtype=I32)[:, None]
    valid = tok < n_real
    experts = jnp.arange(N_EXPERTS, dtype=I32)
    row = jnp.sum(jnp.where(eid[:, :, None] == experts, pstart, 0), axis=-1) + rank
    dest_flat = jnp.where(valid, row, cap + 2 * (tok - n_real) + jnp.arange(2, dtype=I32)[None, :]).reshape(-1)
    src_flat = jnp.where(valid, row, 0).reshape(-1)
    block_start = jnp.arange(n_blocks, dtype=I32) * bm
    bexp = jnp.minimum(jnp.sum((pend[None, :] <= block_start[:, None]).astype(I32), axis=1), N_EXPERTS - 1)
    nused = (pend[-1:] // bm).astype(I32)

    xb = _dispatch(dest_flat, h, cap + 2 * (lp - n_real))
    yb = _experts(bexp, nused, xb, w_gate, w_up, w_down, layer, n_blocks)
    return _combine(src_flat, yb, gate, h, ln_g, ln_b)


def kernel(x, meta_tokens, ab_w_in, ab_conv_w, ab_conv_b, ab_ln_g, ab_ln_b, ab_w_out, c_w_in, c_conv_w, c_conv_b, c_gate_a_w, c_gate_a_b, c_gate_x_w, c_gate_x_b, c_lambda, c_w_out, moe_router_group_w, moe_router_group_b, moe_router_expert_w, moe_router_expert_b, moe_w_gate, moe_w_up, moe_w_down, ln_mix_g, ln_mix_b, ln_ffn_g, ln_ffn_b):
    bsz, seq, d = x.shape
    assert bsz == 1, "kernel is written for batch 1"
    n_real = N_META + seq
    lp = -(-n_real // Q_BLOCK) * Q_BLOCK
    lk = -(-lp // KEY_CHUNK) * KEY_CHUNK
    topk = min(TOPK_MAX, seq // 4)
    row2 = lambda a: a.reshape(1, -1)

    h = jnp.concatenate([meta_tokens.astype(x.dtype), x[0], jnp.zeros((lp - n_real, d), x.dtype)], axis=0)

    half = HEAD_DIM // 2
    inv_freq = (np.float32(ROPE_THETA) ** (np.float32(-2.0) * np.arange(half, dtype=np.float32)
                                           / np.float32(HEAD_DIM))).astype(np.float32)
    ang = (np.arange(lp, dtype=np.float32)[:, None] * inv_freq[None, :]).astype(np.float64)
    cos = jnp.tile(jnp.asarray(np.cos(ang), F32), (1, 4))
    sin_half = jnp.asarray(np.sin(ang), F32)
    sin = jnp.tile(jnp.concatenate([-sin_half, sin_half], axis=1), (1, 2))

    for layer in range(DEPTH):
        j = layer // 2
        if layer % 2 == 0:
            w_in = ab_w_in[j]
            wglu = w_in[:, :2 * CONV_CH].astype(BF16)
            wqkv = w_in[:, 2 * CONV_CH:2 * CONV_CH + 4 * ATT_WIDTH].astype(BF16)
            wsm = jnp.zeros((d, LANE), F32).at[:, :IDX_DIM + IDX_HEADS].set(
                w_in[:, 2 * CONV_CH + 4 * ATT_WIDTH:]).astype(BF16)
            a2, q, k, v, qi, ki, wi = _l0_in(h, wglu, wqkv, wsm, cos, sin, ab_conv_w[j], row2(ab_conv_b[j]),
                                             row2(ab_ln_g[j]), row2(ab_ln_b[j]))
            pad = lambda t: jnp.pad(t, ((0, lk - lp), (0, 0)))
            o = _dsa(q, qi, wi, pad(k), pad(v), pad(ki), topk)
            w_out = ab_w_out[j].astype(BF16)
            h = _l0_out(a2, o, w_out[:CONV_CH], w_out[CONV_CH:], h, row2(ln_mix_g[layer]), row2(ln_mix_b[layer]))
        else:
            h = _l1_mixer(h, c_w_in[j].astype(BF16), c_conv_w[j], row2(c_conv_b[j]),
                          c_gate_a_w[j].astype(BF16), row2(c_gate_a_b[j]),
                          c_gate_x_w[j].astype(BF16), row2(c_gate_x_b[j]), row2(c_lambda[j]),
                          c_w_out[j].astype(BF16), row2(ln_mix_g[layer]), row2(ln_mix_b[layer]))
        h = _moe(h, n_real, layer, moe_router_group_w[layer], moe_router_group_b[layer],
                 moe_router_expert_w[layer], moe_router_expert_b[layer], moe_w_gate, moe_w_up, moe_w_down,
                 row2(ln_ffn_g[layer]), row2(ln_ffn_b[layer]))
    return h[N_META:n_real][None]
```

```python
import functools

import jax
import jax.numpy as jnp
import numpy as np
from jax import lax
from jax.experimental import pallas as pl
from jax.experimental.pallas import tpu as pltpu

F32 = jnp.float32
BF16 = jnp.bfloat16
I32 = jnp.int32

N_META = 16
CONV_CH = 512
CONV_K = 31
ATT_HEADS = 8
HEAD_DIM = 64
ATT_WIDTH = ATT_HEADS * HEAD_DIM
IDX_HEADS = 8
IDX_DIM = 64
TOPK_MAX = 256
ROPE_THETA = 10000.0
RNN_WIDTH = 1280
RNN_BLOCKS = 10
RNN_BLOCK_W = RNN_WIDTH // RNN_BLOCKS
RNN_CONV_K = 4
RG_C = 8.0
N_GROUPS = 4
EXPERTS_PER_GROUP = 8
N_EXPERTS = N_GROUPS * EXPERTS_PER_GROUP
D_EXPERT = 512
LN_EPS = 1e-5
DEPTH = 2
DN_ALPHA = (2 * DEPTH) ** 0.25

LANE = 128
VMEM_BYTES = 64 << 20

Q_BLOCK = 128
KEY_CHUNK = 512
LIST_DEPTH = 12
BISECT_MAX_STEPS = 34
CONV_HALO = 32
RNN_HALO = 8
EXPERT_ROWS = 256
ROW_DMA_UNROLL = 8
FLT_MAX = 3.4028234663852886e38
MIN_NORMAL_KEY = 1 << 23
NEG_MASK = -2e30
M_INIT = -1e30
LOG2_E = 1.4426950408889634

_NT = (((1,), (1,)), ((), ()))


def _tile_rows(n):
    for t in (512, 384, 256, 128):
        if n % t == 0:
            return t
    raise ValueError(n)


def _mm(a, b):
    return jnp.dot(a, b, preferred_element_type=F32)


def _layer_norm(x, g, b):
    mu = jnp.mean(x, axis=-1, keepdims=True)
    xc = x - mu
    var = jnp.mean(xc * xc, axis=-1, keepdims=True)
    return xc * lax.rsqrt(var + LN_EPS) * g + b


def _pack_bf16_pairs(x):
    n = x.shape[1] // 2
    bits = lax.bitcast_convert_type(x.astype(BF16).astype(F32), I32)
    return bits[:, :n] | lax.shift_right_logical(bits[:, n:], 16)


def _unpack_bf16_pairs(u):
    first = lax.bitcast_convert_type(u & jnp.int32(-65536), F32)
    second = lax.bitcast_convert_type(lax.shift_left(u, 16), F32)
    return jnp.concatenate([first, second], axis=1)


def _rope_group(t, cos, sin_signed, first_half):
    partner = jnp.where(first_half, pltpu.roll(t, LANE - 32, 1), pltpu.roll(t, 32, 1))
    return t * cos + partner * sin_signed


def _params(sem, vmem_mb):
    assert (vmem_mb << 20) < VMEM_BYTES
    return pltpu.CompilerParams(dimension_semantics=sem, vmem_limit_bytes=vmem_mb << 20)


def _l0_in_kernel(h_ref, wglu_ref, wqkv_ref, wsm_ref, cos_ref, sin_ref, cw_ref, cb_ref, lg_ref, lb_ref,
                  a2_ref, q_ref, k_ref, v_ref, qi_ref, ki_ref, wi_ref, abuf, shift_buf):
    i = pl.program_id(0)
    tm = h_ref.shape[0]
    hb = h_ref[...].astype(BF16)

    glu = _mm(hb, wglu_ref[...])
    a = glu[:, :CONV_CH] * jax.nn.sigmoid(glu[:, CONV_CH:])

    @pl.when(i == 0)
    def _():
        abuf[0:CONV_HALO, :] = jnp.zeros((CONV_HALO, CONV_CH), F32)

    abuf[pl.ds(CONV_HALO, tm), :] = a
    acc = jnp.broadcast_to(cb_ref[...], (tm, CONV_CH))
    first_off = CONV_HALO - (CONV_K - 1)
    for r in range(8):
        taps = [j for j in range(CONV_K) if (first_off + j) % 8 == r]
        n_rows = tm if r == 0 else tm + 8
        part = None
        for j in taps:
            base = (first_off + j) - r
            term = cw_ref[j:j + 1, :] * abuf[pl.ds(base, n_rows), :]
            part = term if part is None else part + term
        if r == 0:
            acc = acc + part
        else:
            shift_buf[...] = part
            acc = acc + shift_buf[pl.ds(r, tm), :]
    abuf[0:CONV_HALO, :] = abuf[pl.ds(tm, CONV_HALO), :]
    y = _layer_norm(acc, lg_ref[...], lb_ref[...])
    a2_ref[...] = (y * jax.nn.sigmoid(y)).astype(BF16)

    cos = cos_ref[...]
    sin = sin_ref[...]
    lane = lax.broadcasted_iota(I32, (tm, LANE), 1)
    first_half = (lane % HEAD_DIM) < (HEAD_DIM // 2)
    qkv = _mm(hb, wqkv_ref[...])
    for g in range(ATT_WIDTH // LANE):
        sl = slice(g * LANE, (g + 1) * LANE)
        qg = qkv[:, g * LANE:(g + 1) * LANE]
        kg = qkv[:, ATT_WIDTH + g * LANE:ATT_WIDTH + (g + 1) * LANE]
        ig = qkv[:, 3 * ATT_WIDTH + g * LANE:3 * ATT_WIDTH + (g + 1) * LANE]
        q_ref[:, sl] = (_rope_group(qg, cos, sin, first_half) * (LOG2_E * HEAD_DIM ** -0.5)).astype(BF16)
        k_ref[:, sl] = _rope_group(kg, cos, sin, first_half).astype(BF16)
        qi_ref[:, sl] = (_rope_group(ig, cos, sin, first_half) * (IDX_DIM ** -0.5)).astype(BF16)
    v_ref[...] = qkv[:, 2 * ATT_WIDTH:3 * ATT_WIDTH].astype(BF16)

    sm = _mm(hb, wsm_ref[...])
    ki = _rope_group(sm, cos, sin, first_half)
    ki_ref[...] = jnp.where(lane < IDX_DIM, ki, 0.0).astype(BF16)
    wi_ref[...] = sm * (IDX_HEADS ** -0.5)


def _l0_in(h, wglu, wqkv, wsm, cos, sin, cw, cb, lg, lb):
    lp, d = h.shape
    tm = _tile_rows(lp)
    row = lambda w: pl.BlockSpec((tm, w), lambda i: (i, 0))
    full = lambda a: pl.BlockSpec(a.shape, lambda i: (0,) * a.ndim)
    outs = [jax.ShapeDtypeStruct((lp, CONV_CH), BF16)] + [jax.ShapeDtypeStruct((lp, ATT_WIDTH), BF16)] * 4 + [
        jax.ShapeDtypeStruct((lp, LANE), BF16), jax.ShapeDtypeStruct((lp, LANE), F32)]
    return pl.pallas_call(
        _l0_in_kernel,
        grid=(lp // tm,),
        in_specs=[row(d), full(wglu), full(wqkv), full(wsm), row(LANE), row(LANE), full(cw), full(cb), full(lg),
                  full(lb)],
        out_specs=[row(CONV_CH)] + [row(ATT_WIDTH)] * 4 + [row(LANE), row(LANE)],
        out_shape=outs,
        scratch_shapes=[pltpu.VMEM((CONV_HALO + tm, CONV_CH), F32), pltpu.VMEM((tm + 8, CONV_CH), F32)],
        compiler_params=_params(("arbitrary",), 48),
        name="l0_in",
    )(h, wglu, wqkv, wsm, cos, sin, cw, cb, lg, lb)


def _two_stage_chunks(nch, first, second):
    first(0, 0)

    def body(t, carry):
        c = 2 * t
        first(c + 1, 1)
        second(c, 0, False)
        first(c + 2, 0)
        second(c + 1, 1, False)
        return carry

    n_pairs = (nch - 1) // 2
    lax.fori_loop(0, n_pairs, body, 0)
    c0 = 2 * n_pairs

    @pl.when(nch - c0 == 2)
    def _():
        first(c0 + 1, 1)
        second(c0, 0, False)
        second(c0 + 1, 1, True)

    @pl.when(nch - c0 == 1)
    def _():
        second(c0, 0, True)


def _dsa_kernel(q_ref, qi_ref, wi_ref, k_ref, v_ref, ki_ref, u_ref, o_ref,
                s_ref, qi8_ref, qh_ref, wb_ref, m_ref, acc_ref, pbuf, lohi_ref, cand_ref, thr_ref, nlo_ref,
                *, topk):
    i = pl.program_id(0)
    bq = q_ref.shape[0]
    ck = KEY_CHUNK
    ngrp = ck // LANE
    nch = ((i + 1) * bq + ck - 1) // ck
    lane = lax.broadcasted_iota(I32, (bq, LANE), 1)
    low = lane < HEAD_DIM
    chunk_rows = lambda c: pl.ds(pl.multiple_of(c * ck, ck), ck)

    for g in range(ATT_WIDTH // LANE):
        pair = qi_ref[:, g * LANE:(g + 1) * LANE].astype(F32)
        qi8_ref[pl.ds((2 * g) * bq, bq), :] = jnp.where(low, pair, 0.0).astype(BF16)
        qi8_ref[pl.ds((2 * g + 1) * bq, bq), :] = jnp.where(low, pltpu.roll(pair, HEAD_DIM, 1), 0.0).astype(BF16)
        qp = q_ref[:, g * LANE:(g + 1) * LANE].astype(F32)
        qh_ref[g, 0:bq, :] = jnp.where(low, qp, 0.0).astype(BF16)
        qh_ref[g, bq:2 * bq, :] = jnp.where(low, 0.0, qp).astype(BF16)
    wi = wi_ref[...]
    for h in range(IDX_HEADS):
        wb_ref[h] = jnp.broadcast_to(wi[:, IDX_DIM + h:IDX_DIM + h + 1], (bq, LANE))
    lohi_ref[0] = jnp.full((bq, LANE), jnp.inf, F32)
    lohi_ref[1] = jnp.full((bq, LANE), -jnp.inf, F32)

    def to_key(x):
        bits = lax.bitcast_convert_type(x, I32)
        return bits ^ ((bits >> 31) & jnp.int32(0x7FFFFFFF))

    def from_key(k):
        k = jnp.where(k > 0, jnp.where(k < MIN_NORMAL_KEY, MIN_NORMAL_KEY, k),
                      jnp.where(k >= -MIN_NORMAL_KEY, 0, k))
        return lax.bitcast_convert_type(k ^ ((k >> 31) & jnp.int32(0x7FFFFFFF)), F32)

    def index_logits(c, slot):
        pbuf[slot] = lax.dot_general(qi8_ref[...], ki_ref[chunk_rows(c), :], _NT, preferred_element_type=F32)

    cand_ref[...] = jnp.full(cand_ref.shape, -jnp.inf, F32)
    lane8 = lax.broadcasted_iota(I32, (8, LANE), 1)
    row8 = lax.broadcasted_iota(I32, (8, LANE), 0)

    def index_scores(c, slot, last):
        for slab in range(bq // 8):
            rows = slice(slab * 8, (slab + 1) * 8)
            smin = lohi_ref[0, rows, :]
            smax = lohi_ref[1, rows, :]
            lst = [cand_ref[d, rows, :] for d in range(LIST_DEPTH)]
            for j in range(ngrp):
                cols = slice(j * LANE, (j + 1) * LANE)
                sc = jnp.zeros((8, LANE), F32)
                for h in range(IDX_HEADS):
                    hrows = slice(h * bq + slab * 8, h * bq + (slab + 1) * 8)
                    sc = sc + jnp.maximum(pbuf[slot, hrows, cols], 0.0) * wb_ref[h, rows, :]
                if last:
                    valid = (c * ck + j * LANE + lane8) <= (i * bq + slab * 8 + row8)
                    smin = jnp.minimum(smin, jnp.where(valid, sc, jnp.inf))
                    sc = jnp.where(valid, sc, -jnp.inf)
                else:
                    smin = jnp.minimum(smin, sc)
                smax = jnp.maximum(smax, sc)
                s_ref[c, rows, cols] = sc
                x = sc
                for d in range(LIST_DEPTH):
                    top = jnp.maximum(lst[d], x)
                    x = jnp.minimum(lst[d], x)
                    lst[d] = top
            for d in range(LIST_DEPTH):
                cand_ref[d, rows, :] = lst[d]
            lohi_ref[0, rows, :] = smin
            lohi_ref[1, rows, :] = smax

    _two_stage_chunks(nch, index_logits, index_scores)

    def count_ge(cand):
        cand_f = from_key(cand)

        def body(c, cnt):
            blk = s_ref[c]
            for g in range(ngrp):
                cnt = cnt + jnp.where(blk[:, g * LANE:(g + 1) * LANE] >= cand_f, 1.0, 0.0)
            return cnt
        cnt = lax.fori_loop(0, nch, body, jnp.zeros((bq, LANE), F32))
        return jnp.sum(cnt, axis=1, keepdims=True)

    kf = float(topk)
    zeros = jnp.zeros((bq, LANE), F32)
    n_valid = (i * bq + lax.broadcasted_iota(I32, (bq, LANE), 0) + 1).astype(F32)
    lo0 = to_key(jnp.min(lohi_ref[0], axis=1, keepdims=True) + zeros)
    hi0 = to_key(jnp.max(lohi_ref[1], axis=1, keepdims=True) + zeros) + 1

    def open_rows(lo, hi, n_lo):
        return jnp.where(n_lo > kf, jnp.where(hi > lo + 1, 1.0, 0.0), 0.0)

    def bisect(count_fn, steps, lo_init, n_init):
        def cond(st):
            it, _, _, _, any_open = st
            return jnp.logical_and(it < BISECT_MAX_STEPS, any_open)

        def body(st):
            it, lo, hi, n_lo, _ = st
            for _ in range(steps):
                is_open = open_rows(lo, hi, n_lo) > 0.0
                mid = (lo >> 1) + (hi >> 1) + (lo & hi & 1)
                cand = jnp.where(is_open, mid, lo)
                n = count_fn(cand) + zeros
                ge = n >= kf
                lo, hi, n_lo = (jnp.where(is_open, jnp.where(ge, cand, lo), lo),
                                jnp.where(is_open, jnp.where(ge, hi, cand), hi),
                                jnp.where(is_open, jnp.where(ge, n, n_lo), n_lo))
            return it + steps, lo, hi, n_lo, jnp.max(open_rows(lo, hi, n_lo)) > 0.0

        st = lax.while_loop(cond, body, (jnp.int32(0), lo_init, hi0, n_init,
                                         jnp.max(open_rows(lo_init, hi0, n_init)) > 0.0))
        return st[1], st[3]

    def count_ge_lists(cand):
        cand_f = from_key(cand)
        cnt = jnp.zeros((bq, LANE), F32)
        for d in range(LIST_DEPTH):
            cnt = cnt + jnp.where(cand_ref[d] >= cand_f, 1.0, 0.0)
        return jnp.sum(cnt, axis=1, keepdims=True)

    runner_up = jnp.min(cand_ref[1], axis=1, keepdims=True) + zeros
    have_two = runner_up > -jnp.inf
    lo_start = jnp.where(have_two, to_key(runner_up), lo0)
    n_start = jnp.where(have_two, count_ge_lists(lo_start) + zeros, n_valid)
    lo_l, n_l = bisect(count_ge_lists, 2, lo_start, n_start)
    thr_ref[...] = lo_l
    nlo_ref[...] = n_l
    deepest = jnp.where(cand_ref[LIST_DEPTH - 1] >= from_key(lo_l), 1.0, 0.0)
    lists_short = jnp.max(jnp.where(n_valid > kf, deepest, 0.0)) > 0.0

    @pl.when(lists_short)
    def _():
        lo_a, n_a = bisect(count_ge, 1, lo0, n_valid)
        thr_ref[...] = lo_a
        nlo_ref[...] = n_a

    n_lo = nlo_ref[...]
    take_all = n_valid <= kf
    thr_key = thr_ref[...]
    thr = jnp.where(take_all, -FLT_MAX, from_key(thr_key))

    m_ref[...] = jnp.full(m_ref.shape, M_INIT, F32)
    acc_ref[...] = jnp.zeros(acc_ref.shape, F32)
    ones = jnp.ones((ck, LANE), BF16)
    pair_rows = lambda g: slice(g * 2 * bq, (g + 1) * 2 * bq)

    def masked_scores(c, slot, bias):
        bias2 = jnp.concatenate([jnp.concatenate(bias, axis=1)] * 2, axis=0)
        for g in range(ATT_WIDTH // LANE):
            kc = k_ref[chunk_rows(c), g * LANE:(g + 1) * LANE]
            pbuf[slot, pair_rows(g), :] = lax.dot_general(qh_ref[g], kc, _NT, preferred_element_type=F32) + bias2

    def softmax_pv(c, slot, last):
        del last
        for g in range(ATT_WIDTH // LANE):
            sj = [pbuf[slot, pair_rows(g), j * LANE:(j + 1) * LANE] for j in range(ngrp)]
            mx = sj[0]
            for j in range(1, ngrp):
                mx = jnp.maximum(mx, sj[j])
            m_prev = m_ref[g]
            m_new = jnp.maximum(m_prev, jnp.max(mx, axis=1, keepdims=True))
            alpha = jnp.exp2(m_prev - m_new)
            p2 = jnp.concatenate([jnp.exp2(s - m_new) for s in sj], axis=1).astype(BF16)
            v_aug = jnp.concatenate([v_ref[chunk_rows(c), g * LANE:(g + 1) * LANE], ones], axis=1)
            acc_ref[g] = jnp.concatenate([alpha, alpha], axis=1) * acc_ref[g] + _mm(p2, v_aug)
            m_ref[g] = m_new

    has_ties = jnp.max(n_lo) > kf

    @pl.when(jnp.logical_not(has_ties))
    def _():
        def scores(c, slot):
            masked_scores(c, slot, [jnp.where(s_ref[c, :, j * LANE:(j + 1) * LANE] >= thr, 0.0, NEG_MASK)
                                    for j in range(ngrp)])

        _two_stage_chunks(nch, scores, softmax_pv)

    @pl.when(has_ties)
    def _():
        n_above = count_ge(thr_key + 1)
        need = jnp.where(take_all[:, 0:1], 0.0, kf - n_above)
        thr_b = jnp.broadcast_to(thr[:, 0:1], (bq, ck))
        need_b = jnp.broadcast_to(need, (bq, ck))

        def attend(c, ties_seen):
            sc = s_ref[c]
            eq = sc == thr_b
            prefix = _mm(jnp.where(eq, 1.0, 0.0).astype(BF16), u_ref[...]) + ties_seen
            take_tie = jnp.where(eq, jnp.where(prefix <= need_b, 0.0, NEG_MASK), NEG_MASK)
            bias = jnp.where(sc > thr_b, 0.0, take_tie)
            masked_scores(c, 0, [bias[:, j * LANE:(j + 1) * LANE] for j in range(ngrp)])
            softmax_pv(c, 0, False)
            return prefix[:, ck - 1:ck]

        lax.fori_loop(0, nch, attend, jnp.zeros((bq, 1), F32))

    for g in range(ATT_WIDTH // LANE):
        acc = acc_ref[g]
        o2 = acc[:, :LANE] * (1.0 / acc[:, LANE:])
        o_ref[:, g * LANE:(g + 1) * LANE] = jnp.where(low, o2[:bq], o2[bq:]).astype(BF16)


def _dsa(q, qi, wi, k, v, ki, topk):
    lp = q.shape[0]
    lk = k.shape[0]
    bq, ck = Q_BLOCK, KEY_CHUNK
    assert topk <= min(ck, 2 * LANE) and lk % ck == 0 and lk >= lp
    u = (np.arange(ck)[:, None] <= np.arange(ck)[None, :]).astype(np.float32)
    u = jnp.asarray(u, BF16)
    row = lambda w: pl.BlockSpec((bq, w), lambda i: (i, 0))
    res = lambda a: pl.BlockSpec(a.shape, lambda i: (0, 0), pipeline_mode=pl.Buffered(1))
    return pl.pallas_call(
        functools.partial(_dsa_kernel, topk=topk),
        grid=(lp // bq,),
        in_specs=[row(ATT_WIDTH), row(ATT_WIDTH), row(LANE), res(k), res(v), res(ki), res(u)],
        out_specs=row(ATT_WIDTH),
        out_shape=jax.ShapeDtypeStruct((lp, ATT_WIDTH), BF16),
        scratch_shapes=[
            pltpu.VMEM((lk // ck, bq, ck), F32),
            pltpu.VMEM((IDX_HEADS * bq, LANE), BF16),
            pltpu.VMEM((ATT_HEADS // 2, 2 * bq, LANE), BF16),
            pltpu.VMEM((IDX_HEADS, bq, LANE), F32),
            pltpu.VMEM((ATT_HEADS // 2, 2 * bq, LANE), F32),
            pltpu.VMEM((ATT_HEADS // 2, 2 * bq, 2 * LANE), F32),
            pltpu.VMEM((2, IDX_HEADS * bq, ck), F32),
            pltpu.VMEM((2, bq, LANE), F32),
            pltpu.VMEM((LIST_DEPTH, bq, LANE), F32),
            pltpu.VMEM((bq, LANE), I32),
            pltpu.VMEM((bq, LANE), F32),
        ],
        compiler_params=_params(("arbitrary",), 60),
        name="dsa",
    )(q, qi, wi, k, v, ki, u)


def _l0_out_kernel(a2_ref, o_ref, wa_ref, wo_ref, h_ref, g_ref, b_ref, out_ref):
    m = _mm(a2_ref[...], wa_ref[...]) + _mm(o_ref[...], wo_ref[...])
    out_ref[...] = _layer_norm(DN_ALPHA * h_ref[...] + m, g_ref[...], b_ref[...])


def _l0_out(a2, o, wa, wo, h, g, b):
    lp, d = h.shape
    tm = _tile_rows(lp)
    row = lambda w: pl.BlockSpec((tm, w), lambda i: (i, 0))
    full = lambda a: pl.BlockSpec(a.shape, lambda i: (0,) * a.ndim)
    return pl.pallas_call(
        _l0_out_kernel,
        grid=(lp // tm,),
        in_specs=[row(CONV_CH), row(ATT_WIDTH), full(wa), full(wo), row(d), full(g), full(b)],
        out_specs=row(d),
        out_shape=jax.ShapeDtypeStruct((lp, d), F32),
        compiler_params=_params(("parallel",), 32),
        name="l0_out",
    )(a2, o, wa, wo, h, g, b)


def _l1_kernel(h_ref, win_ref, cw_ref, cb_ref, wa_ref, ba_ref, wx_ref, bx_ref, lam_ref, wout_ref, g_ref, b_ref,
               out_ref, xbuf, hstate):
    i = pl.program_id(0)
    tm = h_ref.shape[0]
    h = h_ref[...]
    z = _mm(h.astype(BF16), win_ref[...])
    gate = z[:, :RNN_WIDTH]

    @pl.when(i == 0)
    def _():
        xbuf[0:RNN_HALO, :] = jnp.zeros((RNN_HALO, RNN_WIDTH), F32)
        hstate[...] = jnp.zeros(hstate.shape, F32)

    xbuf[pl.ds(RNN_HALO, tm), :] = z[:, RNN_WIDTH:]
    xc = jnp.broadcast_to(cb_ref[...], (tm, RNN_WIDTH))
    for j in range(RNN_CONV_K):
        xc = xc + cw_ref[j:j + 1, :] * xbuf[pl.ds(RNN_HALO - (RNN_CONV_K - 1) + j, tm), :]
    xbuf[0:RNN_HALO, :] = xbuf[pl.ds(tm, RNN_HALO), :]

    xcb = xc.astype(BF16)
    ra, ri = [], []
    for n in range(RNN_BLOCKS):
        blk = xcb[:, n * RNN_BLOCK_W:(n + 1) * RNN_BLOCK_W]
        ra.append(_mm(blk, wa_ref[n]))
        ri.append(_mm(blk, wx_ref[n]))
    r = jax.nn.sigmoid(jnp.concatenate(ra, axis=1) + ba_ref[...])
    ig = jax.nn.sigmoid(jnp.concatenate(ri, axis=1) + bx_ref[...])
    nl = -lam_ref[...]
    softplus = jnp.maximum(nl, 0.0) + jnp.log(1.0 + jnp.exp(-jnp.abs(nl)))
    log_a = -RG_C * r * softplus
    a = jnp.exp(log_a)
    gap = 1.0 - a * a
    u = jnp.where(gap > 0.0, gap * lax.rsqrt(gap), 0.0) * (ig * xc)

    in_group = lax.broadcasted_iota(I32, (tm, RNN_WIDTH), 0) % 8
    for d in (1, 2, 4):
        keep = in_group >= d
        a_sh = jnp.where(keep, pltpu.roll(a, d, 0), 1.0)
        u_sh = jnp.where(keep, pltpu.roll(u, d, 0), 0.0)
        u = a * u_sh + u
        a = a * a_sh
    h_prev = hstate[0:1, :]
    groups = []
    for g in range(tm // 8):
        h_g = u[g * 8:(g + 1) * 8, :] + a[g * 8:(g + 1) * 8, :] * h_prev
        groups.append(h_g)
        h_prev = h_g[7:8, :]
    hs = jnp.concatenate(groups, axis=0)
    hstate[0:1, :] = h_prev

    y = (jax.nn.gelu(gate) * hs).astype(BF16)
    m = _mm(y, wout_ref[...])
    out_ref[...] = _layer_norm(DN_ALPHA * h + m, g_ref[...], b_ref[...])


def _l1_mixer(h, win, cw, cb, wa, ba, wx, bx, lam, wout, g, b):
    lp, d = h.shape
    tm = _tile_rows(lp)
    row = lambda w: pl.BlockSpec((tm, w), lambda i: (i, 0))
    full = lambda a: pl.BlockSpec(a.shape, lambda i: (0,) * a.ndim)
    args = (h, win, cw, cb, wa, ba, wx, bx, lam, wout, g, b)
    return pl.pallas_call(
        _l1_kernel,
        grid=(lp // tm,),
        in_specs=[row(d)] + [full(a) for a in args[1:]],
        out_specs=row(d),
        out_shape=jax.ShapeDtypeStruct((lp, d), F32),
        scratch_shapes=[pltpu.VMEM((RNN_HALO + tm, RNN_WIDTH), F32), pltpu.VMEM((8, RNN_WIDTH), F32)],
        compiler_params=_params(("arbitrary",), 56),
        name="l1_mixer",
    )(*args)


def _router_kernel(h_ref, whi_ref, wlo_ref, br_ref, ltri_ref, eid_ref, gate_ref, rank_ref, cnt_ref, carry_ref,
                   *, n_real):
    i = pl.program_id(0)
    tm = h_ref.shape[0]

    @pl.when(i == 0)
    def _():
        carry_ref[...] = jnp.zeros(carry_ref.shape, F32)

    h = h_ref[...]
    h_hi = h.astype(BF16)
    h_lo = (h - h_hi.astype(F32)).astype(BF16)
    logits = _mm(h_hi, whi_ref[...]) + _mm(h_lo, whi_ref[...]) + _mm(h_hi, wlo_ref[...]) + br_ref[...]
    lane = lax.broadcasted_iota(I32, (tm, LANE), 1).astype(F32)
    ninf = -jnp.inf
    big = float(LANE)

    gl = jnp.where(lane < N_GROUPS, logits[:, :LANE], ninf)
    gmax = jnp.max(gl, axis=1, keepdims=True)
    g_p = 1.0 / jnp.sum(jnp.exp(gl - gmax), axis=1, keepdims=True)
    g_idx = jnp.min(jnp.where(gl == gmax, lane, big), axis=1, keepdims=True)

    first = g_idx * EXPERTS_PER_GROUP
    el = logits[:, LANE:]
    m1 = jnp.where(lane >= first, jnp.where(lane < first + EXPERTS_PER_GROUP, el, ninf), ninf)
    t1 = jnp.max(m1, axis=1, keepdims=True)
    i1 = jnp.min(jnp.where(m1 == t1, lane, big), axis=1, keepdims=True)
    m2 = jnp.where(lane == i1, ninf, m1)
    t2 = jnp.max(m2, axis=1, keepdims=True)
    i2 = jnp.min(jnp.where(m2 == t2, lane, big), axis=1, keepdims=True)
    e2 = jnp.exp(t2 - t1)
    den = 1.0 / (1.0 + e2)

    tok = i * tm + lax.broadcasted_iota(I32, (tm, LANE), 0)
    valid = tok < n_real
    oh0 = jnp.where(valid, jnp.where(lane == i1, 1.0, 0.0), 0.0)
    oh1 = jnp.where(valid, jnp.where(lane == i2, 1.0, 0.0), 0.0)
    ohs = oh0 + oh1
    before = _mm(ltri_ref[...], ohs.astype(BF16)) + carry_ref[...]
    carry_ref[...] = carry_ref[...] + jnp.sum(ohs, axis=0, keepdims=True)
    cnt_ref[...] = carry_ref[...]

    eid_ref[:, 0:1] = i1.astype(I32)
    eid_ref[:, 1:2] = i2.astype(I32)
    gate_ref[:, 0:1] = g_p * den
    gate_ref[:, 1:2] = g_p * e2 * den
    rank_ref[:, 0:1] = jnp.sum(oh0 * before, axis=1, keepdims=True).astype(I32)
    rank_ref[:, 1:2] = jnp.sum(oh1 * before, axis=1, keepdims=True).astype(I32)


def _router(h, wr, br, n_real):
    lp, d = h.shape
    tm = _tile_rows(lp)
    w_hi = wr.astype(BF16)
    w_lo = (wr - w_hi.astype(F32)).astype(BF16)
    ltri = jnp.asarray((np.arange(tm)[:, None] > np.arange(tm)[None, :]).astype(np.float32), BF16)
    row = lambda w: pl.BlockSpec((tm, w), lambda i: (i, 0))
    full = lambda a: pl.BlockSpec(a.shape, lambda i: (0,) * a.ndim)
    return pl.pallas_call(
        functools.partial(_router_kernel, n_real=n_real),
        grid=(lp // tm,),
        in_specs=[row(d), full(w_hi), full(w_lo), full(br), full(ltri)],
        out_specs=[row(2), row(2), row(2), pl.BlockSpec((1, LANE), lambda i: (0, 0))],
        out_shape=[jax.ShapeDtypeStruct((lp, 2), I32), jax.ShapeDtypeStruct((lp, 2), F32),
                   jax.ShapeDtypeStruct((lp, 2), I32), jax.ShapeDtypeStruct((1, LANE), F32)],
        scratch_shapes=[pltpu.VMEM((1, LANE), F32)],
        compiler_params=_params(("arbitrary",), 32),
        name="moe_router",
    )(h, w_hi, w_lo, br, ltri)


def _row_copy(src_ref, src_row, dst_ref, dst_row, sem):
    return pltpu.make_async_copy(src_ref.at[pl.ds(src_row, 1), :], dst_ref.at[pl.ds(dst_row, 1), :], sem)


def _dispatch_kernel(dest_ref, h_ref, xb_in_ref, xb_ref, packed, sem):
    del xb_in_ref
    i = pl.program_id(0)
    tm = h_ref.shape[0]
    half = i % 2
    packed[half] = _pack_bf16_pairs(h_ref[...])

    def issue(r8, carry):
        for u in range(ROW_DMA_UNROLL):
            r = r8 * ROW_DMA_UNROLL + u
            for s in range(2):
                _row_copy(packed.at[half], r, xb_ref, dest_ref[2 * (i * tm + r) + s], sem.at[half]).start()
        return carry

    lax.fori_loop(0, tm // ROW_DMA_UNROLL, issue, 0)

    def drain(which):
        for s in range(2):
            pltpu.make_async_copy(packed.at[which], xb_ref.at[pl.ds(0, tm), :], sem.at[which]).wait()

    @pl.when(i > 0)
    def _():
        drain(1 - half)

    @pl.when(i == pl.num_programs(0) - 1)
    def _():
        drain(half)


def _dispatch(dest_flat, h, n_rows_out):
    lp, d = h.shape
    tm = _tile_rows(lp)
    xb0 = jnp.zeros((n_rows_out, d // 2), I32)
    grid_spec = pltpu.PrefetchScalarGridSpec(
        num_scalar_prefetch=1,
        grid=(lp // tm,),
        in_specs=[pl.BlockSpec((tm, d), lambda i, dest: (i, 0)), pl.BlockSpec(memory_space=pl.ANY)],
        out_specs=pl.BlockSpec(memory_space=pl.ANY),
        scratch_shapes=[pltpu.VMEM((2, tm, d // 2), I32), pltpu.SemaphoreType.DMA((2,))],
    )
    return pl.pallas_call(
        _dispatch_kernel,
        grid_spec=grid_spec,
        out_shape=jax.ShapeDtypeStruct((n_rows_out, d // 2), I32),
        input_output_aliases={2: 0},
        compiler_params=pltpu.CompilerParams(dimension_semantics=("arbitrary",), has_side_effects=True),
        name="moe_dispatch",
    )(dest_flat, h, xb0)


def _experts_kernel(bexp_ref, nused_ref, xb_ref, wg_ref, wu_ref, wd_ref, yb_ref, wg_s, wu_s, wd_s):
    b = pl.program_id(0)
    prev = bexp_ref[jnp.maximum(b - 1, 0)]

    @pl.when((b == 0) | (bexp_ref[b] != prev))
    def _():
        wg_s[...] = wg_ref[...].astype(BF16)
        wu_s[...] = wu_ref[...].astype(BF16)
        wd_s[...] = wd_ref[...].astype(BF16)

    @pl.when(b < nused_ref[0])
    def _():
        x = _unpack_bf16_pairs(xb_ref[...]).astype(BF16)
        gt = _mm(x, wg_s[...])
        up = _mm(x, wu_s[...])
        mid = (gt * jax.nn.sigmoid(gt) * up).astype(BF16)
        yb_ref[...] = _pack_bf16_pairs(_mm(mid, wd_s[...]))

    @pl.when(b >= nused_ref[0])
    def _():
        yb_ref[...] = jnp.zeros(yb_ref.shape, I32)


def _experts(bexp, nused, xb, wg, wu, wd, layer, n_blocks):
    d = xb.shape[1]
    bm = EXPERT_ROWS
    wspec = lambda a: pl.BlockSpec((None, None) + a.shape[2:], lambda b, bexp, *_: (layer, bexp[b], 0, 0))
    grid_spec = pltpu.PrefetchScalarGridSpec(
        num_scalar_prefetch=2,
        grid=(n_blocks,),
        in_specs=[pl.BlockSpec((bm, d), lambda b, *_: (b, 0)), wspec(wg), wspec(wu), wspec(wd)],
        out_specs=pl.BlockSpec((bm, d), lambda b, *_: (b, 0)),
        scratch_shapes=[pltpu.VMEM(wg.shape[2:], BF16), pltpu.VMEM(wu.shape[2:], BF16),
                        pltpu.VMEM(wd.shape[2:], BF16)],
    )
    return pl.pallas_call(
        _experts_kernel,
        grid_spec=grid_spec,
        out_shape=jax.ShapeDtypeStruct((n_blocks * bm, d), I32),
        compiler_params=_params(("arbitrary",), 48),
        name="moe_experts",
    )(bexp, nused, xb, wg, wu, wd)


def _combine_kernel(src_ref, yb_ref, gate_ref, h_ref, g_ref, b_ref, out_ref, ybuf, sem):
    i = pl.program_id(0)
    tm = h_ref.shape[0]

    def gather_tile(tile, half):
        def issue(r8, carry):
            for u in range(ROW_DMA_UNROLL):
                r = r8 * ROW_DMA_UNROLL + u
                for s in range(2):
                    _row_copy(yb_ref, src_ref[2 * (tile * tm + r) + s], ybuf.at[half, s], r, sem.at[half]).start()
            return carry

        lax.fori_loop(0, tm // ROW_DMA_UNROLL, issue, 0)

    @pl.when(i == 0)
    def _():
        gather_tile(0, 0)

    @pl.when(i + 1 < pl.num_programs(0))
    def _():
        gather_tile(i + 1, (i + 1) % 2)

    half = i % 2
    for s in range(2):
        pltpu.make_async_copy(yb_ref.at[pl.ds(0, tm), :], ybuf.at[half, s], sem.at[half]).wait()
    gate = gate_ref[...]
    y = gate[:, 0:1] * _unpack_bf16_pairs(ybuf[half, 0]) + gate[:, 1:2] * _unpack_bf16_pairs(ybuf[half, 1])
    out_ref[...] = _layer_norm(DN_ALPHA * h_ref[...] + y, g_ref[...], b_ref[...])


def _combine(dest_flat, yb, gate, h, g, b):
    lp, d = h.shape
    tm = _tile_rows(lp)
    full = lambda a: pl.BlockSpec(a.shape, lambda i, dest: (0,) * a.ndim)
    grid_spec = pltpu.PrefetchScalarGridSpec(
        num_scalar_prefetch=1,
        grid=(lp // tm,),
        in_specs=[pl.BlockSpec(memory_space=pl.ANY), pl.BlockSpec((tm, 2), lambda i, dest: (i, 0)),
                  pl.BlockSpec((tm, d), lambda i, dest: (i, 0)), full(g), full(b)],
        out_specs=pl.BlockSpec((tm, d), lambda i, dest: (i, 0)),
        scratch_shapes=[pltpu.VMEM((2, 2, tm, d // 2), I32), pltpu.SemaphoreType.DMA((2,))],
    )
    return pl.pallas_call(
        _combine_kernel,
        grid_spec=grid_spec,
        out_shape=jax.ShapeDtypeStruct((lp, d), F32),
        compiler_params=_params(("arbitrary",), 32),
        name="moe_combine",
    )(dest_flat, yb, gate, h, g, b)


def _moe(h, n_real, layer, wg, bg, we, be, w_gate, w_up, w_down, ln_g, ln_b):
    lp, d = h.shape
    bm = EXPERT_ROWS
    wr = jnp.zeros((d, 2 * LANE), F32).at[:, :N_GROUPS].set(wg).at[:, LANE:LANE + N_EXPERTS].set(we)
    br = jnp.zeros((1, 2 * LANE), F32).at[0, :N_GROUPS].set(bg).at[0, LANE:LANE + N_EXPERTS].set(be)
    eid, gate, rank, cnt = _router(h, wr, br, n_real)

    counts = cnt[0, :N_EXPERTS].astype(I32)
    padded = (counts + bm - 1) // bm * bm
    pend = jnp.cumsum(padded)
    pstart = pend - padded
    n_blocks = -(-(2 * n_real + N_EXPERTS * (bm - 1)) // bm)
    cap = n_blocks * bm
    tok = jnp.arange(lp, dtype=I32)[:, None]
    valid = tok < n_real
    experts = jnp.arange(N_EXPERTS, dtype=I32)
    row = jnp.sum(jnp.where(eid[:, :, None] == experts, pstart, 0), axis=-1) + rank
    dest_flat = jnp.where(valid, row, cap + 2 * (tok - n_real) + jnp.arange(2, dtype=I32)[None, :]).reshape(-1)
    src_flat = jnp.where(valid, row, 0).reshape(-1)
    block_start = jnp.arange(n_blocks, dtype=I32) * bm
    bexp = jnp.minimum(jnp.sum((pend[None, :] <= block_start[:, None]).astype(I32), axis=1), N_EXPERTS - 1)
    nused = (pend[-1:] // bm).astype(I32)

    xb = _dispatch(dest_flat, h, cap + 2 * (lp - n_real))
    yb = _experts(bexp, nused, xb, w_gate, w_up, w_down, layer, n_blocks)
    return _combine(src_flat, yb, gate, h, ln_g, ln_b)


def kernel(x, meta_tokens, ab_w_in, ab_conv_w, ab_conv_b, ab_ln_g, ab_ln_b, ab_w_out, c_w_in, c_conv_w, c_conv_b, c_gate_a_w, c_gate_a_b, c_gate_x_w, c_gate_x_b, c_lambda, c_w_out, moe_router_group_w, moe_router_group_b, moe_router_expert_w, moe_router_expert_b, moe_w_gate, moe_w_up, moe_w_down, ln_mix_g, ln_mix_b, ln_ffn_g, ln_ffn_b):
    bsz, seq, d = x.shape
    assert bsz == 1, "kernel is written for batch 1"
    n_real = N_META + seq
    lp = -(-n_real // Q_BLOCK) * Q_BLOCK
    lk = -(-lp // KEY_CHUNK) * KEY_CHUNK
    topk = min(TOPK_MAX, seq // 4)
    row2 = lambda a: a.reshape(1, -1)

    h = jnp.concatenate([meta_tokens.astype(x.dtype), x[0], jnp.zeros((lp - n_real, d), x.dtype)], axis=0)

    half = HEAD_DIM // 2
    inv_freq = (np.float32(ROPE_THETA) ** (np.float32(-2.0) * np.arange(half, dtype=np.float32)
                                           / np.float32(HEAD_DIM))).astype(np.float32)
    ang = (np.arange(lp, dtype=np.float32)[:, None] * inv_freq[None, :]).astype(np.float64)
    cos = jnp.tile(jnp.asarray(np.cos(ang), F32), (1, 4))
    sin_half = jnp.asarray(np.sin(ang), F32)
    sin = jnp.tile(jnp.concatenate([-sin_half, sin_half], axis=1), (1, 2))

    for layer in range(DEPTH):
        j = layer // 2
        if layer % 2 == 0:
            w_in = ab_w_in[j]
            wglu = w_in[:, :2 * CONV_CH].astype(BF16)
            wqkv = w_in[:, 2 * CONV_CH:2 * CONV_CH + 4 * ATT_WIDTH].astype(BF16)
            wsm = jnp.zeros((d, LANE), F32).at[:, :IDX_DIM + IDX_HEADS].set(
                w_in[:, 2 * CONV_CH + 4 * ATT_WIDTH:]).astype(BF16)
            a2, q, k, v, qi, ki, wi = _l0_in(h, wglu, wqkv, wsm, cos, sin, ab_conv_w[j], row2(ab_conv_b[j]),
                                             row2(ab_ln_g[j]), row2(ab_ln_b[j]))
            pad = lambda t: jnp.pad(t, ((0, lk - lp), (0, 0)))
            o = _dsa(q, qi, wi, pad(k), pad(v), pad(ki), topk)
            w_out = ab_w_out[j].astype(BF16)
            h = _l0_out(a2, o, w_out[:CONV_CH], w_out[CONV_CH:], h, row2(ln_mix_g[layer]), row2(ln_mix_b[layer]))
        else:
            h = _l1_mixer(h, c_w_in[j].astype(BF16), c_conv_w[j], row2(c_conv_b[j]),
                          c_gate_a_w[j].astype(BF16), row2(c_gate_a_b[j]),
                          c_gate_x_w[j].astype(BF16), row2(c_gate_x_b[j]), row2(c_lambda[j]),
                          c_w_out[j].astype(BF16), row2(ln_mix_g[layer]), row2(ln_mix_b[layer]))
        h = _moe(h, n_real, layer, moe_router_group_w[layer], moe_router_group_b[layer],
                 moe_router_expert_w[layer], moe_router_expert_b[layer], moe_w_gate, moe_w_up, moe_w_down,
                 row2(ln_ffn_g[layer]), row2(ln_ffn_b[layer]))
    return h[N_META:n_real][None]
```

```python
import functools

import jax
import jax.numpy as jnp
import numpy as np
from jax import lax
from jax.experimental import pallas as pl
from jax.experimental.pallas import tpu as pltpu

F32 = jnp.float32
BF16 = jnp.bfloat16
I32 = jnp.int32

N_META = 16
CONV_CH = 512
CONV_K = 31
ATT_HEADS = 8
HEAD_DIM = 64
ATT_WIDTH = ATT_HEADS * HEAD_DIM
IDX_HEADS = 8
IDX_DIM = 64
TOPK_MAX = 256
ROPE_THETA = 10000.0
RNN_WIDTH = 1280
RNN_BLOCKS = 10
RNN_BLOCK_W = RNN_WIDTH // RNN_BLOCKS
RNN_CONV_K = 4
RG_C = 8.0
N_GROUPS = 4
EXPERTS_PER_GROUP = 8
N_EXPERTS = N_GROUPS * EXPERTS_PER_GROUP
D_EXPERT = 512
LN_EPS = 1e-5
DEPTH = 2
DN_ALPHA = (2 * DEPTH) ** 0.25

LANE = 128
VMEM_BYTES = 64 << 20

Q_BLOCK = 128
KEY_CHUNK = 512
LIST_DEPTH = 12
CONV_HALO = 32
RNN_HALO = 8
EXPERT_ROWS = 512
ROW_DMA_UNROLL = 8
FLT_MAX = 3.4028234663852886e38
MIN_NORMAL_KEY = 1 << 23
NEG_MASK = -2e30
M_INIT = -1e30
LOG2_E = 1.4426950408889634

_NT = (((1,), (1,)), ((), ()))


def _tile_rows(n):
    for t in (512, 384, 256, 128):
        if n % t == 0:
            return t
    raise ValueError(n)


def _mm(a, b):
    return jnp.dot(a, b, preferred_element_type=F32)


def _layer_norm(x, g, b):
    mu = jnp.mean(x, axis=-1, keepdims=True)
    xc = x - mu
    var = jnp.mean(xc * xc, axis=-1, keepdims=True)
    return xc * lax.rsqrt(var + LN_EPS) * g + b


def _pack_bf16_pairs(x):
    n = x.shape[1] // 2
    bits = lax.bitcast_convert_type(x.astype(BF16).astype(F32), I32)
    return bits[:, :n] | lax.shift_right_logical(bits[:, n:], 16)


def _unpack_bf16_pairs(u):
    first = lax.bitcast_convert_type(u & jnp.int32(-65536), F32)
    second = lax.bitcast_convert_type(lax.shift_left(u, 16), F32)
    return jnp.concatenate([first, second], axis=1)


def _rope_group(t, cos, sin_signed, first_half):
    partner = jnp.where(first_half, pltpu.roll(t, LANE - 32, 1), pltpu.roll(t, 32, 1))
    return t * cos + partner * sin_signed


def _params(sem, vmem_mb):
    assert (vmem_mb << 20) < VMEM_BYTES
    return pltpu.CompilerParams(dimension_semantics=sem, vmem_limit_bytes=vmem_mb << 20)


def _l0_in_kernel(h_ref, wglu_ref, wqkv_ref, wsm_ref, cos_ref, sin_ref, cw_ref, cb_ref, lg_ref, lb_ref,
                  a2_ref, q_ref, k_ref, v_ref, qi_ref, ki_ref, wi_ref, abuf, shift_buf):
    i = pl.program_id(0)
    tm = h_ref.shape[0]
    hb = h_ref[...].astype(BF16)

    glu = _mm(hb, wglu_ref[...])
    a = glu[:, :CONV_CH] * jax.nn.sigmoid(glu[:, CONV_CH:])

    @pl.when(i == 0)
    def _():
        abuf[0:CONV_HALO, :] = jnp.zeros((CONV_HALO, CONV_CH), F32)

    abuf[pl.ds(CONV_HALO, tm), :] = a
    acc = jnp.broadcast_to(cb_ref[...], (tm, CONV_CH))
    first_off = CONV_HALO - (CONV_K - 1)
    for r in range(8):
        taps = [j for j in range(CONV_K) if (first_off + j) % 8 == r]
        n_rows = tm if r == 0 else tm + 8
        part = None
        for j in taps:
            base = (first_off + j) - r
            term = cw_ref[j:j + 1, :] * abuf[pl.ds(base, n_rows), :]
            part = term if part is None else part + term
        if r == 0:
            acc = acc + part
        else:
            shift_buf[...] = part
            acc = acc + shift_buf[pl.ds(r, tm), :]
    abuf[0:CONV_HALO, :] = abuf[pl.ds(tm, CONV_HALO), :]
    y = _layer_norm(acc, lg_ref[...], lb_ref[...])
    a2_ref[...] = (y * jax.nn.sigmoid(y)).astype(BF16)

    cos = cos_ref[...]
    sin = sin_ref[...]
    lane = lax.broadcasted_iota(I32, (tm, LANE), 1)
    first_half = (lane % HEAD_DIM) < (HEAD_DIM // 2)
    qkv = _mm(hb, wqkv_ref[...])
    for g in range(ATT_WIDTH // LANE):
        sl = slice(g * LANE, (g + 1) * LANE)
        qg = qkv[:, g * LANE:(g + 1) * LANE]
        kg = qkv[:, ATT_WIDTH + g * LANE:ATT_WIDTH + (g + 1) * LANE]
        ig = qkv[:, 3 * ATT_WIDTH + g * LANE:3 * ATT_WIDTH + (g + 1) * LANE]
        q_ref[:, sl] = (_rope_group(qg, cos, sin, first_half) * (LOG2_E * HEAD_DIM ** -0.5)).astype(BF16)
        k_ref[:, sl] = _rope_group(kg, cos, sin, first_half).astype(BF16)
        qi_ref[:, sl] = (_rope_group(ig, cos, sin, first_half) * (IDX_DIM ** -0.5)).astype(BF16)
    v_ref[...] = qkv[:, 2 * ATT_WIDTH:3 * ATT_WIDTH].astype(BF16)

    sm = _mm(hb, wsm_ref[...])
    ki = _rope_group(sm, cos, sin, first_half)
    ki_ref[...] = jnp.where(lane < IDX_DIM, ki, 0.0).astype(BF16)
    wi_ref[...] = sm * (IDX_HEADS ** -0.5)


def _l0_in(h, wglu, wqkv, wsm, cos, sin, cw, cb, lg, lb):
    lp, d = h.shape
    tm = _tile_rows(lp)
    row = lambda w: pl.BlockSpec((tm, w), lambda i: (i, 0))
    full = lambda a: pl.BlockSpec(a.shape, lambda i: (0,) * a.ndim)
    outs = [jax.ShapeDtypeStruct((lp, CONV_CH), BF16)] + [jax.ShapeDtypeStruct((lp, ATT_WIDTH), BF16)] * 4 + [
        jax.ShapeDtypeStruct((lp, LANE), BF16), jax.ShapeDtypeStruct((lp, LANE), F32)]
    return pl.pallas_call(
        _l0_in_kernel,
        grid=(lp // tm,),
        in_specs=[row(d), full(wglu), full(wqkv), full(wsm), row(LANE), row(LANE), full(cw), full(cb), full(lg),
                  full(lb)],
        out_specs=[row(CONV_CH)] + [row(ATT_WIDTH)] * 4 + [row(LANE), row(LANE)],
        out_shape=outs,
        scratch_shapes=[pltpu.VMEM((CONV_HALO + tm, CONV_CH), F32), pltpu.VMEM((tm + 8, CONV_CH), F32)],
        compiler_params=_params(("arbitrary",), 48),
        name="l0_in",
    )(h, wglu, wqkv, wsm, cos, sin, cw, cb, lg, lb)


def _two_stage_chunks(nch, first, second):
    first(0, 0)

    def body(t, carry):
        c = 2 * t
        first(c + 1, 1)
        second(c, 0, False)
        first(c + 2, 0)
        second(c + 1, 1, False)
        return carry

    n_pairs = (nch - 1) // 2
    lax.fori_loop(0, n_pairs, body, 0)
    c0 = 2 * n_pairs

    @pl.when(nch - c0 == 2)
    def _():
        first(c0 + 1, 1)
        second(c0, 0, False)
        second(c0 + 1, 1, True)

    @pl.when(nch - c0 == 1)
    def _():
        second(c0, 0, True)


def _dsa_kernel(q_ref, qi_ref, wi_ref, k_ref, v_ref, ki_ref, u_ref, o_ref,
                s_ref, qi8_ref, qh_ref, wb_ref, m_ref, acc_ref, pbuf, lohi_ref, cand_ref, thr_ref, nlo_ref,
                *, topk):
    i = pl.program_id(0)
    bq = q_ref.shape[0]
    ck = KEY_CHUNK
    ngrp = ck // LANE
    nch = ((i + 1) * bq + ck - 1) // ck
    lane = lax.broadcasted_iota(I32, (bq, LANE), 1)
    low = lane < HEAD_DIM
    chunk_rows = lambda c: pl.ds(pl.multiple_of(c * ck, ck), ck)

    for g in range(ATT_WIDTH // LANE):
        pair = qi_ref[:, g * LANE:(g + 1) * LANE].astype(F32)
        qi8_ref[pl.ds((2 * g) * bq, bq), :] = jnp.where(low, pair, 0.0).astype(BF16)
        qi8_ref[pl.ds((2 * g + 1) * bq, bq), :] = jnp.where(low, pltpu.roll(pair, HEAD_DIM, 1), 0.0).astype(BF16)
        qp = q_ref[:, g * LANE:(g + 1) * LANE].astype(F32)
        qh_ref[g, 0:bq, :] = jnp.where(low, qp, 0.0).astype(BF16)
        qh_ref[g, bq:2 * bq, :] = jnp.where(low, 0.0, qp).astype(BF16)
    wi = wi_ref[...]
    for h in range(IDX_HEADS):
        wb_ref[h] = jnp.broadcast_to(wi[:, IDX_DIM + h:IDX_DIM + h + 1], (bq, LANE))
    lohi_ref[0] = jnp.full((bq, LANE), jnp.inf, F32)
    lohi_ref[1] = jnp.full((bq, LANE), -jnp.inf, F32)

    def to_key(x):
        bits = lax.bitcast_convert_type(x, I32)
        return bits ^ ((bits >> 31) & jnp.int32(0x7FFFFFFF))

    def from_key(k):
        k = jnp.where(k > 0, jnp.where(k < MIN_NORMAL_KEY, MIN_NORMAL_KEY, k),
                      jnp.where(k >= -MIN_NORMAL_KEY, 0, k))
        return lax.bitcast_convert_type(k ^ ((k >> 31) & jnp.int32(0x7FFFFFFF)), F32)

    def index_logits(c, slot):
        pbuf[slot] = lax.dot_general(qi8_ref[...], ki_ref[chunk_rows(c), :], _NT, preferred_element_type=F32)

    cand_ref[...] = jnp.full(cand_ref.shape, -jnp.inf, F32)
    lane8 = lax.broadcasted_iota(I32, (8, LANE), 1)
    row8 = lax.broadcasted_iota(I32, (8, LANE), 0)

    def index_scores(c, slot, last):
        for slab in range(bq // 8):
            rows = slice(slab * 8, (slab + 1) * 8)
            smin = lohi_ref[0, rows, :]
            smax = lohi_ref[1, rows, :]
            lst = [cand_ref[d, rows, :] for d in range(LIST_DEPTH)]
            for j in range(ngrp):
                cols = slice(j * LANE, (j + 1) * LANE)
                sc = jnp.zeros((8, LANE), F32)
                for h in range(IDX_HEADS):
                    hrows = slice(h * bq + slab * 8, h * bq + (slab + 1) * 8)
                    sc = sc + jnp.maximum(pbuf[slot, hrows, cols], 0.0) * wb_ref[h, rows, :]
                if last:
                    valid = (c * ck + j * LANE + lane8) <= (i * bq + slab * 8 + row8)
                    smin = jnp.minimum(smin, jnp.where(valid, sc, jnp.inf))
                    sc = jnp.where(valid, sc, -jnp.inf)
                else:
                    smin = jnp.minimum(smin, sc)
                smax = jnp.maximum(smax, sc)
                s_ref[c, rows, cols] = sc
                x = sc
                for d in range(LIST_DEPTH):
                    top = jnp.maximum(lst[d], x)
                    x = jnp.minimum(lst[d], x)
                    lst[d] = top
            for d in range(LIST_DEPTH):
                cand_ref[d, rows, :] = lst[d]
            lohi_ref[0, rows, :] = smin
            lohi_ref[1, rows, :] = smax

    _two_stage_chunks(nch, index_logits, index_scores)

    def count_ge(cand):
        cand_f = from_key(cand)

        def body(c, cnt):
            blk = s_ref[c]
            for g in range(ngrp):
                cnt = cnt + jnp.where(blk[:, g * LANE:(g + 1) * LANE] >= cand_f, 1.0, 0.0)
            return cnt
        cnt = lax.fori_loop(0, nch, body, jnp.zeros((bq, LANE), F32))
        return jnp.sum(cnt, axis=1, keepdims=True)

    kf = float(topk)
    zeros = jnp.zeros((bq, LANE), F32)
    n_valid = (i * bq + lax.broadcasted_iota(I32, (bq, LANE), 0) + 1).astype(F32)
    lo0 = to_key(jnp.min(lohi_ref[0], axis=1, keepdims=True) + zeros)
    hi0 = to_key(jnp.max(lohi_ref[1], axis=1, keepdims=True) + zeros) + 1

    def open_rows(lo, hi, n_lo):
        return jnp.where(n_lo > kf, jnp.where(hi > lo + 1, 1.0, 0.0), 0.0)

    def bisect(count_fn, steps, lo_init, n_init):
        def cond(st):
            it, _, _, _, any_open = st
            return jnp.logical_and(it < 34, any_open)

        def body(st):
            it, lo, hi, n_lo, _ = st
            for _ in range(steps):
                is_open = open_rows(lo, hi, n_lo) > 0.0
                mid = (lo >> 1) + (hi >> 1) + (lo & hi & 1)
                cand = jnp.where(is_open, mid, lo)
                n = count_fn(cand) + zeros
                ge = n >= kf
                lo, hi, n_lo = (jnp.where(is_open, jnp.where(ge, cand, lo), lo),
                                jnp.where(is_open, jnp.where(ge, hi, cand), hi),
                                jnp.where(is_open, jnp.where(ge, n, n_lo), n_lo))
            return it + steps, lo, hi, n_lo, jnp.max(open_rows(lo, hi, n_lo)) > 0.0

        st = lax.while_loop(cond, body, (jnp.int32(0), lo_init, hi0, n_init,
                                         jnp.max(open_rows(lo_init, hi0, n_init)) > 0.0))
        return st[1], st[3]

    def count_ge_lists(cand):
        cand_f = from_key(cand)
        cnt = jnp.zeros((bq, LANE), F32)
        for d in range(LIST_DEPTH):
            cnt = cnt + jnp.where(cand_ref[d] >= cand_f, 1.0, 0.0)
        return jnp.sum(cnt, axis=1, keepdims=True)

    runner_up = jnp.min(cand_ref[1], axis=1, keepdims=True) + zeros
    have_two = runner_up > -jnp.inf
    lo_start = jnp.where(have_two, to_key(runner_up), lo0)
    n_start = jnp.where(have_two, count_ge_lists(lo_start) + zeros, n_valid)
    lo_l, n_l = bisect(count_ge_lists, 2, lo_start, n_start)
    thr_ref[...] = lo_l
    nlo_ref[...] = n_l
    deepest = jnp.where(cand_ref[LIST_DEPTH - 1] >= from_key(lo_l), 1.0, 0.0)
    lists_short = jnp.max(jnp.where(n_valid > kf, deepest, 0.0)) > 0.0

    @pl.when(lists_short)
    def _():
        lo_a, n_a = bisect(count_ge, 1, lo0, n_valid)
        thr_ref[...] = lo_a
        nlo_ref[...] = n_a

    n_lo = nlo_ref[...]
    take_all = n_valid <= kf
    thr_key = thr_ref[...]
    thr = jnp.where(take_all, -FLT_MAX, from_key(thr_key))

    m_ref[...] = jnp.full(m_ref.shape, M_INIT, F32)
    acc_ref[...] = jnp.zeros(acc_ref.shape, F32)
    ones = jnp.ones((ck, LANE), BF16)
    pair_rows = lambda g: slice(g * 2 * bq, (g + 1) * 2 * bq)

    def masked_scores(c, slot, bias):
        bias2 = jnp.concatenate([jnp.concatenate(bias, axis=1)] * 2, axis=0)
        for g in range(ATT_WIDTH // LANE):
            kc = k_ref[chunk_rows(c), g * LANE:(g + 1) * LANE]
            pbuf[slot, pair_rows(g), :] = lax.dot_general(qh_ref[g], kc, _NT, preferred_element_type=F32) + bias2

    def softmax_pv(c, slot, last):
        del last
        for g in range(ATT_WIDTH // LANE):
            sj = [pbuf[slot, pair_rows(g), j * LANE:(j + 1) * LANE] for j in range(ngrp)]
            mx = sj[0]
            for j in range(1, ngrp):
                mx = jnp.maximum(mx, sj[j])
            m_prev = m_ref[g]
            m_new = jnp.maximum(m_prev, jnp.max(mx, axis=1, keepdims=True))
            alpha = jnp.exp2(m_prev - m_new)
            p2 = jnp.concatenate([jnp.exp2(s - m_new) for s in sj], axis=1).astype(BF16)
            v_aug = jnp.concatenate([v_ref[chunk_rows(c), g * LANE:(g + 1) * LANE], ones], axis=1)
            acc_ref[g] = jnp.concatenate([alpha, alpha], axis=1) * acc_ref[g] + _mm(p2, v_aug)
            m_ref[g] = m_new

    has_ties = jnp.max(n_lo) > kf

    @pl.when(jnp.logical_not(has_ties))
    def _():
        def scores(c, slot):
            masked_scores(c, slot, [jnp.where(s_ref[c, :, j * LANE:(j + 1) * LANE] >= thr, 0.0, NEG_MASK)
                                    for j in range(ngrp)])

        _two_stage_chunks(nch, scores, softmax_pv)

    @pl.when(has_ties)
    def _():
        n_above = count_ge(thr_key + 1)
        need = jnp.where(take_all[:, 0:1], 0.0, kf - n_above)
        thr_b = jnp.broadcast_to(thr[:, 0:1], (bq, ck))
        need_b = jnp.broadcast_to(need, (bq, ck))

        def attend(c, ties_seen):
            sc = s_ref[c]
            eq = sc == thr_b
            prefix = _mm(jnp.where(eq, 1.0, 0.0).astype(BF16), u_ref[...]) + ties_seen
            take_tie = jnp.where(eq, jnp.where(prefix <= need_b, 0.0, NEG_MASK), NEG_MASK)
            bias = jnp.where(sc > thr_b, 0.0, take_tie)
            masked_scores(c, 0, [bias[:, j * LANE:(j + 1) * LANE] for j in range(ngrp)])
            softmax_pv(c, 0, False)
            return prefix[:, ck - 1:ck]

        lax.fori_loop(0, nch, attend, jnp.zeros((bq, 1), F32))

    for g in range(ATT_WIDTH // LANE):
        acc = acc_ref[g]
        o2 = acc[:, :LANE] * (1.0 / acc[:, LANE:])
        o_ref[:, g * LANE:(g + 1) * LANE] = jnp.where(low, o2[:bq], o2[bq:]).astype(BF16)


def _dsa(q, qi, wi, k, v, ki, topk):
    lp = q.shape[0]
    lk = k.shape[0]
    bq, ck = Q_BLOCK, KEY_CHUNK
    assert topk <= min(ck, 2 * LANE) and lk % ck == 0 and lk >= lp
    u = (np.arange(ck)[:, None] <= np.arange(ck)[None, :]).astype(np.float32)
    u = jnp.asarray(u, BF16)
    row = lambda w: pl.BlockSpec((bq, w), lambda i: (i, 0))
    res = lambda a: pl.BlockSpec(a.shape, lambda i: (0, 0), pipeline_mode=pl.Buffered(1))
    return pl.pallas_call(
        functools.partial(_dsa_kernel, topk=topk),
        grid=(lp // bq,),
        in_specs=[row(ATT_WIDTH), row(ATT_WIDTH), row(LANE), res(k), res(v), res(ki), res(u)],
        out_specs=row(ATT_WIDTH),
        out_shape=jax.ShapeDtypeStruct((lp, ATT_WIDTH), BF16),
        scratch_shapes=[
            pltpu.VMEM((lk // ck, bq, ck), F32),
            pltpu.VMEM((IDX_HEADS * bq, LANE), BF16),
            pltpu.VMEM((ATT_HEADS // 2, 2 * bq, LANE), BF16),
            pltpu.VMEM((IDX_HEADS, bq, LANE), F32),
            pltpu.VMEM((ATT_HEADS // 2, 2 * bq, LANE), F32),
            pltpu.VMEM((ATT_HEADS // 2, 2 * bq, 2 * LANE), F32),
            pltpu.VMEM((2, IDX_HEADS * bq, ck), F32),
            pltpu.VMEM((2, bq, LANE), F32),
            pltpu.VMEM((LIST_DEPTH, bq, LANE), F32),
            pltpu.VMEM((bq, LANE), I32),
            pltpu.VMEM((bq, LANE), F32),
        ],
        compiler_params=_params(("arbitrary",), 60),
        name="dsa",
    )(q, qi, wi, k, v, ki, u)


def _l0_out_kernel(a2_ref, o_ref, wa_ref, wo_ref, h_ref, g_ref, b_ref, out_ref):
    m = _mm(a2_ref[...], wa_ref[...]) + _mm(o_ref[...], wo_ref[...])
    out_ref[...] = _layer_norm(DN_ALPHA * h_ref[...] + m, g_ref[...], b_ref[...])


def _l0_out(a2, o, wa, wo, h, g, b):
    lp, d = h.shape
    tm = _tile_rows(lp)
    row = lambda w: pl.BlockSpec((tm, w), lambda i: (i, 0))
    full = lambda a: pl.BlockSpec(a.shape, lambda i: (0,) * a.ndim)
    return pl.pallas_call(
        _l0_out_kernel,
        grid=(lp // tm,),
        in_specs=[row(CONV_CH), row(ATT_WIDTH), full(wa), full(wo), row(d), full(g), full(b)],
        out_specs=row(d),
        out_shape=jax.ShapeDtypeStruct((lp, d), F32),
        compiler_params=_params(("parallel",), 32),
        name="l0_out",
    )(a2, o, wa, wo, h, g, b)


def _l1_kernel(h_ref, win_ref, cw_ref, cb_ref, wa_ref, ba_ref, wx_ref, bx_ref, lam_ref, wout_ref, g_ref, b_ref,
               out_ref, xbuf, hstate):
    i = pl.program_id(0)
    tm = h_ref.shape[0]
    h = h_ref[...]
    z = _mm(h.astype(BF16), win_ref[...])
    gate = z[:, :RNN_WIDTH]

    @pl.when(i == 0)
    def _():
        xbuf[0:RNN_HALO, :] = jnp.zeros((RNN_HALO, RNN_WIDTH), F32)
        hstate[...] = jnp.zeros(hstate.shape, F32)

    xbuf[pl.ds(RNN_HALO, tm), :] = z[:, RNN_WIDTH:]
    xc = jnp.broadcast_to(cb_ref[...], (tm, RNN_WIDTH))
    for j in range(RNN_CONV_K):
        xc = xc + cw_ref[j:j + 1, :] * xbuf[pl.ds(RNN_HALO - (RNN_CONV_K - 1) + j, tm), :]
    xbuf[0:RNN_HALO, :] = xbuf[pl.ds(tm, RNN_HALO), :]

    xcb = xc.astype(BF16)
    ra, ri = [], []
    for n in range(RNN_BLOCKS):
        blk = xcb[:, n * RNN_BLOCK_W:(n + 1) * RNN_BLOCK_W]
        ra.append(_mm(blk, wa_ref[n]))
        ri.append(_mm(blk, wx_ref[n]))
    r = jax.nn.sigmoid(jnp.concatenate(ra, axis=1) + ba_ref[...])
    ig = jax.nn.sigmoid(jnp.concatenate(ri, axis=1) + bx_ref[...])
    nl = -lam_ref[...]
    softplus = jnp.maximum(nl, 0.0) + jnp.log(1.0 + jnp.exp(-jnp.abs(nl)))
    log_a = -RG_C * r * softplus
    a = jnp.exp(log_a)
    gap = 1.0 - a * a
    u = jnp.where(gap > 0.0, gap * lax.rsqrt(gap), 0.0) * (ig * xc)

    in_group = lax.broadcasted_iota(I32, (tm, RNN_WIDTH), 0) % 8
    for d in (1, 2, 4):
        keep = in_group >= d
        a_sh = jnp.where(keep, pltpu.roll(a, d, 0), 1.0)
        u_sh = jnp.where(keep, pltpu.roll(u, d, 0), 0.0)
        u = a * u_sh + u
        a = a * a_sh
    h_prev = hstate[0:1, :]
    groups = []
    for g in range(tm // 8):
        h_g = u[g * 8:(g + 1) * 8, :] + a[g * 8:(g + 1) * 8, :] * h_prev
        groups.append(h_g)
        h_prev = h_g[7:8, :]
    hs = jnp.concatenate(groups, axis=0)
    hstate[0:1, :] = h_prev

    y = (jax.nn.gelu(gate) * hs).astype(BF16)
    m = _mm(y, wout_ref[...])
    out_ref[...] = _layer_norm(DN_ALPHA * h + m, g_ref[...], b_ref[...])


def _l1_mixer(h, win, cw, cb, wa, ba, wx, bx, lam, wout, g, b):
    lp, d = h.shape
    tm = _tile_rows(lp)
    row = lambda w: pl.BlockSpec((tm, w), lambda i: (i, 0))
    full = lambda a: pl.BlockSpec(a.shape, lambda i: (0,) * a.ndim)
    args = (h, win, cw, cb, wa, ba, wx, bx, lam, wout, g, b)
    return pl.pallas_call(
        _l1_kernel,
        grid=(lp // tm,),
        in_specs=[row(d)] + [full(a) for a in args[1:]],
        out_specs=row(d),
        out_shape=jax.ShapeDtypeStruct((lp, d), F32),
        scratch_shapes=[pltpu.VMEM((RNN_HALO + tm, RNN_WIDTH), F32), pltpu.VMEM((8, RNN_WIDTH), F32)],
        compiler_params=_params(("arbitrary",), 56),
        name="l1_mixer",
    )(*args)


def _router_kernel(h_ref, whi_ref, wlo_ref, br_ref, ltri_ref, eid_ref, gate_ref, rank_ref, cnt_ref, carry_ref,
                   *, n_real):
    i = pl.program_id(0)
    tm = h_ref.shape[0]

    @pl.when(i == 0)
    def _():
        carry_ref[...] = jnp.zeros(carry_ref.shape, F32)

    h = h_ref[...]
    h_hi = h.astype(BF16)
    h_lo = (h - h_hi.astype(F32)).astype(BF16)
    logits = _mm(h_hi, whi_ref[...]) + _mm(h_lo, whi_ref[...]) + _mm(h_hi, wlo_ref[...]) + br_ref[...]
    lane = lax.broadcasted_iota(I32, (tm, LANE), 1).astype(F32)
    ninf = -jnp.inf
    big = float(LANE)

    gl = jnp.where(lane < N_GROUPS, logits[:, :LANE], ninf)
    gmax = jnp.max(gl, axis=1, keepdims=True)
    g_p = 1.0 / jnp.sum(jnp.exp(gl - gmax), axis=1, keepdims=True)
    g_idx = jnp.min(jnp.where(gl == gmax, lane, big), axis=1, keepdims=True)

    first = g_idx * EXPERTS_PER_GROUP
    el = logits[:, LANE:]
    m1 = jnp.where(lane >= first, jnp.where(lane < first + EXPERTS_PER_GROUP, el, ninf), ninf)
    t1 = jnp.max(m1, axis=1, keepdims=True)
    i1 = jnp.min(jnp.where(m1 == t1, lane, big), axis=1, keepdims=True)
    m2 = jnp.where(lane == i1, ninf, m1)
    t2 = jnp.max(m2, axis=1, keepdims=True)
    i2 = jnp.min(jnp.where(m2 == t2, lane, big), axis=1, keepdims=True)
    e2 = jnp.exp(t2 - t1)
    den = 1.0 / (1.0 + e2)

    tok = i * tm + lax.broadcasted_iota(I32, (tm, LANE), 0)
    valid = tok < n_real
    oh0 = jnp.where(valid, jnp.where(lane == i1, 1.0, 0.0), 0.0)
    oh1 = jnp.where(valid, jnp.where(lane == i2, 1.0, 0.0), 0.0)
    ohs = oh0 + oh1
    before = _mm(ltri_ref[...], ohs.astype(BF16)) + carry_ref[...]
    carry_ref[...] = carry_ref[...] + jnp.sum(ohs, axis=0, keepdims=True)
    cnt_ref[...] = carry_ref[...]

    eid_ref[:, 0:1] = i1.astype(I32)
    eid_ref[:, 1:2] = i2.astype(I32)
    gate_ref[:, 0:1] = g_p * den
    gate_ref[:, 1:2] = g_p * e2 * den
    rank_ref[:, 0:1] = jnp.sum(oh0 * before, axis=1, keepdims=True).astype(I32)
    rank_ref[:, 1:2] = jnp.sum(oh1 * before, axis=1, keepdims=True).astype(I32)


def _router(h, wr, br, n_real):
    lp, d = h.shape
    tm = _tile_rows(lp)
    w_hi = wr.astype(BF16)
    w_lo = (wr - w_hi.astype(F32)).astype(BF16)
    ltri = jnp.asarray((np.arange(tm)[:, None] > np.arange(tm)[None, :]).astype(np.float32), BF16)
    row = lambda w: pl.BlockSpec((tm, w), lambda i: (i, 0))
    full = lambda a: pl.BlockSpec(a.shape, lambda i: (0,) * a.ndim)
    return pl.pallas_call(
        functools.partial(_router_kernel, n_real=n_real),
        grid=(lp // tm,),
        in_specs=[row(d), full(w_hi), full(w_lo), full(br), full(ltri)],
        out_specs=[row(2), row(2), row(2), pl.BlockSpec((1, LANE), lambda i: (0, 0))],
        out_shape=[jax.ShapeDtypeStruct((lp, 2), I32), jax.ShapeDtypeStruct((lp, 2), F32),
                   jax.ShapeDtypeStruct((lp, 2), I32), jax.ShapeDtypeStruct((1, LANE), F32)],
        scratch_shapes=[pltpu.VMEM((1, LANE), F32)],
        compiler_params=_params(("arbitrary",), 32),
        name="moe_router",
    )(h, w_hi, w_lo, br, ltri)


def _row_copy(src_ref, src_row, dst_ref, dst_row, sem):
    return pltpu.make_async_copy(src_ref.at[pl.ds(src_row, 1), :], dst_ref.at[pl.ds(dst_row, 1), :], sem)


def _dispatch_kernel(dest_ref, h_ref, xb_in_ref, xb_ref, packed, sem):
    del xb_in_ref
    i = pl.program_id(0)
    tm = h_ref.shape[0]
    half = i % 2
    packed[half] = _pack_bf16_pairs(h_ref[...])

    def issue(r8, carry):
        for u in range(ROW_DMA_UNROLL):
            r = r8 * ROW_DMA_UNROLL + u
            for s in range(2):
                _row_copy(packed.at[half], r, xb_ref, dest_ref[2 * (i * tm + r) + s], sem.at[half]).start()
        return carry

    lax.fori_loop(0, tm // ROW_DMA_UNROLL, issue, 0)

    def drain(which):
        for s in range(2):
            pltpu.make_async_copy(packed.at[which], xb_ref.at[pl.ds(0, tm), :], sem.at[which]).wait()

    @pl.when(i > 0)
    def _():
        drain(1 - half)

    @pl.when(i == pl.num_programs(0) - 1)
    def _():
        drain(half)


def _dispatch(dest_flat, h, n_rows_out):
    lp, d = h.shape
    tm = _tile_rows(lp)
    xb0 = jnp.zeros((n_rows_out, d // 2), I32)
    grid_spec = pltpu.PrefetchScalarGridSpec(
        num_scalar_prefetch=1,
        grid=(lp // tm,),
        in_specs=[pl.BlockSpec((tm, d), lambda i, dest: (i, 0)), pl.BlockSpec(memory_space=pl.ANY)],
        out_specs=pl.BlockSpec(memory_space=pl.ANY),
        scratch_shapes=[pltpu.VMEM((2, tm, d // 2), I32), pltpu.SemaphoreType.DMA((2,))],
    )
    return pl.pallas_call(
        _dispatch_kernel,
        grid_spec=grid_spec,
        out_shape=jax.ShapeDtypeStruct((n_rows_out, d // 2), I32),
        input_output_aliases={2: 0},
        compiler_params=pltpu.CompilerParams(dimension_semantics=("arbitrary",), has_side_effects=True),
        name="moe_dispatch",
    )(dest_flat, h, xb0)


def _experts_kernel(bexp_ref, nused_ref, xb_ref, wg_ref, wu_ref, wd_ref, yb_ref, wg_s, wu_s, wd_s):
    b = pl.program_id(0)
    prev = bexp_ref[jnp.maximum(b - 1, 0)]

    @pl.when((b == 0) | (bexp_ref[b] != prev))
    def _():
        wg_s[...] = wg_ref[...].astype(BF16)
        wu_s[...] = wu_ref[...].astype(BF16)
        wd_s[...] = wd_ref[...].astype(BF16)

    @pl.when(b < nused_ref[0])
    def _():
        x = _unpack_bf16_pairs(xb_ref[...]).astype(BF16)
        gt = _mm(x, wg_s[...])
        up = _mm(x, wu_s[...])
        mid = (gt * jax.nn.sigmoid(gt) * up).astype(BF16)
        yb_ref[...] = _pack_bf16_pairs(_mm(mid, wd_s[...]))

    @pl.when(b >= nused_ref[0])
    def _():
        yb_ref[...] = jnp.zeros(yb_ref.shape, I32)


def _experts(bexp, nused, xb, wg, wu, wd, layer, n_blocks):
    d = xb.shape[1]
    bm = EXPERT_ROWS
    wspec = lambda a: pl.BlockSpec((None, None) + a.shape[2:], lambda b, bexp, *_: (layer, bexp[b], 0, 0))
    grid_spec = pltpu.PrefetchScalarGridSpec(
        num_scalar_prefetch=2,
        grid=(n_blocks,),
        in_specs=[pl.BlockSpec((bm, d), lambda b, *_: (b, 0)), wspec(wg), wspec(wu), wspec(wd)],
        out_specs=pl.BlockSpec((bm, d), lambda b, *_: (b, 0)),
        scratch_shapes=[pltpu.VMEM(wg.shape[2:], BF16), pltpu.VMEM(wu.shape[2:], BF16),
                        pltpu.VMEM(wd.shape[2:], BF16)],
    )
    return pl.pallas_call(
        _experts_kernel,
        grid_spec=grid_spec,
        out_shape=jax.ShapeDtypeStruct((n_blocks * bm, d), I32),
        compiler_params=_params(("arbitrary",), 48),
        name="moe_experts",
    )(bexp, nused, xb, wg, wu, wd)


def _combine_kernel(src_ref, yb_ref, gate_ref, h_ref, g_ref, b_ref, out_ref, ybuf, sem):
    i = pl.program_id(0)
    tm = h_ref.shape[0]

    def gather_tile(tile, half):
        def issue(r8, carry):
            for u in range(ROW_DMA_UNROLL):
                r = r8 * ROW_DMA_UNROLL + u
                for s in range(2):
                    _row_copy(yb_ref, src_ref[2 * (tile * tm + r) + s], ybuf.at[half, s], r, sem.at[half]).start()
            return carry

        lax.fori_loop(0, tm // ROW_DMA_UNROLL, issue, 0)

    @pl.when(i == 0)
    def _():
        gather_tile(0, 0)

    @pl.when(i + 1 < pl.num_programs(0))
    def _():
        gather_tile(i + 1, (i + 1) % 2)

    half = i % 2
    for s in range(2):
        pltpu.make_async_copy(yb_ref.at[pl.ds(0, tm), :], ybuf.at[half, s], sem.at[half]).wait()
    gate = gate_ref[...]
    y = gate[:, 0:1] * _unpack_bf16_pairs(ybuf[half, 0]) + gate[:, 1:2] * _unpack_bf16_pairs(ybuf[half, 1])
    out_ref[...] = _layer_norm(DN_ALPHA * h_ref[...] + y, g_ref[...], b_ref[...])


def _combine(dest_flat, yb, gate, h, g, b):
    lp, d = h.shape
    tm = _tile_rows(lp)
    full = lambda a: pl.BlockSpec(a.shape, lambda i, dest: (0,) * a.ndim)
    grid_spec = pltpu.PrefetchScalarGridSpec(
        num_scalar_prefetch=1,
        grid=(lp // tm,),
        in_specs=[pl.BlockSpec(memory_space=pl.ANY), pl.BlockSpec((tm, 2), lambda i, dest: (i, 0)),
                  pl.BlockSpec((tm, d), lambda i, dest: (i, 0)), full(g), full(b)],
        out_specs=pl.BlockSpec((tm, d), lambda i, dest: (i, 0)),
        scratch_shapes=[pltpu.VMEM((2, 2, tm, d // 2), I32), pltpu.SemaphoreType.DMA((2,))],
    )
    return pl.pallas_call(
        _combine_kernel,
        grid_spec=grid_spec,
        out_shape=jax.ShapeDtypeStruct((lp, d), F32),
        compiler_params=_params(("arbitrary",), 32),
        name="moe_combine",
    )(dest_flat, yb, gate, h, g, b)


def _moe(h, n_real, layer, wg, bg, we, be, w_gate, w_up, w_down, ln_g, ln_b):
    lp, d = h.shape
    bm = EXPERT_ROWS
    wr = jnp.zeros((d, 2 * LANE), F32).at[:, :N_GROUPS].set(wg).at[:, LANE:LANE + N_EXPERTS].set(we)
    br = jnp.zeros((1, 2 * LANE), F32).at[0, :N_GROUPS].set(bg).at[0, LANE:LANE + N_EXPERTS].set(be)
    eid, gate, rank, cnt = _router(h, wr, br, n_real)

    counts = cnt[0, :N_EXPERTS].astype(I32)
    padded = (counts + bm - 1) // bm * bm
    pend = jnp.cumsum(padded)
    pstart = pend - padded
    n_blocks = -(-(2 * n_real + N_EXPERTS * (bm - 1)) // bm)
    cap = n_blocks * bm
    tok = jnp.arange(lp, dtype=I32)[:, None]
    valid = tok < n_real
    experts = jnp.arange(N_EXPERTS, dtype=I32)
    row = jnp.sum(jnp.where(eid[:, :, None] == experts, pstart, 0), axis=-1) + rank
    dest_flat = jnp.where(valid, row, cap + 2 * (tok - n_real) + jnp.arange(2, dtype=I32)[None, :]).reshape(-1)
    src_flat = jnp.where(valid, row, 0).reshape(-1)
    block_start = jnp.arange(n_blocks, dtype=I32) * bm
    bexp = jnp.minimum(jnp.sum((pend[None, :] <= block_start[:, None]).astype(I32), axis=1), N_EXPERTS - 1)
    nused = (pend[-1:] // bm).astype(I32)

    xb = _dispatch(dest_flat, h, cap + 2 * (lp - n_real))
    yb = _experts(bexp, nused, xb, w_gate, w_up, w_down, layer, n_blocks)
    return _combine(src_flat, yb, gate, h, ln_g, ln_b)


def kernel(x, meta_tokens, ab_w_in, ab_conv_w, ab_conv_b, ab_ln_g, ab_ln_b, ab_w_out, c_w_in, c_conv_w, c_conv_b, c_gate_a_w, c_gate_a_b, c_gate_x_w, c_gate_x_b, c_lambda, c_w_out, moe_router_group_w, moe_router_group_b, moe_router_expert_w, moe_router_expert_b, moe_w_gate, moe_w_up, moe_w_down, ln_mix_g, ln_mix_b, ln_ffn_g, ln_ffn_b):
    bsz, seq, d = x.shape
    assert bsz == 1, "kernel is written for batch 1"
    n_real = N_META + seq
    lp = -(-n_real // Q_BLOCK) * Q_BLOCK
    lk = -(-lp // KEY_CHUNK) * KEY_CHUNK
    topk = min(TOPK_MAX, seq // 4)
    row2 = lambda a: a.reshape(1, -1)

    h = jnp.concatenate([meta_tokens.astype(x.dtype), x[0], jnp.zeros((lp - n_real, d), x.dtype)], axis=0)

    half = HEAD_DIM // 2
    inv_freq = (np.float32(ROPE_THETA) ** (np.float32(-2.0) * np.arange(half, dtype=np.float32)
                                           / np.float32(HEAD_DIM))).astype(np.float32)
    ang = (np.arange(lp, dtype=np.float32)[:, None] * inv_freq[None, :]).astype(np.float64)
    cos = jnp.tile(jnp.asarray(np.cos(ang), F32), (1, 4))
    sin_half = jnp.asarray(np.sin(ang), F32)
    sin = jnp.tile(jnp.concatenate([-sin_half, sin_half], axis=1), (1, 2))

    for layer in range(DEPTH):
        j = layer // 2
        if layer % 2 == 0:
            w_in = ab_w_in[j]
            wglu = w_in[:, :2 * CONV_CH].astype(BF16)
            wqkv = w_in[:, 2 * CONV_CH:2 * CONV_CH + 4 * ATT_WIDTH].astype(BF16)
            wsm = jnp.zeros((d, LANE), F32).at[:, :IDX_DIM + IDX_HEADS].set(
                w_in[:, 2 * CONV_CH + 4 * ATT_WIDTH:]).astype(BF16)
            a2, q, k, v, qi, ki, wi = _l0_in(h, wglu, wqkv, wsm, cos, sin, ab_conv_w[j], row2(ab_conv_b[j]),
                                             row2(ab_ln_g[j]), row2(ab_ln_b[j]))
            pad = lambda t: jnp.pad(t, ((0, lk - lp), (0, 0)))
            o = _dsa(q, qi, wi, pad(k), pad(v), pad(ki), topk)
            w_out = ab_w_out[j].astype(BF16)
            h = _l0_out(a2, o, w_out[:CONV_CH], w_out[CONV_CH:], h, row2(ln_mix_g[layer]), row2(ln_mix_b[layer]))
        else:
            h = _l1_mixer(h, c_w_in[j].astype(BF16), c_conv_w[j], row2(c_conv_b[j]),
                          c_gate_a_w[j].astype(BF16), row2(c_gate_a_b[j]),
                          c_gate_x_w[j].astype(BF16), row2(c_gate_x_b[j]), row2(c_lambda[j]),
                          c_w_out[j].astype(BF16), row2(ln_mix_g[layer]), row2(ln_mix_b[layer]))
        h = _moe(h, n_real, layer, moe_router_group_w[layer], moe_router_group_b[layer],
                 moe_router_expert_w[layer], moe_router_expert_b[layer], moe_w_gate, moe_w_up, moe_w_down,
                 row2(ln_ffn_g[layer]), row2(ln_ffn_b[layer]))
    return h[N_META:n_real][None]
```
